```python
import math
import jax
import jax.numpy as jnp
from jax import lax
import numpy as np

D_MODEL = 2048
BATCH = 4
SEQ = 2048
DEPTH = 4
DEC_BATCH = 8
DEC_SEQ = 1
PAST_LEN = 16384
PAGE_SIZE = 128

N_MIXERS = 4
NORM_EPS = 1e-6
D_FF = 5632
Q_BLOCK = 128
N_MEM = 256
MEM_HEADS = 4
MEM_HD = D_MODEL // MEM_HEADS
RWKV_HD = 64
RWKV_HEADS = D_MODEL // RWKV_HD
DECAY_LORA = 96
AAA_LORA = 96
GATE_LORA = 256
GN_EPS = 64e-5
SB_HD = 128
SB_HEADS = D_MODEL // SB_HD
SB_BIAS_LO = -10.0
SB_BIAS_HI = -4.0
DIL_PATTERNS = ((128, 1), (512, 4), (2048, 16))
DIL_HD = 128
DIL_HEADS = 8
DIL_BAND = 128
ROPE_THETA = 500000.0
ROPE_DIM = DIL_HD // 4
SSM_DINNER = 2 * D_MODEL
SSM_HEADDIM = 64
SSM_HEADS = SSM_DINNER // SSM_HEADDIM
SSM_STATE = 128
SSM_GROUPS = 8
SSM_CONV = 4
SSM_CONV_DIM = SSM_DINNER + 2 * SSM_GROUPS * SSM_STATE
SSM_CHUNK = 128
SSM_NORM_EPS = 1e-5

kernel_name = 'hybrid_rwkv7_stickbreak_dilated_ssd_decoder_step'


def rms_norm(x, g, eps=NORM_EPS):
    xf = x.astype(jnp.float32)
    y = xf * lax.rsqrt(jnp.mean(xf * xf, axis=-1, keepdims=True) + eps)
    return (y * g.astype(jnp.float32)).astype(x.dtype)


def swiglu(x, w_gate, w_up, w_down):
    return (jax.nn.silu(x @ w_gate) * (x @ w_up)) @ w_down


def rope_partial(x, pos):
    half = ROPE_DIM // 2
    inv_freq = ROPE_THETA ** (-jnp.arange(half, dtype=jnp.float32) / half)
    ang = pos.astype(jnp.float32)[:, None] * inv_freq[None, :]
    bshape = (1, pos.shape[0]) + (1,) * (x.ndim - 3) + (half,)
    cos = jnp.cos(ang).reshape(bshape)
    sin = jnp.sin(ang).reshape(bshape)
    xf = x.astype(jnp.float32)
    x1, x2, rest = xf[..., :half], xf[..., half:ROPE_DIM], xf[..., ROPE_DIM:]
    return jnp.concatenate([x1 * cos - x2 * sin, x2 * cos + x1 * sin, rest], axis=-1).astype(x.dtype)


def memory_kv(mem, w_k, g_k, w_v):
    b = mem.shape[0]
    k = rms_norm((mem @ w_k).reshape(b, -1, MEM_HEADS, MEM_HD), g_k)
    v = (mem @ w_v).reshape(b, -1, MEM_HEADS, MEM_HD)
    return k, v


def memory_attend(xn, w_q, g_q, mem_k, mem_v, w_o):
    b, t, _ = xn.shape
    q = rms_norm((xn @ w_q).reshape(b, t, MEM_HEADS, MEM_HD), g_q)
    s = jnp.einsum('bthe,bmhe->bhtm', q, mem_k).astype(jnp.float32) * (MEM_HD ** -0.5)
    p = jax.nn.softmax(s, axis=-1).astype(mem_v.dtype)
    o = jnp.einsum('bhtm,bmhe->bthe', p, mem_v).reshape(b, t, MEM_HEADS * MEM_HD)
    return o.astype(xn.dtype) @ w_o


def rwkv7_time_mix(xn, shift_prev, wkv_prev, mu, w_r, w_k, w_v, w_o, w0, w1, w2,
                   a0, a1, a2, g1, g2, k_k, k_a, r_k, gn_w, gn_b):
    b, t, d = xn.shape
    f32 = jnp.float32
    x_prev = jnp.concatenate([shift_prev[:, None].astype(xn.dtype), xn[:, :-1]], axis=1)
    xx = x_prev - xn
    xr, xw, xk, xv, xa, xg = [xn + xx * mu[i] for i in range(6)]
    heads = lambda z: z.astype(f32).reshape(b, t, RWKV_HEADS, RWKV_HD)
    r = heads(xr @ w_r)
    w_log = -jax.nn.softplus(-(w0 + jnp.tanh(xw @ w1) @ w2).astype(f32)) - 0.5
    decay = heads(jnp.exp(-jnp.exp(w_log)))
    k = xk @ w_k
    v = heads(xv @ w_v)
    a = jax.nn.sigmoid((a0 + (xa @ a1) @ a2).astype(f32))
    g = jax.nn.sigmoid(xg @ g1) @ g2
    kk = heads(k * k_k)
    kk = kk * lax.rsqrt(jnp.maximum(jnp.sum(kk * kk, axis=-1, keepdims=True), 1e-24))
    k = heads(k.astype(f32) * (1.0 + (a - 1.0) * k_a))
    a = heads(a)

    def step(S, inp):
        r_t, w_t, k_t, v_t, kk_t, a_t = inp
        sa = jnp.einsum('bhij,bhj->bhi', S, -kk_t)
        S = (S * w_t[:, :, None, :] + sa[..., None] * (kk_t * a_t)[:, :, None, :]
             + v_t[..., None] * k_t[:, :, None, :])
        return S, jnp.einsum('bhij,bhj->bhi', S, r_t)

    tmaj = lambda z: jnp.moveaxis(z, 1, 0)
    wkv_fin, o = lax.scan(step, wkv_prev.astype(f32),
                          (tmaj(r), tmaj(decay), tmaj(k), tmaj(v), tmaj(kk), tmaj(a)))
    o = jnp.moveaxis(o, 0, 1)
    mean = jnp.mean(o, axis=-1, keepdims=True)
    var = jnp.mean(jnp.square(o - mean), axis=-1, keepdims=True)
    o = ((o - mean) * lax.rsqrt(var + GN_EPS)).reshape(b, t, d) * gn_w + gn_b
    o = o + (jnp.sum(r * k * r_k, axis=-1, keepdims=True) * v).reshape(b, t, d)
    out = (o * g.astype(f32)).astype(xn.dtype) @ w_o
    return out, xn[:, -1], wkv_fin


def stick_breaking(q, q_pos, k_segs, v_segs, k_pos, bias):
    b, t, h, e = q.shape
    blk = Q_BLOCK if t % Q_BLOCK == 0 else t
    nb = t // blk
    q_blocks = jnp.moveaxis(q.reshape(b, nb, blk, h, e), 1, 0)
    qp_blocks = q_pos.reshape(nb, blk)
    bounds = np.cumsum([0] + [ks.shape[1] for ks in k_segs])
    bias_f = bias.astype(jnp.float32)[None, :, None, None]

    def one_block(args):
        qb, qp = args
        z = jnp.concatenate([jnp.einsum('bqhe,bkhe->bhqk', qb, ks) for ks in k_segs], axis=-1)
        z = z.astype(jnp.float32) * (e ** -0.5) + bias_f
        causal = (k_pos[None, :] < qp[:, None])[None, None]
        log_keep = jnp.where(causal, jax.nn.log_sigmoid(-z), 0.0)
        between = lax.cumsum(log_keep, axis=3, reverse=True) - log_keep
        att = jnp.where(causal, jnp.exp(jax.nn.log_sigmoid(z) + between), 0.0).astype(v_segs[0].dtype)
        o = jnp.einsum('bhqk,bkhe->bqhe', att[..., bounds[0]:bounds[1]], v_segs[0])
        for i in range(1, len(v_segs)):
            o = o + jnp.einsum('bhqk,bkhe->bqhe', att[..., bounds[i]:bounds[i + 1]], v_segs[i])
        return o

    o = lax.map(one_block, (q_blocks, qp_blocks))
    return jnp.moveaxis(o, 0, 1).reshape(b, t, h * e)


def dilated_prompt_group(q, k, v, dil):
    b, s, h, e = q.shape
    f32 = jnp.float32
    L = s // dil
    nb = -(-L // Q_BLOCK)
    Lp = nb * Q_BLOCK

    def by_residue(z):
        z = jnp.swapaxes(z.reshape(b, L, dil, h, e), 1, 2)
        return jnp.pad(z, ((0, 0), (0, 0), (0, Lp - L), (0, 0), (0, 0)))

    def band(z):
        zp = jnp.pad(z, ((0, 0), (0, 0), (Q_BLOCK, 0), (0, 0), (0, 0)))
        prev = zp[:, :, :Lp].reshape(b, dil, nb, Q_BLOCK, h, e)
        cur = zp[:, :, Q_BLOCK:].reshape(b, dil, nb, Q_BLOCK, h, e)
        return jnp.concatenate([prev, cur], axis=3)

    qb = by_residue(q).reshape(b, dil, nb, Q_BLOCK, h, e)
    kb = band(by_residue(k))
    vb = band(by_residue(v))
    i = jnp.arange(Q_BLOCK)[:, None]
    j = jnp.arange(2 * Q_BLOCK)[None, :]
    key_sub = (jnp.arange(nb) * Q_BLOCK - Q_BLOCK)[:, None, None] + j[None]
    valid = (j >= i)[None] & (j <= i + DIL_BAND)[None] & (key_sub >= 0)
    sc = jnp.einsum('bdnqhe,bdnkhe->bdnhqk', qb, kb).astype(f32) * (e ** -0.5)
    sc = jnp.where(valid[None, None, :, None], sc, -jnp.inf)
    m = jnp.max(sc, axis=-1, keepdims=True)
    p = jnp.exp(sc - m)
    den = jnp.sum(p, axis=-1)
    o = jnp.einsum('bdnhqk,bdnkhe->bdnqhe', p, vb.astype(f32)) / jnp.moveaxis(den, -1, -2)[..., None]
    lse = jnp.moveaxis(m[..., 0] + jnp.log(den), -1, -2)

    def back(z):
        z = z.reshape((b, dil, Lp) + z.shape[4:])[:, :, :L]
        return jnp.swapaxes(z, 1, 2).reshape((b, s) + z.shape[3:])

    return back(o), back(lse)


def dilated_sample_group(q, k_buf, v_buf, k_new, v_new, dil):
    b, t, h, e = q.shape
    f32 = jnp.float32
    lb = k_buf.shape[1]
    k_all = jnp.concatenate([k_buf.astype(k_new.dtype), k_new], axis=1)
    v_all = jnp.concatenate([v_buf.astype(v_new.dtype), v_new], axis=1)
    idx = lb + jnp.arange(t)[:, None] - dil * jnp.arange(DIL_BAND + 1)[None, :]
    valid = idx >= 0
    idx = jnp.maximum(idx, 0)
    kg = k_all[:, idx]
    vg = v_all[:, idx]
    sc = jnp.einsum('bthe,btmhe->bthm', q, kg).astype(f32) * (e ** -0.5)
    sc = jnp.where(valid[None, :, None, :], sc, -jnp.inf)
    m = jnp.max(sc, axis=-1, keepdims=True)
    p = jnp.exp(sc - m)
    den = jnp.sum(p, axis=-1)
    o = jnp.einsum('bthm,btmhe->bthe', p, vg.astype(f32)) / den[..., None]
    return o, m[..., 0] + jnp.log(den)


def ssd_chunked(xs, dt, a, bm, cm, h0):
    b, t, h, p = xs.shape
    G, N = SSM_GROUPS, SSM_STATE
    R = h // G
    Q = SSM_CHUNK if t % SSM_CHUNK == 0 else t
    nc = t // Q
    f32 = jnp.float32

    def chunks(z):
        return jnp.moveaxis(z.astype(f32).reshape((b, nc, Q) + z.shape[2:]), 1, 0)

    a_gr = a.reshape(G, R)
    tri = jnp.tril(jnp.ones((Q, Q), dtype=bool))[None, :, :, None, None]

    def step(hs, inp):
        xc, dtc, bc, cc = inp
        cum = jnp.cumsum(dtc * a_gr, axis=1)
        seg = cum[:, :, None] - cum[:, None, :]
        decay = jnp.where(tri, jnp.exp(jnp.where(tri, seg, 0.0)), 0.0)
        cb = jnp.einsum('btgn,bsgn->btsg', cc, bc)
        y = jnp.einsum('btsg,btsgr,bsgrp->btgrp', cb, decay * dtc[:, None], xc)
        y = y + jnp.einsum('btgn,bgrpn->btgrp', cc, hs) * jnp.exp(cum)[..., None]
        last = cum[:, -1]
        tail = jnp.exp(last[:, None] - cum) * dtc
        hs = hs * jnp.exp(last)[..., None, None] + jnp.einsum('bsgr,bsgrp,bsgn->bgrpn', tail, xc, bc)
        return hs, y

    h_fin, y = lax.scan(step, h0.astype(f32).reshape(b, G, R, p, N),
                        (chunks(xs.reshape(b, t, G, R, p)), chunks(dt.reshape(b, t, G, R)),
                         chunks(bm), chunks(cm)))
    return jnp.moveaxis(y, 0, 1).reshape(b, t, h, p), h_fin.reshape(b, h, p, N)


def mamba2_mix(xn, conv_prev, h_prev, w_in, conv_w, conv_b, dt_bias, a_log, d_skip, norm_w, w_out):
    b, t, _ = xn.shape
    f32 = jnp.float32
    proj = xn @ w_in
    z = proj[..., :SSM_DINNER]
    xbc = proj[..., SSM_DINNER:SSM_DINNER + SSM_CONV_DIM]
    dt_raw = proj[..., SSM_DINNER + SSM_CONV_DIM:]
    xpad = jnp.concatenate([conv_prev.astype(xbc.dtype), xbc], axis=1)
    conv = conv_b + xpad[:, 0:t] * conv_w[0]
    for j in range(1, SSM_CONV):
        conv = conv + xpad[:, j:j + t] * conv_w[j]
    xbc = jax.nn.silu(conv)
    gn = SSM_GROUPS * SSM_STATE
    xs = xbc[..., :SSM_DINNER].reshape(b, t, SSM_HEADS, SSM_HEADDIM)
    bm = xbc[..., SSM_DINNER:SSM_DINNER + gn].reshape(b, t, SSM_GROUPS, SSM_STATE)
    cm = xbc[..., SSM_DINNER + gn:].reshape(b, t, SSM_GROUPS, SSM_STATE)
    dt = jax.nn.softplus(dt_raw.astype(f32) + dt_bias)
    a = -jnp.exp(a_log.astype(f32))
    y, h_fin = ssd_chunked(xs, dt, a, bm, cm, h_prev)
    y = y + xs.astype(f32) * d_skip[:, None]
    y = y.reshape(b, t, SSM_DINNER) * jax.nn.silu(z.astype(f32))
    yg = y.reshape(b, t, SSM_GROUPS, SSM_DINNER // SSM_GROUPS)
    yg = yg * lax.rsqrt(jnp.mean(yg * yg, axis=-1, keepdims=True) + SSM_NORM_EPS)
    y = yg.reshape(b, t, SSM_DINNER) * norm_w
    return y.astype(xn.dtype) @ w_out, xpad[:, t:], h_fin


def setup_inputs(seed: int = 0) -> dict:
    key = jax.random.key(seed)
    ks = iter(jax.random.split(key, 128))
    f32 = jnp.float32

    def nrm(shape, scale=1.0):
        return scale * jax.random.normal(next(ks), shape, f32)

    def gain(shape):
        return 1.0 + nrm(shape, 0.02)

    def unif(shape, lo, hi):
        return jax.random.uniform(next(ks), shape, f32, lo, hi)

    n_pages = PAST_LEN // PAGE_SIZE
    n_used = DEC_BATCH * n_pages
    n_pool = n_used + (n_used + 3) // 4
    page_table = jax.random.permutation(next(ks), n_pool)[:n_used].reshape(DEC_BATCH, n_pages).astype(jnp.int32)
    n_grp = len(DIL_PATTERNS)
    dil_len = [min(w, PAST_LEN) for w, _ in DIL_PATTERNS]
    dt0 = jnp.exp(unif((SSM_HEADS,), math.log(1e-3), math.log(1e-1)))
    a_init = unif((SSM_HEADS,), 1.0, 16.0)
    sd = D_MODEL ** -0.5
    mem_w = MEM_HEADS * MEM_HD
    return {
        'x_prompt': nrm((BATCH, SEQ, D_MODEL)),
        'x_sample': nrm((DEC_BATCH, DEC_SEQ, D_MODEL)),
        'state_rwkv_shift': nrm((DEC_BATCH, D_MODEL)),
        'state_rwkv_wkv': nrm((DEC_BATCH, RWKV_HEADS, RWKV_HD, RWKV_HD), 0.3),
        'cache_sb_k': nrm((n_pool, PAGE_SIZE, SB_HEADS, SB_HD)),
        'cache_sb_v': nrm((n_pool, PAGE_SIZE, SB_HEADS, SB_HD)),
        'cache_dil0_k': nrm((DEC_BATCH, dil_len[0], DIL_HEADS, DIL_HD)),
        'cache_dil0_v': nrm((DEC_BATCH, dil_len[0], DIL_HEADS, DIL_HD)),
        'cache_dil1_k': nrm((DEC_BATCH, dil_len[1], DIL_HEADS, DIL_HD)),
        'cache_dil1_v': nrm((DEC_BATCH, dil_len[1], DIL_HEADS, DIL_HD)),
        'cache_dil2_k': nrm((DEC_BATCH, dil_len[2], DIL_HEADS, DIL_HD)),
        'cache_dil2_v': nrm((DEC_BATCH, dil_len[2], DIL_HEADS, DIL_HD)),
        'state_ssm_conv': nrm((DEC_BATCH, SSM_CONV - 1, SSM_CONV_DIM)),
        'state_ssm_h': nrm((DEC_BATCH, SSM_HEADS, SSM_HEADDIM, SSM_STATE), 0.1),
        'cache_mem_k': nrm((DEPTH, DEC_BATCH, N_MEM, MEM_HEADS, MEM_HD)),
        'cache_mem_v': nrm((DEPTH, DEC_BATCH, N_MEM, MEM_HEADS, MEM_HD)),
        'page_table': page_table,
        'mem_prompt': nrm((BATCH, N_MEM, D_MODEL)),
        'ln_ffn1': gain((DEPTH, D_MODEL)),
        'ffn1_gate': nrm((DEPTH, D_MODEL, D_FF), sd),
        'ffn1_up': nrm((DEPTH, D_MODEL, D_FF), sd),
        'ffn1_down': nrm((DEPTH, D_FF, D_MODEL), D_FF ** -0.5),
        'ln_mix': gain((DEPTH, D_MODEL)),
        'ln_mem': gain((DEPTH, D_MODEL)),
        'mem_wq': nrm((DEPTH, D_MODEL, mem_w), sd),
        'mem_gq': gain((DEPTH, MEM_HD)),
        'mem_wk': nrm((DEPTH, D_MODEL, mem_w), sd),
        'mem_gk': gain((DEPTH, MEM_HD)),
        'mem_wv': nrm((DEPTH, D_MODEL, mem_w), sd),
        'mem_wo': nrm((DEPTH, mem_w, D_MODEL), mem_w ** -0.5),
        'ln_ffn2': gain((DEPTH, D_MODEL)),
        'ffn2_gate': nrm((DEPTH, D_MODEL, D_FF), sd),
        'ffn2_up': nrm((DEPTH, D_MODEL, D_FF), sd),
        'ffn2_down': nrm((DEPTH, D_FF, D_MODEL), D_FF ** -0.5),
        'rwkv_mu': unif((6, D_MODEL), 0.0, 1.0),
        'rwkv_wr': nrm((D_MODEL, D_MODEL), sd),
        'rwkv_wk': nrm((D_MODEL, D_MODEL), sd),
        'rwkv_wv': nrm((D_MODEL, D_MODEL), sd),
        'rwkv_wo': nrm((D_MODEL, D_MODEL), sd),
        'rwkv_w0': nrm((D_MODEL,), 0.5) - 1.0,
        'rwkv_w1': nrm((D_MODEL, DECAY_LORA), sd),
        'rwkv_w2': nrm((DECAY_LORA, D_MODEL), 0.1 * DECAY_LORA ** -0.5),
        'rwkv_a0': nrm((D_MODEL,), 0.1),
        'rwkv_a1': nrm((D_MODEL, AAA_LORA), sd),
        'rwkv_a2': nrm((AAA_LORA, D_MODEL), 0.1 * AAA_LORA ** -0.5),
        'rwkv_g1': nrm((D_MODEL, GATE_LORA), sd),
        'rwkv_g2': nrm((GATE_LORA, D_MODEL), GATE_LORA ** -0.5),
        'rwkv_kk': 0.85 + nrm((D_MODEL,), 0.02),
        'rwkv_ka': gain((D_MODEL,)),
        'rwkv_rk': nrm((RWKV_HEADS, RWKV_HD), 0.1),
        'rwkv_gn_w': gain((D_MODEL,)),
        'rwkv_gn_b': nrm((D_MODEL,), 0.02),
        'sb_wqkv': nrm((D_MODEL, 3 * SB_HEADS * SB_HD), sd),
        'sb_gq': gain((SB_HD,)),
        'sb_gk': gain((SB_HD,)),
        'sb_bias': jnp.linspace(SB_BIAS_LO, SB_BIAS_HI, SB_HEADS, dtype=f32) + nrm((SB_HEADS,), 0.1),
        'sb_wo': nrm((SB_HEADS * SB_HD, D_MODEL), (SB_HEADS * SB_HD) ** -0.5),
        'dil_wqkv': nrm((D_MODEL, 3 * n_grp * DIL_HEADS * DIL_HD), sd),
        'dil_gq': gain((DIL_HD,)),
        'dil_gk': gain((DIL_HD,)),
        'dil_wo': nrm((DIL_HEADS * DIL_HD, D_MODEL), (DIL_HEADS * DIL_HD) ** -0.5),
        'ssm_win': nrm((D_MODEL, SSM_DINNER + SSM_CONV_DIM + SSM_HEADS), sd),
        'ssm_conv_w': nrm((SSM_CONV, SSM_CONV_DIM), SSM_CONV ** -0.5),
        'ssm_conv_b': nrm((SSM_CONV_DIM,), 0.02),
        'ssm_dt_bias': dt0 + jnp.log(-jnp.expm1(-dt0)),
        'ssm_a_log': jnp.log(a_init),
        'ssm_d': gain((SSM_HEADS,)),
        'ssm_norm_w': gain((SSM_DINNER,)),
        'ssm_wout': nrm((SSM_DINNER, D_MODEL), SSM_DINNER ** -0.5),
    }


def reference(x_prompt, x_sample, state_rwkv_shift, state_rwkv_wkv, cache_sb_k, cache_sb_v,
              cache_dil0_k, cache_dil0_v, cache_dil1_k, cache_dil1_v, cache_dil2_k, cache_dil2_v,
              state_ssm_conv, state_ssm_h, cache_mem_k, cache_mem_v, page_table, mem_prompt,
              ln_ffn1, ffn1_gate, ffn1_up, ffn1_down, ln_mix, ln_mem,
              mem_wq, mem_gq, mem_wk, mem_gk, mem_wv, mem_wo,
              ln_ffn2, ffn2_gate, ffn2_up, ffn2_down,
              rwkv_mu, rwkv_wr, rwkv_wk, rwkv_wv, rwkv_wo, rwkv_w0, rwkv_w1, rwkv_w2,
              rwkv_a0, rwkv_a1, rwkv_a2, rwkv_g1, rwkv_g2, rwkv_kk, rwkv_ka, rwkv_rk, rwkv_gn_w, rwkv_gn_b,
              sb_wqkv, sb_gq, sb_gk, sb_bias, sb_wo,
              dil_wqkv, dil_gq, dil_gk, dil_wo,
              ssm_win, ssm_conv_w, ssm_conv_b, ssm_dt_bias, ssm_a_log, ssm_d, ssm_norm_w, ssm_wout):
    f32 = jnp.float32
    n_grp = len(DIL_PATTERNS)
    past_len = page_table.shape[1] * cache_sb_k.shape[1]

    def run_group(x, pos0, mem_k, mem_v, shift0, wkv0, sb_past, dil_bufs, conv0, h0):
        b, t, _ = x.shape
        pos = pos0 + jnp.arange(t, dtype=jnp.int32)
        st = {}
        for i in range(DEPTH):
            x = x + 0.5 * swiglu(rms_norm(x, ln_ffn1[i]), ffn1_gate[i], ffn1_up[i], ffn1_down[i])
            hn = rms_norm(x, ln_mix[i])
            kind = i % N_MIXERS
            if kind == 0:
                y, st['rwkv_shift'], st['rwkv_wkv'] = rwkv7_time_mix(
                    hn, shift0, wkv0, rwkv_mu, rwkv_wr, rwkv_wk, rwkv_wv, rwkv_wo, rwkv_w0, rwkv_w1, rwkv_w2,
                    rwkv_a0, rwkv_a1, rwkv_a2, rwkv_g1, rwkv_g2, rwkv_kk, rwkv_ka, rwkv_rk, rwkv_gn_w, rwkv_gn_b)
            elif kind == 1:
                qkv = (hn @ sb_wqkv).reshape(b, t, 3, SB_HEADS, SB_HD)
                q = rms_norm(qkv[:, :, 0], sb_gq)
                k = rms_norm(qkv[:, :, 1], sb_gk)
                v = qkv[:, :, 2]
                if sb_past is None:
                    k_segs, v_segs = [k], [v]
                else:
                    pool_k, pool_v, table = sb_past
                    past_k = pool_k[table].reshape(b, -1, SB_HEADS, SB_HD)
                    past_v = pool_v[table].reshape(b, -1, SB_HEADS, SB_HD)
                    k_segs, v_segs = [past_k.astype(k.dtype), k], [past_v.astype(v.dtype), v]
                k_pos = jnp.arange(pos0 + t, dtype=jnp.int32)
                y = stick_breaking(q, pos, k_segs, v_segs, k_pos, sb_bias).astype(hn.dtype) @ sb_wo
                st['sb_k'], st['sb_v'] = k, v
            elif kind == 2:
                qkv = (hn @ dil_wqkv).reshape(b, t, 3, n_grp, DIL_HEADS, DIL_HD)
                q = rope_partial(rms_norm(qkv[:, :, 0], dil_gq), pos)
                k = rope_partial(rms_norm(qkv[:, :, 1], dil_gk), pos)
                v = qkv[:, :, 2]
                outs, lses = [], []
                for g, (win, dil) in enumerate(DIL_PATTERNS):
                    if dil_bufs is None:
                        o, l = dilated_prompt_group(q[:, :, g], k[:, :, g], v[:, :, g], dil)
                        keep = min(win, t)
                        st['dil%d_k' % g] = k[:, t - keep:, g]
                        st['dil%d_v' % g] = v[:, t - keep:, g]
                    else:
                        o, l = dilated_sample_group(q[:, :, g], dil_bufs[2 * g], dil_bufs[2 * g + 1],
                                                    k[:, :, g], v[:, :, g], dil)
                        st['dil%d_k' % g] = k[:, :, g]
                        st['dil%d_v' % g] = v[:, :, g]
                    outs.append(o)
                    lses.append(l)
                wts = jax.nn.softmax(jnp.stack(lses, axis=0), axis=0)
                merged = jnp.sum(wts[..., None] * jnp.stack(outs, axis=0), axis=0)
                y = merged.reshape(b, t, DIL_HEADS * DIL_HD).astype(hn.dtype) @ dil_wo
            else:
                y, st['ssm_conv'], st['ssm_h'] = mamba2_mix(
                    hn, conv0, h0, ssm_win, ssm_conv_w, ssm_conv_b, ssm_dt_bias, ssm_a_log, ssm_d,
                    ssm_norm_w, ssm_wout)
            x = x + y
            x = x + memory_attend(rms_norm(x, ln_mem[i]), mem_wq[i], mem_gq[i], mem_k[i], mem_v[i], mem_wo[i])
            x = x + 0.5 * swiglu(rms_norm(x, ln_ffn2[i]), ffn2_gate[i], ffn2_up[i], ffn2_down[i])
        return x, st

    mem_pairs = [memory_kv(mem_prompt, mem_wk[i], mem_gk[i], mem_wv[i]) for i in range(DEPTH)]
    p_mem_k = jnp.stack([kv[0] for kv in mem_pairs], axis=0)
    p_mem_v = jnp.stack([kv[1] for kv in mem_pairs], axis=0)
    bp = x_prompt.shape[0]
    y_p, sp = run_group(
        x_prompt, 0, p_mem_k, p_mem_v,
        jnp.zeros((bp, D_MODEL), x_prompt.dtype),
        jnp.zeros((bp, RWKV_HEADS, RWKV_HD, RWKV_HD), f32),
        None, None,
        jnp.zeros((bp, SSM_CONV - 1, SSM_CONV_DIM), x_prompt.dtype),
        jnp.zeros((bp, SSM_HEADS, SSM_HEADDIM, SSM_STATE), f32))

    y_s, ss = run_group(
        x_sample, past_len, cache_mem_k, cache_mem_v, state_rwkv_shift, state_rwkv_wkv,
        (cache_sb_k, cache_sb_v, page_table),
        (cache_dil0_k, cache_dil0_v, cache_dil1_k, cache_dil1_v, cache_dil2_k, cache_dil2_v),
        state_ssm_conv, state_ssm_h)

    return (y_p, y_s,
            sp['rwkv_shift'], ss['rwkv_shift'], sp['rwkv_wkv'], ss['rwkv_wkv'],
            sp['sb_k'], sp['sb_v'], ss['sb_k'], ss['sb_v'],
            sp['dil0_k'], sp['dil0_v'], ss['dil0_k'], ss['dil0_v'],
            sp['dil1_k'], sp['dil1_v'], ss['dil1_k'], ss['dil1_v'],
            sp['dil2_k'], sp['dil2_v'], ss['dil2_k'], ss['dil2_v'],
            sp['ssm_conv'], ss['ssm_conv'], sp['ssm_h'], ss['ssm_h'],
            p_mem_k, p_mem_v)
```

```python
import functools
import math

import jax
import jax.numpy as jnp
from jax import lax
from jax.experimental import pallas as pl
from jax.experimental.pallas import tpu as pltpu

F32 = jnp.float32
BF16 = jnp.bfloat16

D_MODEL = 2048
DEPTH = 4
NORM_EPS = 1e-6
PAGE_SIZE = 128
MEM_HEADS = 4
MEM_HD = D_MODEL // MEM_HEADS
RWKV_HD = 64
GN_EPS = 64e-5
SB_HD = 128
SB_HEADS = D_MODEL // SB_HD
DIL_PATTERNS = ((128, 1), (512, 4), (2048, 16))
DIL_HD = 128
DIL_HEADS = 8
ROPE_THETA = 500000.0
ROPE_DIM = DIL_HD // 4
SSM_DINNER = 2 * D_MODEL
SSM_HEADDIM = 64
SSM_HEADS = SSM_DINNER // SSM_HEADDIM
SSM_STATE = 128
SSM_GROUPS = 8
SSM_CONV = 4
SSM_CONV_DIM = SSM_DINNER + 2 * SSM_GROUPS * SSM_STATE
SSM_CHUNK = 128
SSM_NORM_EPS = 1e-5

LANES = 128
V7X_VMEM_BYTES = 64 << 20
VMEM_LIMIT = V7X_VMEM_BYTES - (8 << 20)
RWKV_CHUNK = 64
NEG_BIG = -1e30


def _params(*sem):
    return pltpu.CompilerParams(dimension_semantics=sem, vmem_limit_bytes=VMEM_LIMIT)


def _dot(a, b):
    return jnp.dot(a.astype(BF16), b.astype(BF16), preferred_element_type=F32)


def _dot_nt(a, b):
    return lax.dot_general(a.astype(BF16), b.astype(BF16), (((1,), (1,)), ((), ())),
                           preferred_element_type=F32)


def _split3(x):
    hi = x.astype(BF16)
    r1 = x - hi.astype(F32)
    mid = r1.astype(BF16)
    lo = (r1 - mid.astype(F32)).astype(BF16)
    return hi, mid, lo


def _dot_exact_rhs(x, m_bf16):
    hi, mid, lo = _split3(x)
    d = lambda p: jnp.dot(p, m_bf16, preferred_element_type=F32)
    return d(hi) + d(mid) + d(lo)


def _dot_exact_lhs(m_bf16, x):
    hi, mid, lo = _split3(x)
    d = lambda p: jnp.dot(m_bf16, p, preferred_element_type=F32)
    return d(hi) + d(mid) + d(lo)


def _iota(shape, dim):
    return lax.broadcasted_iota(jnp.int32, shape, dim)


def _softplus(x):
    return jnp.maximum(x, 0.0) + jnp.log1p(jnp.exp(-jnp.abs(x)))


def _log_sigmoid(x):
    return jnp.minimum(x, 0.0) - jnp.log1p(jnp.exp(-jnp.abs(x)))


def _silu(x):
    return x * jax.nn.sigmoid(x)


def _linear_body(*refs, has_gain, has_rope, has_res, act, hw, res_scale):
    it = iter(refs)
    x_ref, w_ref = next(it), next(it)
    gain_ref = next(it) if has_gain else None
    cos_ref, sin_ref = (next(it), next(it)) if has_rope else (None, None)
    res_ref = next(it) if has_res else None
    o_ref, wbf_ref = next(it), next(it)

    @pl.when(pl.program_id(1) == 0)
    def _():
        wbf_ref[...] = w_ref[...].astype(BF16)

    acc = jnp.dot(x_ref[...].astype(BF16), wbf_ref[...], preferred_element_type=F32)
    if act is not None:
        acc = act(acc)
    if has_res:
        acc = res_ref[...] + res_scale * acc
    if has_gain:
        for s in range(acc.shape[1] // hw):
            y = acc[:, s * hw:(s + 1) * hw]
            ms = jnp.mean(y * y, axis=-1, keepdims=True)
            y = y * lax.rsqrt(ms + NORM_EPS) * gain_ref[...]
            if has_rope:
                lane = _iota(y.shape, 1)
                half = ROPE_DIM // 2
                rot = jnp.where(lane < half, pltpu.roll(y, hw - half, 1), pltpu.roll(y, half, 1))
                y = y * cos_ref[...] + rot * sin_ref[...]
            o_ref[:, s * hw:(s + 1) * hw] = y.astype(o_ref.dtype)
    else:
        o_ref[...] = acc.astype(o_ref.dtype)


def linear(x, w, *, col0=0, n=None, out_dtype=F32, gain=None, hw=None, rope=None, res=None,
           res_scale=1.0, act=None, layer=None, tm=512, tn=512):
    m, k = x.shape
    n = w.shape[-1] - col0 if n is None else n
    tm, tn = min(tm, m), min(tn, n)
    assert m % tm == 0 and n % tn == 0 and col0 % tn == 0 and w.shape[-2] == k
    cb = col0 // tn
    if layer is None:
        w_spec = pl.BlockSpec((k, tn), lambda j, i: (0, j + cb))
    else:
        w_spec = pl.BlockSpec((None, k, tn), lambda j, i: (layer, 0, j + cb))
    in_specs = [pl.BlockSpec((tm, k), lambda j, i: (i, 0)), w_spec]
    args = [x, w]
    if gain is not None:
        assert tn % hw == 0
        in_specs.append(pl.BlockSpec((1, hw), lambda j, i: (0, 0)))
        args.append(gain.reshape(1, hw).astype(F32))
    if rope is not None:
        cos, sin = rope
        nt = cos.shape[0] // tm
        assert hw == LANES and cos.shape[0] % tm == 0
        in_specs += [pl.BlockSpec((tm, LANES), lambda j, i: (i % nt, 0))] * 2
        args += [cos, sin]
    if res is not None:
        in_specs.append(pl.BlockSpec((tm, tn), lambda j, i: (i, j)))
        args.append(res)
    body = functools.partial(_linear_body, has_gain=gain is not None, has_rope=rope is not None,
                             has_res=res is not None, act=act, hw=hw, res_scale=res_scale)
    return pl.pallas_call(
        body,
        grid=(n // tn, m // tm),
        in_specs=in_specs,
        out_specs=pl.BlockSpec((tm, tn), lambda j, i: (i, j)),
        out_shape=jax.ShapeDtypeStruct((m, n), out_dtype),
        scratch_shapes=[pltpu.VMEM((k, tn), BF16)],
        compiler_params=_params("parallel", "arbitrary"),
    )(*args)


def _rmsnorm_body(x_ref, g_ref, o_ref):
    x = x_ref[...]
    ms = jnp.mean(x * x, axis=-1, keepdims=True)
    o_ref[...] = (x * lax.rsqrt(ms + NORM_EPS) * g_ref[...]).astype(o_ref.dtype)


def rmsnorm(x, g, out_dtype=BF16, tm=512):
    m, d = x.shape
    tm = min(tm, m)
    assert m % tm == 0
    return pl.pallas_call(
        _rmsnorm_body,
        grid=(m // tm,),
        in_specs=[pl.BlockSpec((tm, d), lambda i: (i, 0)), pl.BlockSpec((1, d), lambda i: (0, 0))],
        out_specs=pl.BlockSpec((tm, d), lambda i: (i, 0)),
        out_shape=jax.ShapeDtypeStruct((m, d), out_dtype),
        compiler_params=_params("parallel"),
    )(x, g.reshape(1, d))


def _ffn_body(x_ref, g_ref, wg_ref, wu_ref, wd_ref, o_ref, xn_ref):
    @pl.when(pl.program_id(1) == 0)
    def _():
        x = x_ref[...]
        ms = jnp.mean(x * x, axis=-1, keepdims=True)
        xn_ref[...] = (x * lax.rsqrt(ms + NORM_EPS) * g_ref[...]).astype(BF16)
        o_ref[...] = x

    xn = xn_ref[...]
    gate = jnp.dot(xn, wg_ref[...].astype(BF16), preferred_element_type=F32)
    up = jnp.dot(xn, wu_ref[...].astype(BF16), preferred_element_type=F32)
    h = (0.5 * _silu(gate) * up).astype(BF16)
    o_ref[...] += jnp.dot(h, wd_ref[...].astype(BF16), preferred_element_type=F32)


def ffn(x, g, w_gate, w_up, w_down, layer, tm=1024, tf=256):
    m, d = x.shape
    f = w_gate.shape[-1]
    tm = min(tm, m)
    assert m % tm == 0 and f % tf == 0
    return pl.pallas_call(
        _ffn_body,
        grid=(m // tm, f // tf),
        in_specs=[pl.BlockSpec((tm, d), lambda i, j: (i, 0)),
                  pl.BlockSpec((1, d), lambda i, j: (0, 0)),
                  pl.BlockSpec((None, d, tf), lambda i, j: (layer, 0, j)),
                  pl.BlockSpec((None, d, tf), lambda i, j: (layer, 0, j)),
                  pl.BlockSpec((None, tf, d), lambda i, j: (layer, j, 0))],
        out_specs=pl.BlockSpec((tm, d), lambda i, j: (i, 0)),
        out_shape=jax.ShapeDtypeStruct((m, d), F32),
        scratch_shapes=[pltpu.VMEM((tm, d), BF16)],
        compiler_params=_params("parallel", "arbitrary"),
    )(x, g.reshape(1, d), w_gate, w_up, w_down)


def _memattn_body(q_ref, k_ref, v_ref, o_ref, *, scale):
    s = _dot_nt(q_ref[0], k_ref[0, 0]) * scale
    m = jnp.max(s, axis=-1, keepdims=True)
    p = jnp.exp(s - m)
    l = jnp.sum(p, axis=-1, keepdims=True)
    o_ref[0] = (_dot(p, v_ref[0, 0]) / l).astype(o_ref.dtype)


def mem_attention(q, mem_k, mem_v, layer, tq=512):
    b, t, d = q.shape
    n_mem = mem_k.shape[2]
    tq = min(tq, t)
    assert t % tq == 0
    kv_spec = pl.BlockSpec((1, 1, n_mem, MEM_HD), lambda bi, ti, h: (layer, bi, 0, h))
    return pl.pallas_call(
        functools.partial(_memattn_body, scale=MEM_HD ** -0.5),
        grid=(b, t // tq, MEM_HEADS),
        in_specs=[pl.BlockSpec((1, tq, MEM_HD), lambda bi, ti, h: (bi, ti, h)), kv_spec, kv_spec],
        out_specs=pl.BlockSpec((1, tq, MEM_HD), lambda bi, ti, h: (bi, ti, h)),
        out_shape=jax.ShapeDtypeStruct((b, t, d), BF16),
        compiler_params=_params("parallel", "parallel", "parallel"),
    )(q, mem_k, mem_v)


def rope_tables(pos):
    half = ROPE_DIM // 2
    inv_freq = ROPE_THETA ** (-jnp.arange(half, dtype=F32) / half)
    ang = pos.astype(F32)[:, None] * inv_freq[None, :]
    cos, sin = jnp.cos(ang), jnp.sin(ang)
    rest = DIL_HD - ROPE_DIM
    n = pos.shape[0]
    cos_t = jnp.concatenate([cos, cos, jnp.ones((n, rest), F32)], axis=1)
    sin_t = jnp.concatenate([-sin, sin, jnp.zeros((n, rest), F32)], axis=1)
    return cos_t, sin_t


def _sb_tile(q, k, v, bias, causal, carry, acc, upper):
    z = _dot_nt(q, k) * (SB_HD ** -0.5) + bias
    ls = _log_sigmoid(z)
    log_keep = jnp.where(causal, ls - z, 0.0)
    hi = log_keep.astype(BF16)
    lo = (log_keep - hi.astype(F32)).astype(BF16)
    between = (jnp.dot(hi, upper, preferred_element_type=F32)
               + jnp.dot(lo, upper, preferred_element_type=F32) + carry)
    att = jnp.where(causal, jnp.exp(ls + between), 0.0)
    acc = acc + _dot(att, v)
    carry = carry + jnp.sum(log_keep, axis=-1, keepdims=True)
    return carry, acc


def _upper_ones(n):
    return jnp.where(_iota((n, n), 0) > _iota((n, n), 1), 1.0, 0.0).astype(BF16)


def _sb_prompt_body(bias_ref, q_ref, k_ref, v_ref, o_ref, *, tq):
    h, qi = pl.program_id(1), pl.program_id(2)
    q = q_ref[0]
    bias = bias_ref[h]
    upper = _upper_ones(tq)
    row = qi * tq + _iota((tq, tq), 0)

    def step(jj, state):
        carry, acc = state
        j = qi - jj
        start = pl.multiple_of(j * tq, tq)
        k = k_ref[0, pl.ds(start, tq), :]
        v = v_ref[0, pl.ds(start, tq), :]
        causal = (j * tq + _iota((tq, tq), 1)) < row
        return _sb_tile(q, k, v, bias, causal, carry, acc, upper)

    init = (jnp.zeros((tq, 1), F32), jnp.zeros((tq, SB_HD), F32))
    _, acc = lax.fori_loop(0, qi + 1, step, init)
    o_ref[0] = acc.astype(o_ref.dtype)


def sb_attention_prompt(q, k, v, bias, tq=256):
    b, t, d = q.shape
    tq = min(tq, t)
    assert t % tq == 0
    kv_spec = pl.BlockSpec((1, t, SB_HD), lambda bi, h, qi: (bi, 0, h))
    io_spec = pl.BlockSpec((1, tq, SB_HD), lambda bi, h, qi: (bi, qi, h))
    return pl.pallas_call(
        functools.partial(_sb_prompt_body, tq=tq),
        grid=(b, d // SB_HD, t // tq),
        in_specs=[pl.BlockSpec(memory_space=pltpu.SMEM), io_spec, kv_spec, kv_spec],
        out_specs=io_spec,
        out_shape=jax.ShapeDtypeStruct((b, t, d), BF16),
        compiler_params=_params("parallel", "parallel", "arbitrary"),
    )(bias, q, k, v)


def _sb_decode_body(table_ref, bias_ref, q_ref, kn_ref, vn_ref, kp_ref, vp_ref, o_ref,
                    qbd_ref, acc_ref, carry_ref, *, n_pages):
    del table_ref
    j = pl.program_id(1)
    nh, d = SB_HEADS, SB_HEADS * SB_HD
    own = (_iota((nh, d), 1) // SB_HD) == _iota((nh, d), 0)
    bias = bias_ref[...]
    past_len = n_pages * PAGE_SIZE

    @pl.when(j == 0)
    def _():
        qbd = jnp.where(own, jnp.broadcast_to(q_ref[0], (nh, d)), 0.0)
        qbd_ref[...] = qbd.astype(BF16)
        z = jnp.sum(qbd * kn_ref[0], axis=-1, keepdims=True) * (SB_HD ** -0.5) + bias
        causal = jnp.full((nh, 1), past_len, jnp.int32) < past_len
        ls = _log_sigmoid(z)
        carry_ref[...] = jnp.where(causal, ls - z, 0.0)
        att = jnp.where(causal, jnp.exp(ls), 0.0)
        acc_ref[...] = jnp.where(own, att * vn_ref[0], 0.0)

    page = n_pages - 1 - j
    causal = (page * PAGE_SIZE + _iota((nh, PAGE_SIZE), 1)) < past_len
    carry, part = _sb_tile(qbd_ref[...], kp_ref[0], vp_ref[0], bias, causal, carry_ref[...],
                           jnp.zeros((nh, d), F32), _upper_ones(PAGE_SIZE))
    carry_ref[...] = carry
    acc_ref[...] += jnp.where(own, part, 0.0)

    @pl.when(j == n_pages - 1)
    def _():
        o_ref[0] = jnp.sum(acc_ref[...], axis=0, keepdims=True)


def sb_attention_decode(q, k_new, v_new, pool_k, pool_v, table, bias):
    b, _, d = q.shape
    n_pages = table.shape[1]
    pool_k = pool_k.reshape(pool_k.shape[0], PAGE_SIZE, d)
    pool_v = pool_v.reshape(pool_v.shape[0], PAGE_SIZE, d)
    row_spec = pl.BlockSpec((1, 1, d), lambda bi, j, tab: (bi, 0, 0))
    page_spec = pl.BlockSpec((1, PAGE_SIZE, d), lambda bi, j, tab: (tab[bi, n_pages - 1 - j], 0, 0))
    grid_spec = pltpu.PrefetchScalarGridSpec(
        num_scalar_prefetch=1,
        grid=(b, n_pages),
        in_specs=[pl.BlockSpec((SB_HEADS, 1), lambda bi, j, tab: (0, 0)),
                  row_spec, row_spec, row_spec, page_spec, page_spec],
        out_specs=row_spec,
        scratch_shapes=[pltpu.VMEM((SB_HEADS, d), BF16), pltpu.VMEM((SB_HEADS, d), F32),
                        pltpu.VMEM((SB_HEADS, 1), F32)],
    )
    return pl.pallas_call(
        functools.partial(_sb_decode_body, n_pages=n_pages),
        grid_spec=grid_spec,
        out_shape=jax.ShapeDtypeStruct((b, 1, d), F32),
        compiler_params=_params("parallel", "arbitrary"),
    )(table, bias.reshape(SB_HEADS, 1), q, k_new, v_new, pool_k, pool_v)


def _dil_prompt_body(*refs, tq):
    n_grp = len(DIL_PATTERNS)
    q_refs, k_refs, v_refs = refs[:n_grp], refs[n_grp:2 * n_grp], refs[2 * n_grp:3 * n_grp]
    o_ref = refs[3 * n_grp]
    qi = pl.program_id(2)
    rel = _iota((tq, tq), 0) - _iota((tq, tq), 1)
    state = (jnp.full((tq, 1), NEG_BIG, F32), jnp.zeros((tq, 1), F32), jnp.zeros((tq, DIL_HD), F32))
    for g, (win, dil) in enumerate(DIL_PATTERNS):
        q = q_refs[g][0]
        k_ref, v_ref = k_refs[g], v_refs[g]

        def step(jj, st, k_ref=k_ref, v_ref=v_ref, q=q, win=win, dil=dil):
            m, l, acc = st
            start = pl.multiple_of((qi - jj) * tq, tq)
            k = k_ref[0, pl.ds(start, tq), :]
            v = v_ref[0, pl.ds(start, tq), :]
            dist = rel + jj * tq
            valid = (dist >= 0) & (dist <= win) & ((dist & (dil - 1)) == 0)
            s = jnp.where(valid, _dot_nt(q, k) * (DIL_HD ** -0.5), NEG_BIG)
            m_new = jnp.maximum(m, jnp.max(s, axis=-1, keepdims=True))
            alpha = jnp.exp(m - m_new)
            p = jnp.where(valid, jnp.exp(s - m_new), 0.0)
            l = alpha * l + jnp.sum(p, axis=-1, keepdims=True)
            acc = alpha * acc + _dot(p, v)
            return m_new, l, acc

        state = lax.fori_loop(0, jnp.minimum(qi, win // tq) + 1, step, state)
    _, l, acc = state
    o_ref[0] = (acc / l).astype(o_ref.dtype)


def dil_attention_prompt(q, k, v, tq=128):
    b, t, _ = q.shape
    n_grp = len(DIL_PATTERNS)
    assert all(d & (d - 1) == 0 for _, d in DIL_PATTERNS) and t % tq == 0

    def q_spec(g):
        return pl.BlockSpec((1, tq, DIL_HD), lambda bi, h, qi: (bi, qi, g * DIL_HEADS + h))

    def kv_spec(g):
        return pl.BlockSpec((1, t, DIL_HD), lambda bi, h, qi: (bi, 0, g * DIL_HEADS + h))

    in_specs = ([q_spec(g) for g in range(n_grp)] + [kv_spec(g) for g in range(n_grp)]
                + [kv_spec(g) for g in range(n_grp)])
    return pl.pallas_call(
        functools.partial(_dil_prompt_body, tq=tq),
        grid=(b, DIL_HEADS, t // tq),
        in_specs=in_specs,
        out_specs=pl.BlockSpec((1, tq, DIL_HD), lambda bi, h, qi: (bi, qi, h)),
        out_shape=jax.ShapeDtypeStruct((b, t, DIL_HEADS * DIL_HD), BF16),
        compiler_params=_params("parallel", "parallel", "arbitrary"),
    )(*([q] * n_grp + [k] * n_grp + [v] * n_grp))


def _dil_decode_body(*refs):
    n_grp = len(DIL_PATTERNS)
    q_ref, kn_ref, vn_ref = refs[:3]
    kb_refs, vb_refs = refs[3:3 + n_grp], refs[3 + n_grp:3 + 2 * n_grp]
    o_ref = refs[3 + 2 * n_grp]
    scale = DIL_HD ** -0.5
    for h in range(DIL_HEADS):
        scores, news = [], []
        for g in range(n_grp):
            sl = slice((g * DIL_HEADS + h) * DIL_HD, (g * DIL_HEADS + h + 1) * DIL_HD)
            q = q_ref[0, :, sl]
            kb = kb_refs[g][0, :, h * DIL_HD:(h + 1) * DIL_HD]
            scores.append(jnp.sum(kb * q, axis=-1, keepdims=True) * scale)
            news.append(jnp.sum(kn_ref[0, :, sl] * q, axis=-1, keepdims=True) * scale)
        m = functools.reduce(jnp.maximum, [jnp.max(s, axis=0, keepdims=True) for s in scores] + news)
        l = jnp.zeros((1, 1), F32)
        acc = jnp.zeros((1, DIL_HD), F32)
        for g in range(n_grp):
            sl = slice((g * DIL_HEADS + h) * DIL_HD, (g * DIL_HEADS + h + 1) * DIL_HD)
            p = jnp.exp(scores[g] - m)
            pn = jnp.exp(news[g] - m)
            vb = vb_refs[g][0, :, h * DIL_HD:(h + 1) * DIL_HD]
            l = l + jnp.sum(p, axis=0, keepdims=True) + pn
            acc = acc + jnp.sum(p * vb, axis=0, keepdims=True) + pn * vn_ref[0, :, sl]
        o_ref[0, :, h * DIL_HD:(h + 1) * DIL_HD] = acc / l


def dil_attention_decode(q, k_new, v_new, bufs):
    b = q.shape[0]
    n_grp = len(DIL_PATTERNS)
    d = DIL_HEADS * DIL_HD
    row_spec = pl.BlockSpec((1, 1, n_grp * d), lambda bi: (bi, 0, 0))
    views, specs = [], []
    for which in range(2):
        for g, (win, dil) in enumerate(DIL_PATTERNS):
            buf = bufs[2 * g + which]
            assert buf.shape[1] == win and win % dil == 0
            views.append(buf.reshape(b, win // dil, dil * d))
            specs.append(pl.BlockSpec((1, win // dil, d), lambda bi: (bi, 0, 0)))
    order = [views[g] for g in range(n_grp)] + [views[n_grp + g] for g in range(n_grp)]
    return pl.pallas_call(
        _dil_decode_body,
        grid=(b,),
        in_specs=[row_spec] * 3 + specs,
        out_specs=pl.BlockSpec((1, 1, d), lambda bi: (bi, 0, 0)),
        out_shape=jax.ShapeDtypeStruct((b, 1, d), F32),
        compiler_params=_params("parallel"),
    )(q, k_new, v_new, *order)


def _rwkv_mix_body(x_ref, xp_ref, mu_ref, *o_refs):
    x = x_ref[...]
    xx = xp_ref[...] - x
    for i, o_ref in enumerate(o_refs):
        o_ref[...] = (x + xx * mu_ref[i:i + 1, :]).astype(o_ref.dtype)


def rwkv_mix(xn, x_prev, mu, tm=512):
    m, d = xn.shape
    tm = min(tm, m)
    n_mix = mu.shape[0]
    row = pl.BlockSpec((tm, d), lambda i: (i, 0))
    return pl.pallas_call(
        _rwkv_mix_body,
        grid=(m // tm,),
        in_specs=[row, row, pl.BlockSpec((n_mix, d), lambda i: (0, 0))],
        out_specs=[row] * n_mix,
        out_shape=[jax.ShapeDtypeStruct((m, d), BF16)] * n_mix,
        compiler_params=_params("parallel"),
    )(xn, x_prev, mu)


def _rwkv_scan_body(r_ref, k_ref, v_ref, wl_ref, al_ref, g_ref, prm_ref, s0_ref, y_ref, sf_ref,
                    st_ref, *, t_valid, t_total):
    c = pl.program_id(1)
    C, hd = RWKV_CHUNK, RWKV_HD
    n2 = 2 * C

    @pl.when(c == 0)
    def _():
        st_ref[...] = s0_ref[0]

    left = _iota((C, LANES), 1) < hd
    ri, ci = _iota((n2, n2), 0), _iota((n2, n2), 1)
    strict = (ci & (C - 1)) < (ri & (C - 1))
    incl = (ci & (C - 1)) <= (ri & (C - 1))
    tri = jnp.where(_iota((C, C), 1) <= _iota((C, C), 0), 1.0, 0.0).astype(BF16)
    same_head = (_iota((LANES, LANES), 0) // hd) == (_iota((LANES, LANES), 1) // hd)
    seg_ones = jnp.where(same_head, 1.0, 0.0).astype(BF16)
    masked = t_valid < t_total
    rows_valid = (c * C + _iota((C, LANES), 0)) < t_valid

    def stack(x):
        return jnp.concatenate([jnp.where(left, x, 0.0), jnp.where(left, 0.0, x)], axis=0)

    def pair(p, _):
        off = pl.multiple_of(p * LANES, LANES)
        r, k, v, wl, al, g = [ref[0, :, pl.ds(off, LANES)]
                              for ref in (r_ref, k_ref, v_ref, wl_ref, al_ref, g_ref)]
        prm = prm_ref[:, pl.ds(off, LANES)]
        w0, a0, k_k, k_a, r_k, gn_w, gn_b = [prm[i:i + 1] for i in range(7)]
        lw = -jnp.exp(-_softplus(-(w0 + wl)) - 0.5)
        a = jax.nn.sigmoid(a0 + al)
        kkr = k * k_k
        kk = kkr * lax.rsqrt(jnp.maximum(_dot_exact_rhs(kkr * kkr, seg_ones), 1e-24))
        kmod = k * (1.0 + (a - 1.0) * k_a)
        bonus = _dot_exact_rhs(r * kmod * r_k, seg_ones) * v
        if masked:
            lw, kk, kmod, v = [jnp.where(rows_valid, t, 0.0) for t in (lw, kk, kmod, v)]
        lc = _dot_exact_lhs(tri, lw)
        lend = lc[C - 1:C, :]
        kka = kk * a
        e_inv = jnp.exp(-lc)
        e_end = jnp.exp(lend - lc)
        ar = jnp.concatenate([stack(-kk * jnp.exp(lc - lw)), stack(r * jnp.exp(lc))], axis=0)
        bk = jnp.concatenate([stack(kka * e_inv), stack(kmod * e_inv)], axis=0)
        gram = _dot_nt(ar, bk)
        st = st_ref[p]
        a_s = _dot_nt(ar, st)
        vv = jnp.concatenate([v, v], axis=0)
        y = _dot(jnp.where(strict, gram[:n2, n2:], 0.0), vv) + a_s[:n2]
        pw = jnp.where(strict, gram[:n2, :n2], 0.0)
        n = 1
        while n < C:
            y = y + _dot(pw, y)
            n *= 2
            if n < C:
                pw = _dot(pw, pw)
        o2 = _dot(jnp.concatenate([jnp.where(incl, gram[n2:, :n2], 0.0),
                                   jnp.where(incl, gram[n2:, n2:], 0.0)], axis=1),
                  jnp.concatenate([y, vv], axis=0)) + a_s[n2:]
        o = jnp.where(left, o2[:C], o2[C:])
        u = jnp.where(left, y[:C], y[C:])
        uv = jnp.concatenate([u, v], axis=0)
        bk_end = jnp.concatenate([kka * e_end, kmod * e_end], axis=0)
        s_new = st * jnp.exp(lend) + _dot(uv.T, bk_end)
        st_ref[p] = jnp.where(same_head, s_new, 0.0)
        mean = _dot_exact_rhs(o, seg_ones) * (1.0 / hd)
        cen = o - mean
        var = _dot_exact_rhs(cen * cen, seg_ones) * (1.0 / hd)
        on = cen * lax.rsqrt(var + GN_EPS) * gn_w + gn_b
        y_ref[0, :, pl.ds(off, LANES)] = ((on + bonus) * g).astype(y_ref.dtype)
        return 0

    lax.fori_loop(0, D_MODEL // LANES, pair, 0)

    @pl.when(c == pl.num_programs(1) - 1)
    def _():
        sf_ref[0] = st_ref[...]


def rwkv_scan(r, k, v, wl, al, g, prm, s0, t_valid):
    b, t, d = r.shape
    C = RWKV_CHUNK
    assert t % C == 0 and d == D_MODEL
    seq = pl.BlockSpec((1, C, d), lambda bi, c: (bi, c, 0))
    st_spec = pl.BlockSpec((1, d // LANES, LANES, LANES), lambda bi, c: (bi, 0, 0, 0))
    return pl.pallas_call(
        functools.partial(_rwkv_scan_body, t_valid=t_valid, t_total=t),
        grid=(b, t // C),
        in_specs=[seq] * 6 + [pl.BlockSpec((8, d), lambda bi, c: (0, 0)), st_spec],
        out_specs=[seq, st_spec],
        out_shape=[jax.ShapeDtypeStruct((b, t, d), BF16),
                   jax.ShapeDtypeStruct((b, d // LANES, LANES, LANES), F32)],
        scratch_shapes=[pltpu.VMEM((d // LANES, LANES, LANES), F32)],
        compiler_params=_params("parallel", "arbitrary"),
    )(r, k, v, wl, al, g, prm, s0)


def _conv_body(x_ref, prev_ref, w_ref, b_ref, o_ref, buf_ref, *, tt):
    halo = 8

    @pl.when(pl.program_id(2) == 0)
    def _():
        buf_ref[0:halo, :] = prev_ref[0]

    buf_ref[halo:halo + tt, :] = x_ref[0]
    acc = b_ref[...] + buf_ref[halo:halo + tt, :] * w_ref[SSM_CONV - 1:SSM_CONV, :]
    for back in range(1, SSM_CONV):
        tap = SSM_CONV - 1 - back
        acc = acc + buf_ref[halo - back:halo - back + tt, :] * w_ref[tap:tap + 1, :]
    o_ref[0] = _silu(acc)
    buf_ref[0:halo, :] = buf_ref[tt:tt + halo, :]


def causal_conv_silu(x, prev8, w, bias, tt=256, tc=1024):
    b, t, ch = x.shape
    tt = min(tt, t)
    assert t % tt == 0 and ch % tc == 0
    return pl.pallas_call(
        functools.partial(_conv_body, tt=tt),
        grid=(b, ch // tc, t // tt),
        in_specs=[pl.BlockSpec((1, tt, tc), lambda bi, ci, ti: (bi, ti, ci)),
                  pl.BlockSpec((1, 8, tc), lambda bi, ci, ti: (bi, 0, ci)),
                  pl.BlockSpec((SSM_CONV, tc), lambda bi, ci, ti: (0, ci)),
                  pl.BlockSpec((1, tc), lambda bi, ci, ti: (0, ci))],
        out_specs=pl.BlockSpec((1, tt, tc), lambda bi, ci, ti: (bi, ti, ci)),
        out_shape=jax.ShapeDtypeStruct((b, t, ch), F32),
        scratch_shapes=[pltpu.VMEM((tt + 8, tc), F32)],
        compiler_params=_params("parallel", "parallel", "arbitrary"),
    )(x, prev8, w, bias.reshape(1, ch))


def _ssd_body(xa_ref, z_ref, dt_ref, dtt_ref, dtb_ref, dtbt_ref, al_ref, alt_ref, dsk_ref, nw_ref,
              h0_ref, y_ref, hf_ref, h_ref, *, t_valid, t_total):
    c = pl.program_id(1)
    Q, P = SSM_CHUNK, SSM_HEADDIM
    per_grp = SSM_HEADS // SSM_GROUPS
    gw = per_grp * P

    @pl.when(c == 0)
    def _():
        h_ref[...] = h0_ref[0]

    tri = _iota((Q, Q), 1) <= _iota((Q, Q), 0)
    tri_b = jnp.where(tri, 1.0, 0.0).astype(BF16)
    upp_b = jnp.where(_iota((Q, Q), 0) <= _iota((Q, Q), 1), 1.0, 0.0).astype(BF16)
    left = _iota((Q, LANES), 1) < P
    masked = t_valid < t_total

    def group(g, _):
        dt = _softplus(dt_ref[0, g] + dtb_ref[g])
        dtt = _softplus(dtt_ref[0, g] + dtbt_ref[g])
        if masked:
            dt = jnp.where(c * Q + _iota(dt.shape, 0) < t_valid, dt, 0.0)
            dtt = jnp.where(c * Q + _iota(dtt.shape, 1) < t_valid, dtt, 0.0)
        cum = _dot_exact_lhs(tri_b, dt * -jnp.exp(al_ref[g]))
        cumt = _dot_exact_rhs(dtt * -jnp.exp(alt_ref[g]), upp_b)
        ecum = jnp.exp(cum)
        tail = jnp.exp(cum[Q - 1:Q, :] - cum) * dt
        boff = pl.multiple_of(SSM_DINNER + g * SSM_STATE, SSM_STATE)
        coff = pl.multiple_of(SSM_DINNER + SSM_GROUPS * SSM_STATE + g * SSM_STATE, SSM_STATE)
        roff = pl.multiple_of(g * gw, gw)
        bm = xa_ref[0, :, pl.ds(boff, SSM_STATE)]
        cm = xa_ref[0, :, pl.ds(coff, SSM_STATE)]
        cb = _dot_nt(cm, bm)
        hg = h_ref[pl.ds(roff, gw), :]
        y_state = _dot_nt(cm, hg)
        xg = xa_ref[0, :, pl.ds(roff, gw)]
        ys, xts = [], []
        for q in range(gw // LANES):
            xp = xg[:, q * LANES:(q + 1) * LANES]
            halves = []
            for e in range(2):
                hh = 2 * q + e
                seg = cum[:, hh:hh + 1] - cumt[hh:hh + 1, :]
                dec = jnp.where(tri, jnp.exp(jnp.where(tri, seg, 0.0)), 0.0)
                halves.append(_dot(cb * dec * dtt[hh:hh + 1, :], xp))
            pick = lambda t: jnp.where(left, t[:, 2 * q:2 * q + 1], t[:, 2 * q + 1:2 * q + 2])
            ys.append(jnp.where(left, halves[0], halves[1])
                      + y_state[:, q * LANES:(q + 1) * LANES] * pick(ecum))
            xts.append(xp * pick(tail))
        upd = _dot(jnp.concatenate(xts, axis=1).T, bm)
        for hh in range(per_grp):
            rows = slice(hh * P, (hh + 1) * P)
            h_ref[pl.ds(pl.multiple_of(roff + hh * P, P), P), :] = (
                hg[rows] * jnp.exp(cumt[hh:hh + 1, Q - 1:Q]) + upd[rows])
        yg = jnp.concatenate(ys, axis=1)
        yg = (yg + xg * dsk_ref[:, pl.ds(roff, gw)]) * _silu(z_ref[0, :, pl.ds(roff, gw)])
        ms = jnp.mean(yg * yg, axis=-1, keepdims=True)
        y_ref[0, :, pl.ds(roff, gw)] = (
            yg * lax.rsqrt(ms + SSM_NORM_EPS) * nw_ref[:, pl.ds(roff, gw)]).astype(y_ref.dtype)
        return 0

    lax.fori_loop(0, SSM_GROUPS, group, 0)

    @pl.when(c == pl.num_programs(1) - 1)
    def _():
        hf_ref[0] = h_ref[...]


def ssd_scan(xa, z, dt_raw, dt_bias, a_log, d_skip, norm_w, h0, t_valid):
    b, t, _ = xa.shape
    Q = SSM_CHUNK
    per_grp = SSM_HEADS // SSM_GROUPS
    assert t % Q == 0
    dt_g = dt_raw.reshape(b, t, SSM_GROUPS, per_grp).transpose(0, 2, 1, 3)
    dt_gt = dt_g.transpose(0, 1, 3, 2)
    grp = lambda p: p.reshape(SSM_GROUPS, 1, per_grp)
    grp_t = lambda p: p.reshape(SSM_GROUPS, per_grp, 1)
    full3 = lambda s: pl.BlockSpec(s, lambda bi, c: (0, 0, 0))
    lanes = lambda w: pl.BlockSpec((1, w), lambda bi, c: (0, 0))
    st_spec = pl.BlockSpec((1, SSM_DINNER, SSM_STATE), lambda bi, c: (bi, 0, 0))
    return pl.pallas_call(
        functools.partial(_ssd_body, t_valid=t_valid, t_total=t),
        grid=(b, t // Q),
        in_specs=[pl.BlockSpec((1, Q, SSM_CONV_DIM), lambda bi, c: (bi, c, 0)),
                  pl.BlockSpec((1, Q, SSM_DINNER), lambda bi, c: (bi, c, 0)),
                  pl.BlockSpec((1, SSM_GROUPS, Q, per_grp), lambda bi, c: (bi, 0, c, 0)),
                  pl.BlockSpec((1, SSM_GROUPS, per_grp, Q), lambda bi, c: (bi, 0, 0, c)),
                  full3((SSM_GROUPS, 1, per_grp)), full3((SSM_GROUPS, per_grp, 1)),
                  full3((SSM_GROUPS, 1, per_grp)), full3((SSM_GROUPS, per_grp, 1)),
                  lanes(SSM_DINNER), lanes(SSM_DINNER), st_spec],
        out_specs=[pl.BlockSpec((1, Q, SSM_DINNER), lambda bi, c: (bi, c, 0)), st_spec],
        out_shape=[jax.ShapeDtypeStruct((b, t, SSM_DINNER), BF16),
                   jax.ShapeDtypeStruct((b, SSM_DINNER, SSM_STATE), F32)],
        scratch_shapes=[pltpu.VMEM((SSM_DINNER, SSM_STATE), F32)],
        compiler_params=_params("parallel", "arbitrary"),
    )(xa, z, dt_g, dt_gt, grp(dt_bias), grp_t(dt_bias), grp(a_log), grp_t(a_log),
      jnp.repeat(d_skip, SSM_HEADDIM).reshape(1, SSM_DINNER), norm_w.reshape(1, SSM_DINNER), h0)


def _pad_time(x3, mult):
    t = x3.shape[1]
    tp = -(-t // mult) * mult
    return x3 if tp == t else jnp.pad(x3, ((0, 0), (0, tp - t), (0, 0)))


def _rwkv_mixer(xf, b, t, ln, shift0, wkv0, W):
    d = D_MODEL
    xn = rmsnorm(xf, ln, out_dtype=F32).reshape(b, t, d)
    x_prev = jnp.concatenate([shift0[:, None].astype(F32), xn[:, :-1]], axis=1)
    xn_p, xp_p = _pad_time(xn, RWKV_CHUNK), _pad_time(x_prev, RWKV_CHUNK)
    tp = xn_p.shape[1]
    xr, xw, xk, xv, xa, xg = rwkv_mix(xn_p.reshape(-1, d), xp_p.reshape(-1, d), W['rwkv_mu'])
    r = linear(xr, W['rwkv_wr'])
    k = linear(xk, W['rwkv_wk'])
    v = linear(xv, W['rwkv_wv'])
    wl = linear(linear(xw, W['rwkv_w1'], act=jnp.tanh, out_dtype=BF16), W['rwkv_w2'])
    al = linear(linear(xa, W['rwkv_a1'], out_dtype=BF16), W['rwkv_a2'])
    g = linear(linear(xg, W['rwkv_g1'], act=jax.nn.sigmoid, out_dtype=BF16), W['rwkv_g2'])
    prm = jnp.stack([W['rwkv_w0'], W['rwkv_a0'], W['rwkv_kk'], W['rwkv_ka'], W['rwkv_rk'].reshape(d),
                     W['rwkv_gn_w'], W['rwkv_gn_b'], jnp.zeros((d,), F32)])
    n_pair = d // LANES
    s4 = wkv0.astype(F32).reshape(b, n_pair, 2, RWKV_HD, RWKV_HD)
    zero = jnp.zeros_like(s4[:, :, 0])
    s0 = jnp.concatenate([jnp.concatenate([s4[:, :, 0], zero], axis=-1),
                          jnp.concatenate([zero, s4[:, :, 1]], axis=-1)], axis=-2)
    to3 = lambda a: a.reshape(b, tp, d)
    y, sf = rwkv_scan(to3(r), to3(k), to3(v), to3(wl), to3(al), to3(g), prm, s0, t_valid=t)
    wkv = jnp.stack([sf[:, :, :RWKV_HD, :RWKV_HD], sf[:, :, RWKV_HD:, RWKV_HD:]], axis=2)
    y = y[:, :t].reshape(b * t, d)
    return linear(y, W['rwkv_wo'], res=xf), xn[:, -1], wkv.reshape(b, d // RWKV_HD, RWKV_HD, RWKV_HD)


def _sb_mixer(xf, b, t, ln, sb_past, W):
    d = D_MODEL
    xn = rmsnorm(xf, ln)
    qdt = BF16 if sb_past is None else F32
    q = linear(xn, W['sb_wqkv'], col0=0, n=d, gain=W['sb_gq'], hw=SB_HD, out_dtype=qdt)
    k = linear(xn, W['sb_wqkv'], col0=d, n=d, gain=W['sb_gk'], hw=SB_HD)
    v = linear(xn, W['sb_wqkv'], col0=2 * d, n=d)
    to3 = lambda a: a.reshape(b, t, d)
    if sb_past is None:
        o = sb_attention_prompt(to3(q), to3(k), to3(v), W['sb_bias'])
    else:
        pool_k, pool_v, table = sb_past
        o = sb_attention_decode(to3(q), to3(k), to3(v), pool_k, pool_v, table, W['sb_bias'])
    xf = linear(o.reshape(b * t, d), W['sb_wo'], res=xf)
    return xf, k.reshape(b, t, SB_HEADS, SB_HD), v.reshape(b, t, SB_HEADS, SB_HD)


def _dil_mixer(xf, b, t, pos0, ln, dil_bufs, W):
    n_grp = len(DIL_PATTERNS)
    gd = n_grp * DIL_HEADS * DIL_HD
    xn = rmsnorm(xf, ln)
    qdt = BF16 if dil_bufs is None else F32
    rows = t if t >= 8 else b * t
    rope = rope_tables(pos0 + (jnp.arange(rows, dtype=jnp.int32) % t))
    tm = min(rows, 512)
    q = linear(xn, W['dil_wqkv'], col0=0, n=gd, gain=W['dil_gq'], hw=DIL_HD, rope=rope, out_dtype=qdt, tm=tm)
    k = linear(xn, W['dil_wqkv'], col0=gd, n=gd, gain=W['dil_gk'], hw=DIL_HD, rope=rope, tm=tm)
    v = linear(xn, W['dil_wqkv'], col0=2 * gd, n=gd, tm=tm)
    to3 = lambda a: a.reshape(b, t, gd)
    if dil_bufs is None:
        o = dil_attention_prompt(to3(q), to3(k), to3(v))
    else:
        o = dil_attention_decode(to3(q), to3(k), to3(v), dil_bufs)
    xf = linear(o.reshape(b * t, DIL_HEADS * DIL_HD), W['dil_wo'], res=xf)
    k5 = k.reshape(b, t, n_grp, DIL_HEADS, DIL_HD)
    v5 = v.reshape(b, t, n_grp, DIL_HEADS, DIL_HD)
    states = []
    for g, (win, _) in enumerate(DIL_PATTERNS):
        keep = min(win, t)
        states += [k5[:, t - keep:, g], v5[:, t - keep:, g]]
    return xf, states


def _ssd_mixer(xf, b, t, ln, conv0, h0, W):
    d = D_MODEL
    xn = _pad_time(rmsnorm(xf, ln).reshape(b, t, d), SSM_CHUNK)
    tp = xn.shape[1]
    xn = xn.reshape(b * tp, d)
    z = linear(xn, W['ssm_win'], col0=0, n=SSM_DINNER)
    xbc = linear(xn, W['ssm_win'], col0=SSM_DINNER, n=SSM_CONV_DIM).reshape(b, tp, SSM_CONV_DIM)
    dt_raw = linear(xn, W['ssm_win'][:, SSM_DINNER + SSM_CONV_DIM:])
    prev8 = jnp.pad(conv0.astype(F32), ((0, 0), (8 - (SSM_CONV - 1), 0), (0, 0)))
    xa = causal_conv_silu(xbc, prev8, W['ssm_conv_w'], W['ssm_conv_b'])
    y, hf = ssd_scan(xa, z.reshape(b, tp, SSM_DINNER), dt_raw.reshape(b, tp, SSM_HEADS),
                     W['ssm_dt_bias'], W['ssm_a_log'], W['ssm_d'], W['ssm_norm_w'],
                     h0.astype(F32).reshape(b, SSM_DINNER, SSM_STATE), t_valid=t)
    xf = linear(y[:, :t].reshape(b * t, SSM_DINNER), W['ssm_wout'], res=xf)
    conv_state = jnp.concatenate([conv0.astype(F32), xbc[:, :t]], axis=1)[:, t:]
    return xf, conv_state, hf.reshape(b, SSM_HEADS, SSM_HEADDIM, SSM_STATE)


def _run_group(x, pos0, mem_k, mem_v, shift0, wkv0, sb_past, dil_bufs, conv0, h0, W):
    b, t, d = x.shape
    xf = x.reshape(b * t, d)
    st = {}
    for i in range(DEPTH):
        xf = ffn(xf, W['ln_ffn1'][i], W['ffn1_gate'], W['ffn1_up'], W['ffn1_down'], i)
        kind = i % 4
        ln = W['ln_mix'][i]
        if kind == 0:
            xf, st['rwkv_shift'], st['rwkv_wkv'] = _rwkv_mixer(xf, b, t, ln, shift0, wkv0, W)
        elif kind == 1:
            xf, st['sb_k'], st['sb_v'] = _sb_mixer(xf, b, t, ln, sb_past, W)
        elif kind == 2:
            xf, st['dil'] = _dil_mixer(xf, b, t, pos0, ln, dil_bufs, W)
        else:
            xf, st['ssm_conv'], st['ssm_h'] = _ssd_mixer(xf, b, t, ln, conv0, h0, W)
        xn = rmsnorm(xf, W['ln_mem'][i])
        q = linear(xn, W['mem_wq'], layer=i, gain=W['mem_gq'][i], hw=MEM_HD, out_dtype=BF16)
        q = _pad_time(q.reshape(b, t, d), 8)
        o = mem_attention(q, mem_k, mem_v, i)[:, :t].reshape(b * t, d)
        xf = linear(o, W['mem_wo'], layer=i, res=xf)
        xf = ffn(xf, W['ln_ffn2'][i], W['ffn2_gate'], W['ffn2_up'], W['ffn2_down'], i)
    return xf.reshape(b, t, d), st


def kernel(x_prompt, x_sample, state_rwkv_shift, state_rwkv_wkv, cache_sb_k, cache_sb_v, cache_dil0_k, cache_dil0_v, cache_dil1_k, cache_dil1_v, cache_dil2_k, cache_dil2_v, state_ssm_conv, state_ssm_h, cache_mem_k, cache_mem_v, page_table, mem_prompt, ln_ffn1, ffn1_gate, ffn1_up, ffn1_down, ln_mix, ln_mem, mem_wq, mem_gq, mem_wk, mem_gk, mem_wv, mem_wo, ln_ffn2, ffn2_gate, ffn2_up, ffn2_down, rwkv_mu, rwkv_wr, rwkv_wk, rwkv_wv, rwkv_wo, rwkv_w0, rwkv_w1, rwkv_w2, rwkv_a0, rwkv_a1, rwkv_a2, rwkv_g1, rwkv_g2, rwkv_kk, rwkv_ka, rwkv_rk, rwkv_gn_w, rwkv_gn_b, sb_wqkv, sb_gq, sb_gk, sb_bias, sb_wo, dil_wqkv, dil_gq, dil_gk, dil_wo, ssm_win, ssm_conv_w, ssm_conv_b, ssm_dt_bias, ssm_a_log, ssm_d, ssm_norm_w, ssm_wout):
    W = dict(locals())
    bp, _, d = x_prompt.shape
    bs = x_sample.shape[0]
    n_mem = mem_prompt.shape[1]
    past_len = page_table.shape[1] * cache_sb_k.shape[1]

    mem_rows = mem_prompt.reshape(bp * n_mem, d)
    p_mem_k = jnp.stack([linear(mem_rows, mem_wk, layer=i, gain=mem_gk[i], hw=MEM_HD)
                         for i in range(DEPTH)]).reshape(DEPTH, bp, n_mem, d)
    p_mem_v = jnp.stack([linear(mem_rows, mem_wv, layer=i)
                         for i in range(DEPTH)]).reshape(DEPTH, bp, n_mem, d)
    y_p, sp = _run_group(
        x_prompt, 0, p_mem_k, p_mem_v,
        jnp.zeros((bp, d), F32), jnp.zeros((bp, d // RWKV_HD, RWKV_HD, RWKV_HD), F32),
        None, None,
        jnp.zeros((bp, SSM_CONV - 1, SSM_CONV_DIM), F32),
        jnp.zeros((bp, SSM_HEADS, SSM_HEADDIM, SSM_STATE), F32), W)

    y_s, ss = _run_group(
        x_sample, past_len, cache_mem_k.reshape(DEPTH, bs, n_mem, d), cache_mem_v.reshape(DEPTH, bs, n_mem, d),
        state_rwkv_shift, state_rwkv_wkv, (cache_sb_k, cache_sb_v, page_table),
        (cache_dil0_k, cache_dil0_v, cache_dil1_k, cache_dil1_v, cache_dil2_k, cache_dil2_v),
        state_ssm_conv, state_ssm_h, W)

    mem_shape = (DEPTH, bp, n_mem, MEM_HEADS, MEM_HD)
    dil = []
    for g in range(len(DIL_PATTERNS)):
        dil += [sp['dil'][2 * g], sp['dil'][2 * g + 1], ss['dil'][2 * g], ss['dil'][2 * g + 1]]
    return (y_p, y_s,
            sp['rwkv_shift'], ss['rwkv_shift'], sp['rwkv_wkv'], ss['rwkv_wkv'],
            sp['sb_k'], sp['sb_v'], ss['sb_k'], ss['sb_v'],
            *dil,
            sp['ssm_conv'], ss['ssm_conv'], sp['ssm_h'], ss['ssm_h'],
            p_mem_k.reshape(mem_shape), p_mem_v.reshape(mem_shape))
```

```python
import functools
import math

import jax
import jax.numpy as jnp
from jax import lax
from jax.experimental import pallas as pl
from jax.experimental.pallas import tpu as pltpu

F32 = jnp.float32
BF16 = jnp.bfloat16

D_MODEL = 2048
DEPTH = 4
NORM_EPS = 1e-6
PAGE_SIZE = 128
MEM_HEADS = 4
MEM_HD = D_MODEL // MEM_HEADS
RWKV_HD = 64
GN_EPS = 64e-5
SB_HD = 128
SB_HEADS = D_MODEL // SB_HD
DIL_PATTERNS = ((128, 1), (512, 4), (2048, 16))
DIL_HD = 128
DIL_HEADS = 8
ROPE_THETA = 500000.0
ROPE_DIM = DIL_HD // 4
SSM_DINNER = 2 * D_MODEL
SSM_HEADDIM = 64
SSM_HEADS = SSM_DINNER // SSM_HEADDIM
SSM_STATE = 128
SSM_GROUPS = 8
SSM_CONV = 4
SSM_CONV_DIM = SSM_DINNER + 2 * SSM_GROUPS * SSM_STATE
SSM_CHUNK = 128
SSM_NORM_EPS = 1e-5

LANES = 128
V7X_VMEM_BYTES = 64 << 20
VMEM_LIMIT = V7X_VMEM_BYTES - (8 << 20)
RWKV_CHUNK = 64
RWKV_PAIRS_PER_ITER = 4
RWKV_PAIRS_PER_DOT = 2
NEG_BIG = -1e30


def _params(*sem):
    return pltpu.CompilerParams(dimension_semantics=sem, vmem_limit_bytes=VMEM_LIMIT)


def _dot(a, b):
    return jnp.dot(a.astype(BF16), b.astype(BF16), preferred_element_type=F32)


def _dot_nt(a, b):
    return lax.dot_general(a.astype(BF16), b.astype(BF16), (((1,), (1,)), ((), ())),
                           preferred_element_type=F32)


def _split3(x):
    hi = x.astype(BF16)
    r1 = x - hi.astype(F32)
    mid = r1.astype(BF16)
    lo = (r1 - mid.astype(F32)).astype(BF16)
    return hi, mid, lo


def _dot_exact_rhs(x, m_bf16):
    hi, mid, lo = _split3(x)
    d = lambda p: jnp.dot(p, m_bf16, preferred_element_type=F32)
    return d(hi) + d(mid) + d(lo)


def _dot_exact_lhs(m_bf16, x):
    hi, mid, lo = _split3(x)
    d = lambda p: jnp.dot(m_bf16, p, preferred_element_type=F32)
    return d(hi) + d(mid) + d(lo)


def _iota(shape, dim):
    return lax.broadcasted_iota(jnp.int32, shape, dim)


def _softplus(x):
    return jnp.maximum(x, 0.0) + jnp.log1p(jnp.exp(-jnp.abs(x)))


def _log_sigmoid(x):
    return jnp.minimum(x, 0.0) - jnp.log(1.0 + jnp.exp(-jnp.abs(x)))


def _silu(x):
    return x * jax.nn.sigmoid(x)


def _linear_body(*refs, has_gain, has_rope, has_res, act, hw, res_scale):
    it = iter(refs)
    x_ref, w_ref = next(it), next(it)
    gain_ref = next(it) if has_gain else None
    cos_ref, sin_ref = (next(it), next(it)) if has_rope else (None, None)
    res_ref = next(it) if has_res else None
    o_ref, wbf_ref = next(it), next(it)

    @pl.when(pl.program_id(1) == 0)
    def _():
        wbf_ref[...] = w_ref[...].astype(BF16)

    acc = jnp.dot(x_ref[...].astype(BF16), wbf_ref[...], preferred_element_type=F32)
    if act is not None:
        acc = act(acc)
    if has_res:
        acc = res_ref[...] + res_scale * acc
    if has_gain:
        for s in range(acc.shape[1] // hw):
            y = acc[:, s * hw:(s + 1) * hw]
            ms = jnp.mean(y * y, axis=-1, keepdims=True)
            y = y * lax.rsqrt(ms + NORM_EPS) * gain_ref[...]
            if has_rope:
                lane = _iota(y.shape, 1)
                half = ROPE_DIM // 2
                rot = jnp.where(lane < half, pltpu.roll(y, hw - half, 1), pltpu.roll(y, half, 1))
                y = y * cos_ref[...] + rot * sin_ref[...]
            o_ref[:, s * hw:(s + 1) * hw] = y.astype(o_ref.dtype)
    else:
        o_ref[...] = acc.astype(o_ref.dtype)


def linear(x, w, *, col0=0, n=None, out_dtype=F32, gain=None, hw=None, rope=None, res=None,
           res_scale=1.0, act=None, layer=None, tm=512, tn=512, name="linear"):
    m, k = x.shape
    n = w.shape[-1] - col0 if n is None else n
    tm, tn = min(tm, m), min(tn, n)
    assert m % tm == 0 and n % tn == 0 and col0 % tn == 0 and w.shape[-2] == k
    cb = col0 // tn
    if layer is None:
        w_spec = pl.BlockSpec((k, tn), lambda j, i: (0, j + cb))
    else:
        w_spec = pl.BlockSpec((None, k, tn), lambda j, i: (layer, 0, j + cb))
    in_specs = [pl.BlockSpec((tm, k), lambda j, i: (i, 0)), w_spec]
    args = [x, w]
    if gain is not None:
        assert tn % hw == 0
        in_specs.append(pl.BlockSpec((1, hw), lambda j, i: (0, 0)))
        args.append(gain.reshape(1, hw).astype(F32))
    if rope is not None:
        cos, sin = rope
        nt = cos.shape[0] // tm
        assert hw == LANES and cos.shape[0] % tm == 0
        in_specs += [pl.BlockSpec((tm, LANES), lambda j, i: (i % nt, 0))] * 2
        args += [cos, sin]
    if res is not None:
        in_specs.append(pl.BlockSpec((tm, tn), lambda j, i: (i, j)))
        args.append(res)
    body = functools.partial(_linear_body, has_gain=gain is not None, has_rope=rope is not None,
                             has_res=res is not None, act=act, hw=hw, res_scale=res_scale)
    return pl.pallas_call(
        body,
        grid=(n // tn, m // tm),
        in_specs=in_specs,
        out_specs=pl.BlockSpec((tm, tn), lambda j, i: (i, j)),
        out_shape=jax.ShapeDtypeStruct((m, n), out_dtype),
        scratch_shapes=[pltpu.VMEM((k, tn), BF16)],
        compiler_params=_params("parallel", "arbitrary"),
        name=name,
    )(*args)


def _rmsnorm_body(x_ref, g_ref, o_ref):
    x = x_ref[...]
    ms = jnp.mean(x * x, axis=-1, keepdims=True)
    o_ref[...] = (x * lax.rsqrt(ms + NORM_EPS) * g_ref[...]).astype(o_ref.dtype)


def rmsnorm(x, g, out_dtype=BF16, tm=512):
    m, d = x.shape
    tm = min(tm, m)
    assert m % tm == 0
    return pl.pallas_call(
        _rmsnorm_body,
        grid=(m // tm,),
        in_specs=[pl.BlockSpec((tm, d), lambda i: (i, 0)), pl.BlockSpec((1, d), lambda i: (0, 0))],
        out_specs=pl.BlockSpec((tm, d), lambda i: (i, 0)),
        out_shape=jax.ShapeDtypeStruct((m, d), out_dtype),
        compiler_params=_params("parallel"),
        name="rmsnorm",
    )(x, g.reshape(1, d))


def _ffn_body(x_ref, g_ref, wg_ref, wu_ref, wd_ref, o_ref, xn_ref):
    @pl.when(pl.program_id(1) == 0)
    def _():
        x = x_ref[...]
        ms = jnp.mean(x * x, axis=-1, keepdims=True)
        xn_ref[...] = (x * lax.rsqrt(ms + NORM_EPS) * g_ref[...]).astype(BF16)
        o_ref[...] = x

    xn = xn_ref[...]
    gate = jnp.dot(xn, wg_ref[...].astype(BF16), preferred_element_type=F32)
    up = jnp.dot(xn, wu_ref[...].astype(BF16), preferred_element_type=F32)
    h = (0.5 * _silu(gate) * up).astype(BF16)
    o_ref[...] += jnp.dot(h, wd_ref[...].astype(BF16), preferred_element_type=F32)


def ffn(x, g, w_gate, w_up, w_down, layer, tm=1024, tf=256):
    m, d = x.shape
    f = w_gate.shape[-1]
    tm = min(tm, m)
    assert m % tm == 0 and f % tf == 0
    return pl.pallas_call(
        _ffn_body,
        grid=(m // tm, f // tf),
        in_specs=[pl.BlockSpec((tm, d), lambda i, j: (i, 0)),
                  pl.BlockSpec((1, d), lambda i, j: (0, 0)),
                  pl.BlockSpec((None, d, tf), lambda i, j: (layer, 0, j)),
                  pl.BlockSpec((None, d, tf), lambda i, j: (layer, 0, j)),
                  pl.BlockSpec((None, tf, d), lambda i, j: (layer, j, 0))],
        out_specs=pl.BlockSpec((tm, d), lambda i, j: (i, 0)),
        out_shape=jax.ShapeDtypeStruct((m, d), F32),
        scratch_shapes=[pltpu.VMEM((tm, d), BF16)],
        compiler_params=_params("parallel", "arbitrary"),
        name="ffn",
    )(x, g.reshape(1, d), w_gate, w_up, w_down)


def _memattn_body(q_ref, k_ref, v_ref, o_ref, *, scale):
    s = _dot_nt(q_ref[0], k_ref[0, 0]) * scale
    m = jnp.max(s, axis=-1, keepdims=True)
    p = jnp.exp(s - m)
    l = jnp.sum(p, axis=-1, keepdims=True)
    o_ref[0] = (_dot(p, v_ref[0, 0]) / l).astype(o_ref.dtype)


def mem_attention(q, mem_k, mem_v, layer, tq=512):
    b, t, d = q.shape
    n_mem = mem_k.shape[2]
    tq = min(tq, t)
    assert t % tq == 0
    kv_spec = pl.BlockSpec((1, 1, n_mem, MEM_HD), lambda bi, ti, h: (layer, bi, 0, h))
    return pl.pallas_call(
        functools.partial(_memattn_body, scale=MEM_HD ** -0.5),
        grid=(b, t // tq, MEM_HEADS),
        in_specs=[pl.BlockSpec((1, tq, MEM_HD), lambda bi, ti, h: (bi, ti, h)), kv_spec, kv_spec],
        out_specs=pl.BlockSpec((1, tq, MEM_HD), lambda bi, ti, h: (bi, ti, h)),
        out_shape=jax.ShapeDtypeStruct((b, t, d), BF16),
        compiler_params=_params("parallel", "parallel", "parallel"),
        name="mem_attention",
    )(q, mem_k, mem_v)


def rope_tables(pos):
    half = ROPE_DIM // 2
    inv_freq = ROPE_THETA ** (-jnp.arange(half, dtype=F32) / half)
    ang = pos.astype(F32)[:, None] * inv_freq[None, :]
    cos, sin = jnp.cos(ang), jnp.sin(ang)
    rest = DIL_HD - ROPE_DIM
    n = pos.shape[0]
    cos_t = jnp.concatenate([cos, cos, jnp.ones((n, rest), F32)], axis=1)
    sin_t = jnp.concatenate([-sin, sin, jnp.zeros((n, rest), F32)], axis=1)
    return cos_t, sin_t


def _sb_tile(q, k, v, bias, causal, carry, acc, upper):
    z = _dot_nt(q, k) * (SB_HD ** -0.5) + bias
    ls = _log_sigmoid(z)
    log_keep = jnp.where(causal, ls - z, 0.0)
    hi = log_keep.astype(BF16)
    lo = (log_keep - hi.astype(F32)).astype(BF16)
    between = (jnp.dot(hi, upper, preferred_element_type=F32)
               + jnp.dot(lo, upper, preferred_element_type=F32) + carry)
    att = jnp.where(causal, jnp.exp(ls + between), 0.0)
    acc = acc + _dot(att, v)
    carry = carry + jnp.sum(log_keep, axis=-1, keepdims=True)
    return carry, acc


def _upper_ones(n):
    return jnp.where(_iota((n, n), 0) > _iota((n, n), 1), 1.0, 0.0).astype(BF16)


def _sb_prompt_body(bias_ref, q_ref, k_ref, v_ref, o_ref, *, tq, heads):
    hb, qi = pl.program_id(1), pl.program_id(2)
    upper = _upper_ones(tq)
    row = qi * tq + _iota((tq, tq), 0)
    lanes = [slice(u * SB_HD, (u + 1) * SB_HD) for u in range(heads)]
    qs = [q_ref[0, :, sl] for sl in lanes]
    biases = [bias_ref[hb * heads + u] for u in range(heads)]

    def step(jj, state):
        j = qi - jj
        start = pl.multiple_of(j * tq, tq)
        causal = (j * tq + _iota((tq, tq), 1)) < row
        out = []
        for u, sl in enumerate(lanes):
            k = k_ref[0, pl.ds(start, tq), sl]
            v = v_ref[0, pl.ds(start, tq), sl]
            out.append(_sb_tile(qs[u], k, v, biases[u], causal, *state[u], upper))
        return tuple(out)

    init = tuple((jnp.zeros((tq, 1), F32), jnp.zeros((tq, SB_HD), F32)) for _ in lanes)
    final = lax.fori_loop(0, qi + 1, step, init)
    for u, sl in enumerate(lanes):
        o_ref[0, :, sl] = final[u][1].astype(o_ref.dtype)


def sb_attention_prompt(q, k, v, bias, tq=256, heads=2):
    b, t, d = q.shape
    tq = min(tq, t)
    w = heads * SB_HD
    assert t % tq == 0 and d % w == 0
    kv_spec = pl.BlockSpec((1, t, w), lambda bi, h, qi: (bi, 0, h))
    io_spec = pl.BlockSpec((1, tq, w), lambda bi, h, qi: (bi, qi, h))
    return pl.pallas_call(
        functools.partial(_sb_prompt_body, tq=tq, heads=heads),
        grid=(b, d // w, t // tq),
        in_specs=[pl.BlockSpec(memory_space=pltpu.SMEM), io_spec, kv_spec, kv_spec],
        out_specs=io_spec,
        out_shape=jax.ShapeDtypeStruct((b, t, d), BF16),
        compiler_params=_params("parallel", "parallel", "arbitrary"),
        name="sb_attention_prompt",
    )(bias, q, k, v)


def _sb_decode_body(table_ref, bias_ref, q_ref, kn_ref, vn_ref, *rest, n_pages, per_step):
    del table_ref
    page_refs = rest[:2 * per_step]
    o_ref, qbd_ref, acc_ref, carry_ref = rest[2 * per_step:]
    j = pl.program_id(1)
    nh, d = SB_HEADS, SB_HEADS * SB_HD
    own = (_iota((nh, d), 1) // SB_HD) == _iota((nh, d), 0)
    bias = bias_ref[...]
    past_len = n_pages * PAGE_SIZE

    @pl.when(j == 0)
    def _():
        qbd = jnp.where(own, jnp.broadcast_to(q_ref[0], (nh, d)), 0.0)
        qbd_ref[...] = qbd.astype(BF16)
        z = jnp.sum(qbd * kn_ref[0], axis=-1, keepdims=True) * (SB_HD ** -0.5) + bias
        causal = jnp.full((nh, 1), past_len, jnp.int32) < past_len
        ls = _log_sigmoid(z)
        carry_ref[...] = jnp.where(causal, ls - z, 0.0)
        att = jnp.where(causal, jnp.exp(ls), 0.0)
        acc_ref[...] = jnp.where(own, att * vn_ref[0], 0.0)

    qbd = qbd_ref[...]
    upper = _upper_ones(PAGE_SIZE)
    carry = carry_ref[...]
    total = jnp.zeros((nh, d), F32)
    for u in range(per_step):
        page = n_pages - 1 - (j * per_step + u)
        causal = (page * PAGE_SIZE + _iota((nh, PAGE_SIZE), 1)) < past_len
        k2d, v2d = page_refs[2 * u], page_refs[2 * u + 1]
        kp = jnp.concatenate([k2d[pl.ds(h, PAGE_SIZE, stride=nh), :] for h in range(nh)], axis=1)
        vp = jnp.concatenate([v2d[pl.ds(h, PAGE_SIZE, stride=nh), :] for h in range(nh)], axis=1)
        carry, total = _sb_tile(qbd, kp, vp, bias, causal, carry, total, upper)
    carry_ref[...] = carry
    acc_ref[...] += jnp.where(own, total, 0.0)

    @pl.when(j == pl.num_programs(1) - 1)
    def _():
        o_ref[0] = jnp.sum(acc_ref[...], axis=0, keepdims=True)


def sb_attention_decode(q, k_new, v_new, pool_k, pool_v, table, bias, per_step=2):
    b, _, d = q.shape
    n_pages = table.shape[1]
    assert n_pages % per_step == 0 and pool_k.shape[1:] == (PAGE_SIZE, SB_HEADS, SB_HD)
    page_rows = PAGE_SIZE * SB_HEADS
    pool_k = pool_k.reshape(-1, SB_HD)
    pool_v = pool_v.reshape(-1, SB_HD)
    row_spec = pl.BlockSpec((1, 1, d), lambda bi, j, tab: (bi, 0, 0))

    def page_spec(u):
        return pl.BlockSpec((page_rows, SB_HD),
                            lambda bi, j, tab: (tab[bi, n_pages - 1 - (j * per_step + u)], 0))

    pages = [page_spec(u) for u in range(per_step) for _ in range(2)]
    grid_spec = pltpu.PrefetchScalarGridSpec(
        num_scalar_prefetch=1,
        grid=(b, n_pages // per_step),
        in_specs=[pl.BlockSpec((SB_HEADS, 1), lambda bi, j, tab: (0, 0)),
                  row_spec, row_spec, row_spec] + pages,
        out_specs=row_spec,
        scratch_shapes=[pltpu.VMEM((SB_HEADS, d), BF16), pltpu.VMEM((SB_HEADS, d), F32),
                        pltpu.VMEM((SB_HEADS, 1), F32)],
    )
    return pl.pallas_call(
        functools.partial(_sb_decode_body, n_pages=n_pages, per_step=per_step),
        grid_spec=grid_spec,
        out_shape=jax.ShapeDtypeStruct((b, 1, d), F32),
        compiler_params=_params("parallel", "arbitrary"),
        name="sb_attention_decode",
    )(table, bias.reshape(SB_HEADS, 1), q, k_new, v_new, *([pool_k, pool_v] * per_step))


def _dil_prompt_body(*refs, t, blk):
    n_grp = len(DIL_PATTERNS)
    q_refs, k_refs, v_refs = refs[:n_grp], refs[n_grp:2 * n_grp], refs[2 * n_grp:3 * n_grp]
    o_ref, m_s, l_s, acc_s = refs[3 * n_grp:3 * n_grp + 4]
    scale = DIL_HD ** -0.5
    order = sorted(range(n_grp), key=lambda g: -DIL_PATTERNS[g][1])
    for n, g in enumerate(order):
        win, dil = DIL_PATTERNS[g]
        band, cls_len = win // dil, t // dil
        n_blk = cls_len // blk
        n_keys = min(2 * blk, cls_len)
        first, last = n == 0, n == n_grp - 1
        assert band == blk and (not last or dil == 1) and n_blk * blk == cls_len
        q_ref, k_ref, v_ref = q_refs[g], k_refs[g], v_refs[g]
        rel = _iota((blk, n_keys), 0) - _iota((blk, n_keys), 1)

        def rows(start, size, dil=dil):
            return pl.ds(start, size, stride=dil) if dil > 1 else pl.ds(start, size)

        def block(idx, q_ref=q_ref, k_ref=k_ref, v_ref=v_ref, dil=dil, n_blk=n_blk, n_keys=n_keys,
                  rel=rel, rows=rows, first=first, last=last, band=band):
            r, lb = idx // n_blk, idx % n_blk
            k_cls = jnp.maximum(lb - 1, 0) * blk
            if dil == 1:
                q_tok, k_tok = pl.multiple_of(lb * blk, blk), pl.multiple_of(k_cls, blk)
            else:
                q_tok, k_tok = r + lb * blk * dil, r + k_cls * dil
            q = q_ref[0, rows(q_tok, blk), :]
            k = k_ref[0, rows(k_tok, n_keys), :]
            v = v_ref[0, rows(k_tok, n_keys), :]
            dist = rel + (lb * blk - k_cls)
            valid = jnp.where(dist >= 0, dist, band + 1) <= band
            s = jnp.where(valid, _dot_nt(q, k) * scale, NEG_BIG)
            m = jnp.max(s, axis=-1, keepdims=True)
            p = jnp.exp(s - m)
            l = jnp.sum(p, axis=-1, keepdims=True)
            acc = _dot(p, v)
            sel = rows(q_tok, blk)
            if not first:
                m0, l0, a0 = m_s[sel, :], l_s[sel, :], acc_s[sel, :]
                m_new = jnp.maximum(m0, m)
                e0, e1 = jnp.exp(m0 - m_new), jnp.exp(m - m_new)
                m, l, acc = m_new, e0 * l0 + e1 * l, e0 * a0 + e1 * acc
            if last:
                o_ref[0, sel, :] = (acc / l).astype(o_ref.dtype)
            else:
                m_s[sel, :], l_s[sel, :], acc_s[sel, :] = m, l, acc

        def two(i, _, block=block):
            block(2 * i)
            block(2 * i + 1)
            return 0

        lax.fori_loop(0, dil * n_blk // 2, two, 0)


def dil_attention_prompt(q, k, v, blk=128):
    b, t, _ = q.shape
    n_grp = len(DIL_PATTERNS)
    assert t % blk == 0

    def spec(g):
        return pl.BlockSpec((1, t, DIL_HD), lambda bi, h: (bi, 0, g * DIL_HEADS + h))

    return pl.pallas_call(
        functools.partial(_dil_prompt_body, t=t, blk=blk),
        grid=(b, DIL_HEADS),
        in_specs=[spec(g) for g in range(n_grp)] * 3,
        out_specs=pl.BlockSpec((1, t, DIL_HD), lambda bi, h: (bi, 0, h)),
        out_shape=jax.ShapeDtypeStruct((b, t, DIL_HEADS * DIL_HD), BF16),
        scratch_shapes=[pltpu.VMEM((t, 1), F32), pltpu.VMEM((t, 1), F32), pltpu.VMEM((t, DIL_HD), F32)],
        compiler_params=_params("parallel", "parallel"),
        name="dil_attention_prompt",
    )(*([q] * n_grp + [k] * n_grp + [v] * n_grp))


def _dil_decode_body(*refs):
    n_grp = len(DIL_PATTERNS)
    q_ref, kn_ref, vn_ref = refs[:3]
    kb_refs, vb_refs = refs[3:3 + n_grp], refs[3 + n_grp:3 + 2 * n_grp]
    o_ref = refs[3 + 2 * n_grp]
    scale = DIL_HD ** -0.5
    for h in range(DIL_HEADS):
        scores, news = [], []
        for g in range(n_grp):
            sl = slice((g * DIL_HEADS + h) * DIL_HD, (g * DIL_HEADS + h + 1) * DIL_HD)
            q = q_ref[0, :, sl]
            kb = kb_refs[g][0, :, h * DIL_HD:(h + 1) * DIL_HD]
            scores.append(jnp.sum(kb * q, axis=-1, keepdims=True) * scale)
            news.append(jnp.sum(kn_ref[0, :, sl] * q, axis=-1, keepdims=True) * scale)
        m = functools.reduce(jnp.maximum, [jnp.max(s, axis=0, keepdims=True) for s in scores] + news)
        l = jnp.zeros((1, 1), F32)
        acc = jnp.zeros((1, DIL_HD), F32)
        for g in range(n_grp):
            sl = slice((g * DIL_HEADS + h) * DIL_HD, (g * DIL_HEADS + h + 1) * DIL_HD)
            p = jnp.exp(scores[g] - m)
            pn = jnp.exp(news[g] - m)
            vb = vb_refs[g][0, :, h * DIL_HD:(h + 1) * DIL_HD]
            l = l + jnp.sum(p, axis=0, keepdims=True) + pn
            acc = acc + jnp.sum(p * vb, axis=0, keepdims=True) + pn * vn_ref[0, :, sl]
        o_ref[0, :, h * DIL_HD:(h + 1) * DIL_HD] = acc / l


def dil_attention_decode(q, k_new, v_new, bufs):
    b = q.shape[0]
    n_grp = len(DIL_PATTERNS)
    d = DIL_HEADS * DIL_HD
    row_spec = pl.BlockSpec((1, 1, n_grp * d), lambda bi: (bi, 0, 0))
    views, specs = [], []
    for which in range(2):
        for g, (win, dil) in enumerate(DIL_PATTERNS):
            buf = bufs[2 * g + which]
            assert buf.shape[1] == win and win % dil == 0
            views.append(buf.reshape(b, win // dil, dil * d))
            specs.append(pl.BlockSpec((1, win // dil, d), lambda bi: (bi, 0, 0)))
    order = [views[g] for g in range(n_grp)] + [views[n_grp + g] for g in range(n_grp)]
    return pl.pallas_call(
        _dil_decode_body,
        grid=(b,),
        in_specs=[row_spec] * 3 + specs,
        out_specs=pl.BlockSpec((1, 1, d), lambda bi: (bi, 0, 0)),
        out_shape=jax.ShapeDtypeStruct((b, 1, d), F32),
        compiler_params=_params("parallel"),
        name="dil_attention_decode",
    )(q, k_new, v_new, *order)


def _rwkv_mix_body(x_ref, xp_ref, mu_ref, *o_refs):
    x = x_ref[...]
    xx = xp_ref[...] - x
    for i, o_ref in enumerate(o_refs):
        o_ref[...] = (x + xx * mu_ref[i:i + 1, :]).astype(o_ref.dtype)


def rwkv_mix(xn, x_prev, mu, tm=512):
    m, d = xn.shape
    tm = min(tm, m)
    n_mix = mu.shape[0]
    row = pl.BlockSpec((tm, d), lambda i: (i, 0))
    return pl.pallas_call(
        _rwkv_mix_body,
        grid=(m // tm,),
        in_specs=[row, row, pl.BlockSpec((n_mix, d), lambda i: (0, 0))],
        out_specs=[row] * n_mix,
        out_shape=[jax.ShapeDtypeStruct((m, d), BF16)] * n_mix,
        compiler_params=_params("parallel"),
        name="rwkv_mix",
    )(xn, x_prev, mu)


def _rwkv_scan_body(r_ref, k_ref, v_ref, wl_ref, al_ref, g_ref, prm_ref, s0_ref, y_ref, sf_ref,
                    st_ref, *, t_valid, t_total):
    c = pl.program_id(1)
    C, hd = RWKV_CHUNK, RWKV_HD
    n2 = 2 * C
    grp = RWKV_PAIRS_PER_DOT
    rows = grp * n2

    @pl.when(c == 0)
    def _():
        st_ref[...] = s0_ref[0]

    left = _iota((C, LANES), 1) < hd
    ri, ci = _iota((rows, rows), 0), _iota((rows, rows), 1)
    strict = (ci & (C - 1)) < (ri & (C - 1))
    incl = (ci & (C - 1)) <= (ri & (C - 1))
    tri = jnp.where(_iota((C, C), 1) <= _iota((C, C), 0), 1.0, 0.0).astype(BF16)
    same_head = (_iota((LANES, LANES), 0) // hd) == (_iota((LANES, LANES), 1) // hd)
    seg_ones = jnp.where(same_head, 1.0, 0.0).astype(BF16)
    masked = t_valid < t_total
    rows_valid = (c * C + _iota((C, LANES), 0)) < t_valid
    zero_slab = jnp.zeros((n2, LANES), F32)

    def seg_sum(x):
        hi = x.astype(BF16)
        lo = (x - hi.astype(F32)).astype(BF16)
        y = jnp.dot(jnp.concatenate([hi, lo], axis=0), seg_ones, preferred_element_type=F32)
        return y[:x.shape[0]] + y[x.shape[0]:]

    def cumsum_steps(x):
        w = x.shape[1]
        y = jnp.dot(tri, jnp.concatenate(_split3(x), axis=1), preferred_element_type=F32)
        return y[:, :w] + y[:, w:2 * w] + y[:, 2 * w:]

    def stack(x):
        return jnp.concatenate([jnp.where(left, x, 0.0), jnp.where(left, 0.0, x)], axis=0)

    def spread(slabs):
        return jnp.concatenate(
            [jnp.concatenate([s if q == j else zero_slab for j in range(grp)], axis=1)
             for q, s in enumerate(slabs)], axis=0)

    def own_lanes(x):
        return jnp.concatenate([x[q * n2:(q + 1) * n2, q * LANES:(q + 1) * LANES] for q in range(grp)],
                               axis=0)

    def core(ds, sts):
        ar = jnp.concatenate([spread([stack(d['a_t']) for d in ds]),
                              spread([stack(d['r_t']) for d in ds])], axis=0).astype(BF16)
        bk = jnp.concatenate([spread([stack(d['b_t']) for d in ds]),
                              spread([stack(d['k_t']) for d in ds])], axis=0).astype(BF16)
        gram = _dot_nt(ar, bk)
        st_bd = spread(sts)
        a_s = _dot_nt(ar, st_bd)
        vv = jnp.concatenate([d['v'] for d in ds for _ in range(2)], axis=0)
        y = _dot(jnp.where(strict, gram[:rows, rows:], 0.0), vv) + own_lanes(a_s[:rows])
        pw = jnp.where(strict, gram[:rows, :rows], 0.0)
        n = 1
        while n < C:
            y = y + _dot(pw, y)
            n *= 2
            if n < C:
                pw = _dot(pw, pw)
        o2 = _dot(jnp.concatenate([jnp.where(incl, gram[rows:, :rows], 0.0),
                                   jnp.where(incl, gram[rows:, rows:], 0.0)], axis=1),
                  jnp.concatenate([y, vv], axis=0)) + own_lanes(a_s[rows:])
        pick = lambda t, q: jnp.where(left, t[q * n2:q * n2 + C], t[q * n2 + C:(q + 1) * n2])
        os = [pick(o2, q) for q in range(grp)]
        uv = jnp.concatenate([jnp.concatenate([pick(y, q), d['v']], axis=0)
                              for q, d in enumerate(ds)], axis=1)
        bk_end = jnp.concatenate([jnp.concatenate([d['b_end'], d['k_end']], axis=0) for d in ds], axis=1)
        upd = _dot(uv.T, bk_end)
        s_new = [jnp.where(same_head, st * d['decay_end']
                           + upd[q * LANES:(q + 1) * LANES, q * LANES:(q + 1) * LANES], 0.0)
                 for q, (d, st) in enumerate(zip(ds, sts))]
        return os, s_new

    def pairs(i, _):
        ps = [i * RWKV_PAIRS_PER_ITER + u for u in range(RWKV_PAIRS_PER_ITER)]
        offs = [pl.multiple_of(p * LANES, LANES) for p in ps]
        sts = [st_ref[p] for p in ps]
        ds = []
        for off in offs:
            r, k, v, wl, al, g = [ref[0, :, pl.ds(off, LANES)]
                                  for ref in (r_ref, k_ref, v_ref, wl_ref, al_ref, g_ref)]
            prm = prm_ref[:, pl.ds(off, LANES)]
            w0, a0, k_k, k_a, r_k, gn_w, gn_b = [prm[j:j + 1] for j in range(7)]
            a = jax.nn.sigmoid(a0 + al)
            ds.append(dict(r=r, v=v, g=g, a=a, gn_w=gn_w, gn_b=gn_b, r_k=r_k, kkr=k * k_k,
                           lw=-jnp.exp(-_softplus(-(w0 + wl)) - 0.5),
                           kmod=k * (1.0 + (a - 1.0) * k_a)))
        sums = seg_sum(jnp.concatenate([d['kkr'] * d['kkr'] for d in ds]
                                       + [d['r'] * d['kmod'] * d['r_k'] for d in ds], axis=0))
        n_p = len(ds)
        for u, d in enumerate(ds):
            d['kk'] = d['kkr'] * lax.rsqrt(jnp.maximum(sums[u * C:(u + 1) * C], 1e-24))
            d['bonus'] = sums[(n_p + u) * C:(n_p + u + 1) * C] * d['v']
            if masked:
                for name in ('lw', 'kk', 'kmod', 'v'):
                    d[name] = jnp.where(rows_valid, d[name], 0.0)
        lc_all = cumsum_steps(jnp.concatenate([d['lw'] for d in ds], axis=1))
        for u, d in enumerate(ds):
            lc = lc_all[:, u * LANES:(u + 1) * LANES]
            lend = lc[C - 1:C, :]
            kka = d['kk'] * d['a']
            e_inv, e_end = jnp.exp(-lc), jnp.exp(lend - lc)
            d.update(a_t=-d['kk'] * jnp.exp(lc - d['lw']), r_t=d['r'] * jnp.exp(lc),
                     b_t=kka * e_inv, k_t=d['kmod'] * e_inv, b_end=kka * e_end,
                     k_end=d['kmod'] * e_end, decay_end=jnp.exp(lend))
        os, s_news = [], []
        for q in range(0, n_p, grp):
            o_g, s_g = core(ds[q:q + grp], sts[q:q + grp])
            os += o_g
            s_news += s_g
        stats = seg_sum(jnp.concatenate(os + [o * o for o in os], axis=0))
        for u, (p, off, d, o) in enumerate(zip(ps, offs, ds, os)):
            mean = stats[u * C:(u + 1) * C] * (1.0 / hd)
            var = stats[(n_p + u) * C:(n_p + u + 1) * C] * (1.0 / hd) - mean * mean
            on = (o - mean) * lax.rsqrt(var + GN_EPS) * d['gn_w'] + d['gn_b']
            st_ref[p] = s_news[u]
            y_ref[0, :, pl.ds(off, LANES)] = ((on + d['bonus']) * d['g']).astype(y_ref.dtype)
        return 0

    lax.fori_loop(0, D_MODEL // LANES // RWKV_PAIRS_PER_ITER, pairs, 0)

    @pl.when(c == pl.num_programs(1) - 1)
    def _():
        sf_ref[0] = st_ref[...]


def rwkv_scan(r, k, v, wl, al, g, prm, s0, t_valid):
    b, t, d = r.shape
    C = RWKV_CHUNK
    assert t % C == 0 and d == D_MODEL
    seq = pl.BlockSpec((1, C, d), lambda bi, c: (bi, c, 0))
    st_spec = pl.BlockSpec((1, d // LANES, LANES, LANES), lambda bi, c: (bi, 0, 0, 0))
    return pl.pallas_call(
        functools.partial(_rwkv_scan_body, t_valid=t_valid, t_total=t),
        grid=(b, t // C),
        in_specs=[seq] * 6 + [pl.BlockSpec((8, d), lambda bi, c: (0, 0)), st_spec],
        out_specs=[seq, st_spec],
        out_shape=[jax.ShapeDtypeStruct((b, t, d), BF16),
                   jax.ShapeDtypeStruct((b, d // LANES, LANES, LANES), F32)],
        scratch_shapes=[pltpu.VMEM((d // LANES, LANES, LANES), F32)],
        compiler_params=_params("parallel", "arbitrary"),
        name="rwkv_scan",
    )(r, k, v, wl, al, g, prm, s0)


def _conv_body(x_ref, prev_ref, w_ref, b_ref, o_ref, buf_ref, *, tt):
    halo = 8

    @pl.when(pl.program_id(2) == 0)
    def _():
        buf_ref[0:halo, :] = prev_ref[0]

    buf_ref[halo:halo + tt, :] = x_ref[0]
    acc = b_ref[...] + buf_ref[halo:halo + tt, :] * w_ref[SSM_CONV - 1:SSM_CONV, :]
    for back in range(1, SSM_CONV):
        tap = SSM_CONV - 1 - back
        acc = acc + buf_ref[halo - back:halo - back + tt, :] * w_ref[tap:tap + 1, :]
    o_ref[0] = _silu(acc)
    buf_ref[0:halo, :] = buf_ref[tt:tt + halo, :]


def causal_conv_silu(x, prev8, w, bias, tt=256, tc=1024):
    b, t, ch = x.shape
    tt = min(tt, t)
    assert t % tt == 0 and ch % tc == 0
    return pl.pallas_call(
        functools.partial(_conv_body, tt=tt),
        grid=(b, ch // tc, t // tt),
        in_specs=[pl.BlockSpec((1, tt, tc), lambda bi, ci, ti: (bi, ti, ci)),
                  pl.BlockSpec((1, 8, tc), lambda bi, ci, ti: (bi, 0, ci)),
                  pl.BlockSpec((SSM_CONV, tc), lambda bi, ci, ti: (0, ci)),
                  pl.BlockSpec((1, tc), lambda bi, ci, ti: (0, ci))],
        out_specs=pl.BlockSpec((1, tt, tc), lambda bi, ci, ti: (bi, ti, ci)),
        out_shape=jax.ShapeDtypeStruct((b, t, ch), F32),
        scratch_shapes=[pltpu.VMEM((tt + 8, tc), F32)],
        compiler_params=_params("parallel", "parallel", "arbitrary"),
        name="causal_conv_silu",
    )(x, prev8, w, bias.reshape(1, ch))


def _ssd_body(xa_ref, z_ref, dt_ref, dtt_ref, dtb_ref, dtbt_ref, al_ref, alt_ref, dsk_ref, nw_ref,
              h0_ref, y_ref, hf_ref, h_ref, *, t_valid, t_total):
    c = pl.program_id(1)
    Q, P = SSM_CHUNK, SSM_HEADDIM
    per_grp = SSM_HEADS // SSM_GROUPS
    gw = per_grp * P

    @pl.when(c == 0)
    def _():
        h_ref[...] = h0_ref[0]

    tri = _iota((Q, Q), 1) <= _iota((Q, Q), 0)
    tri_b = jnp.where(tri, 1.0, 0.0).astype(BF16)
    upp_b = jnp.where(_iota((Q, Q), 0) <= _iota((Q, Q), 1), 1.0, 0.0).astype(BF16)
    left = _iota((Q, LANES), 1) < P
    masked = t_valid < t_total

    def group(g, _):
        dt = _softplus(dt_ref[0, g] + dtb_ref[g])
        dtt = _softplus(dtt_ref[0, g] + dtbt_ref[g])
        if masked:
            dt = jnp.where(c * Q + _iota(dt.shape, 0) < t_valid, dt, 0.0)
            dtt = jnp.where(c * Q + _iota(dtt.shape, 1) < t_valid, dtt, 0.0)
        cum = _dot_exact_lhs(tri_b, dt * -jnp.exp(al_ref[g]))
        cumt = _dot_exact_rhs(dtt * -jnp.exp(alt_ref[g]), upp_b)
        ecum = jnp.exp(cum)
        tail = jnp.exp(cum[Q - 1:Q, :] - cum) * dt
        boff = pl.multiple_of(SSM_DINNER + g * SSM_STATE, SSM_STATE)
        coff = pl.multiple_of(SSM_DINNER + SSM_GROUPS * SSM_STATE + g * SSM_STATE, SSM_STATE)
        roff = pl.multiple_of(g * gw, gw)
        bm = xa_ref[0, :, pl.ds(boff, SSM_STATE)]
        cm = xa_ref[0, :, pl.ds(coff, SSM_STATE)]
        cb = _dot_nt(cm, bm)
        hg = h_ref[pl.ds(roff, gw), :]
        y_state = _dot_nt(cm, hg)
        xg = xa_ref[0, :, pl.ds(roff, gw)]
        ys, xts = [], []
        for q in range(gw // LANES):
            xp = xg[:, q * LANES:(q + 1) * LANES]
            halves = []
            for e in range(2):
                hh = 2 * q + e
                seg = cum[:, hh:hh + 1] - cumt[hh:hh + 1, :]
                dec = jnp.where(tri, jnp.exp(jnp.where(tri, seg, 0.0)), 0.0)
                halves.append(_dot(cb * dec * dtt[hh:hh + 1, :], xp))
            pick = lambda t: jnp.where(left, t[:, 2 * q:2 * q + 1], t[:, 2 * q + 1:2 * q + 2])
            ys.append(jnp.where(left, halves[0], halves[1])
                      + y_state[:, q * LANES:(q + 1) * LANES] * pick(ecum))
            xts.append(xp * pick(tail))
        upd = _dot(jnp.concatenate(xts, axis=1).T, bm)
        for hh in range(per_grp):
            rows = slice(hh * P, (hh + 1) * P)
            h_ref[pl.ds(pl.multiple_of(roff + hh * P, P), P), :] = (
                hg[rows] * jnp.exp(cumt[hh:hh + 1, Q - 1:Q]) + upd[rows])
        yg = jnp.concatenate(ys, axis=1)
        yg = (yg + xg * dsk_ref[:, pl.ds(roff, gw)]) * _silu(z_ref[0, :, pl.ds(roff, gw)])
        ms = jnp.mean(yg * yg, axis=-1, keepdims=True)
        y_ref[0, :, pl.ds(roff, gw)] = (
            yg * lax.rsqrt(ms + SSM_NORM_EPS) * nw_ref[:, pl.ds(roff, gw)]).astype(y_ref.dtype)
        return 0

    lax.fori_loop(0, SSM_GROUPS, group, 0)

    @pl.when(c == pl.num_programs(1) - 1)
    def _():
        hf_ref[0] = h_ref[...]


def ssd_scan(xa, z, dt_raw, dt_bias, a_log, d_skip, norm_w, h0, t_valid):
    b, t, _ = xa.shape
    Q = SSM_CHUNK
    per_grp = SSM_HEADS // SSM_GROUPS
    assert t % Q == 0
    dt_g = dt_raw.reshape(b, t, SSM_GROUPS, per_grp).transpose(0, 2, 1, 3)
    dt_gt = dt_g.transpose(0, 1, 3, 2)
    grp = lambda p: p.reshape(SSM_GROUPS, 1, per_grp)
    grp_t = lambda p: p.reshape(SSM_GROUPS, per_grp, 1)
    full3 = lambda s: pl.BlockSpec(s, lambda bi, c: (0, 0, 0))
    lanes = lambda w: pl.BlockSpec((1, w), lambda bi, c: (0, 0))
    st_spec = pl.BlockSpec((1, SSM_DINNER, SSM_STATE), lambda bi, c: (bi, 0, 0))
    return pl.pallas_call(
        functools.partial(_ssd_body, t_valid=t_valid, t_total=t),
        grid=(b, t // Q),
        in_specs=[pl.BlockSpec((1, Q, SSM_CONV_DIM), lambda bi, c: (bi, c, 0)),
                  pl.BlockSpec((1, Q, SSM_DINNER), lambda bi, c: (bi, c, 0)),
                  pl.BlockSpec((1, SSM_GROUPS, Q, per_grp), lambda bi, c: (bi, 0, c, 0)),
                  pl.BlockSpec((1, SSM_GROUPS, per_grp, Q), lambda bi, c: (bi, 0, 0, c)),
                  full3((SSM_GROUPS, 1, per_grp)), full3((SSM_GROUPS, per_grp, 1)),
                  full3((SSM_GROUPS, 1, per_grp)), full3((SSM_GROUPS, per_grp, 1)),
                  lanes(SSM_DINNER), lanes(SSM_DINNER), st_spec],
        out_specs=[pl.BlockSpec((1, Q, SSM_DINNER), lambda bi, c: (bi, c, 0)), st_spec],
        out_shape=[jax.ShapeDtypeStruct((b, t, SSM_DINNER), BF16),
                   jax.ShapeDtypeStruct((b, SSM_DINNER, SSM_STATE), F32)],
        scratch_shapes=[pltpu.VMEM((SSM_DINNER, SSM_STATE), F32)],
        compiler_params=_params("parallel", "arbitrary"),
        name="ssd_scan",
    )(xa, z, dt_g, dt_gt, grp(dt_bias), grp_t(dt_bias), grp(a_log), grp_t(a_log),
      jnp.repeat(d_skip, SSM_HEADDIM).reshape(1, SSM_DINNER), norm_w.reshape(1, SSM_DINNER), h0)


def _pad_time(x3, mult):
    t = x3.shape[1]
    tp = -(-t // mult) * mult
    return x3 if tp == t else jnp.pad(x3, ((0, 0), (0, tp - t), (0, 0)))


def _rwkv_mixer(xf, b, t, ln, shift0, wkv0, W):
    d = D_MODEL
    xn = rmsnorm(xf, ln, out_dtype=F32).reshape(b, t, d)
    x_prev = jnp.concatenate([shift0[:, None].astype(F32), xn[:, :-1]], axis=1)
    xn_p, xp_p = _pad_time(xn, RWKV_CHUNK), _pad_time(x_prev, RWKV_CHUNK)
    tp = xn_p.shape[1]
    xr, xw, xk, xv, xa, xg = rwkv_mix(xn_p.reshape(-1, d), xp_p.reshape(-1, d), W['rwkv_mu'])
    r = linear(xr, W['rwkv_wr'])
    k = linear(xk, W['rwkv_wk'])
    v = linear(xv, W['rwkv_wv'])
    wl = linear(linear(xw, W['rwkv_w1'], act=jnp.tanh, out_dtype=BF16), W['rwkv_w2'])
    al = linear(linear(xa, W['rwkv_a1'], out_dtype=BF16), W['rwkv_a2'])
    g = linear(linear(xg, W['rwkv_g1'], act=jax.nn.sigmoid, out_dtype=BF16), W['rwkv_g2'])
    prm = jnp.stack([W['rwkv_w0'], W['rwkv_a0'], W['rwkv_kk'], W['rwkv_ka'], W['rwkv_rk'].reshape(d),
                     W['rwkv_gn_w'], W['rwkv_gn_b'], jnp.zeros((d,), F32)])
    n_pair = d // LANES
    s4 = wkv0.astype(F32).reshape(b, n_pair, 2, RWKV_HD, RWKV_HD)
    zero = jnp.zeros_like(s4[:, :, 0])
    s0 = jnp.concatenate([jnp.concatenate([s4[:, :, 0], zero], axis=-1),
                          jnp.concatenate([zero, s4[:, :, 1]], axis=-1)], axis=-2)
    to3 = lambda a: a.reshape(b, tp, d)
    y, sf = rwkv_scan(to3(r), to3(k), to3(v), to3(wl), to3(al), to3(g), prm, s0, t_valid=t)
    wkv = jnp.stack([sf[:, :, :RWKV_HD, :RWKV_HD], sf[:, :, RWKV_HD:, RWKV_HD:]], axis=2)
    y = y[:, :t].reshape(b * t, d)
    return linear(y, W['rwkv_wo'], res=xf), xn[:, -1], wkv.reshape(b, d // RWKV_HD, RWKV_HD, RWKV_HD)


def _sb_mixer(xf, b, t, ln, sb_past, W):
    d = D_MODEL
    xn = rmsnorm(xf, ln)
    qdt = BF16 if sb_past is None else F32
    q = linear(xn, W['sb_wqkv'], col0=0, n=d, gain=W['sb_gq'], hw=SB_HD, out_dtype=qdt)
    k = linear(xn, W['sb_wqkv'], col0=d, n=d, gain=W['sb_gk'], hw=SB_HD)
    v = linear(xn, W['sb_wqkv'], col0=2 * d, n=d)
    to3 = lambda a: a.reshape(b, t, d)
    if sb_past is None:
        o = sb_attention_prompt(to3(q), to3(k), to3(v), W['sb_bias'])
    else:
        pool_k, pool_v, table = sb_past
        o = sb_attention_decode(to3(q), to3(k), to3(v), pool_k, pool_v, table, W['sb_bias'])
    xf = linear(o.reshape(b * t, d), W['sb_wo'], res=xf)
    return xf, k.reshape(b, t, SB_HEADS, SB_HD), v.reshape(b, t, SB_HEADS, SB_HD)


def _dil_mixer(xf, b, t, pos0, ln, dil_bufs, W):
    n_grp = len(DIL_PATTERNS)
    gd = n_grp * DIL_HEADS * DIL_HD
    xn = rmsnorm(xf, ln)
    rows = t if t >= 8 else b * t
    rope = rope_tables(pos0 + (jnp.arange(rows, dtype=jnp.int32) % t))
    tm = min(rows, 512)
    q = linear(xn, W['dil_wqkv'], col0=0, n=gd, gain=W['dil_gq'], hw=DIL_HD, rope=rope, tm=tm)
    k = linear(xn, W['dil_wqkv'], col0=gd, n=gd, gain=W['dil_gk'], hw=DIL_HD, rope=rope, tm=tm)
    v = linear(xn, W['dil_wqkv'], col0=2 * gd, n=gd, tm=tm)
    to3 = lambda a: a.reshape(b, t, gd)
    if dil_bufs is None:
        o = dil_attention_prompt(to3(q), to3(k), to3(v))
    else:
        o = dil_attention_decode(to3(q), to3(k), to3(v), dil_bufs)
    xf = linear(o.reshape(b * t, DIL_HEADS * DIL_HD), W['dil_wo'], res=xf)
    k5 = k.reshape(b, t, n_grp, DIL_HEADS, DIL_HD)
    v5 = v.reshape(b, t, n_grp, DIL_HEADS, DIL_HD)
    states = []
    for g, (win, _) in enumerate(DIL_PATTERNS):
        keep = min(win, t)
        states += [k5[:, t - keep:, g], v5[:, t - keep:, g]]
    return xf, states


def _ssd_mixer(xf, b, t, ln, conv0, h0, W):
    d = D_MODEL
    xn = _pad_time(rmsnorm(xf, ln).reshape(b, t, d), SSM_CHUNK)
    tp = xn.shape[1]
    xn = xn.reshape(b * tp, d)
    z = linear(xn, W['ssm_win'], col0=0, n=SSM_DINNER)
    xbc = linear(xn, W['ssm_win'], col0=SSM_DINNER, n=SSM_CONV_DIM).reshape(b, tp, SSM_CONV_DIM)
    dt_raw = linear(xn, W['ssm_win'][:, SSM_DINNER + SSM_CONV_DIM:])
    prev8 = jnp.pad(conv0.astype(F32), ((0, 0), (8 - (SSM_CONV - 1), 0), (0, 0)))
    xa = causal_conv_silu(xbc, prev8, W['ssm_conv_w'], W['ssm_conv_b'])
    y, hf = ssd_scan(xa, z.reshape(b, tp, SSM_DINNER), dt_raw.reshape(b, tp, SSM_HEADS),
                     W['ssm_dt_bias'], W['ssm_a_log'], W['ssm_d'], W['ssm_norm_w'],
                     h0.astype(F32).reshape(b, SSM_DINNER, SSM_STATE), t_valid=t)
    xf = linear(y[:, :t].reshape(b * t, SSM_DINNER), W['ssm_wout'], res=xf)
    conv_state = jnp.concatenate([conv0.astype(F32), xbc[:, :t]], axis=1)[:, t:]
    return xf, conv_state, hf.reshape(b, SSM_HEADS, SSM_HEADDIM, SSM_STATE)


def _run_group(x, pos0, mem_k, mem_v, shift0, wkv0, sb_past, dil_bufs, conv0, h0, W):
    b, t, d = x.shape
    xf = x.reshape(b * t, d)
    st = {}
    for i in range(DEPTH):
        xf = ffn(xf, W['ln_ffn1'][i], W['ffn1_gate'], W['ffn1_up'], W['ffn1_down'], i)
        kind = i % 4
        ln = W['ln_mix'][i]
        if kind == 0:
            xf, st['rwkv_shift'], st['rwkv_wkv'] = _rwkv_mixer(xf, b, t, ln, shift0, wkv0, W)
        elif kind == 1:
            xf, st['sb_k'], st['sb_v'] = _sb_mixer(xf, b, t, ln, sb_past, W)
        elif kind == 2:
            xf, st['dil'] = _dil_mixer(xf, b, t, pos0, ln, dil_bufs, W)
        else:
            xf, st['ssm_conv'], st['ssm_h'] = _ssd_mixer(xf, b, t, ln, conv0, h0, W)
        xn = rmsnorm(xf, W['ln_mem'][i])
        q = linear(xn, W['mem_wq'], layer=i, gain=W['mem_gq'][i], hw=MEM_HD, out_dtype=BF16)
        q = _pad_time(q.reshape(b, t, d), 8)
        o = mem_attention(q, mem_k, mem_v, i)[:, :t].reshape(b * t, d)
        xf = linear(o, W['mem_wo'], layer=i, res=xf)
        xf = ffn(xf, W['ln_ffn2'][i], W['ffn2_gate'], W['ffn2_up'], W['ffn2_down'], i)
    return xf.reshape(b, t, d), st


def kernel(x_prompt, x_sample, state_rwkv_shift, state_rwkv_wkv, cache_sb_k, cache_sb_v, cache_dil0_k, cache_dil0_v, cache_dil1_k, cache_dil1_v, cache_dil2_k, cache_dil2_v, state_ssm_conv, state_ssm_h, cache_mem_k, cache_mem_v, page_table, mem_prompt, ln_ffn1, ffn1_gate, ffn1_up, ffn1_down, ln_mix, ln_mem, mem_wq, mem_gq, mem_wk, mem_gk, mem_wv, mem_wo, ln_ffn2, ffn2_gate, ffn2_up, ffn2_down, rwkv_mu, rwkv_wr, rwkv_wk, rwkv_wv, rwkv_wo, rwkv_w0, rwkv_w1, rwkv_w2, rwkv_a0, rwkv_a1, rwkv_a2, rwkv_g1, rwkv_g2, rwkv_kk, rwkv_ka, rwkv_rk, rwkv_gn_w, rwkv_gn_b, sb_wqkv, sb_gq, sb_gk, sb_bias, sb_wo, dil_wqkv, dil_gq, dil_gk, dil_wo, ssm_win, ssm_conv_w, ssm_conv_b, ssm_dt_bias, ssm_a_log, ssm_d, ssm_norm_w, ssm_wout):
    W = dict(locals())
    bp, _, d = x_prompt.shape
    bs = x_sample.shape[0]
    n_mem = mem_prompt.shape[1]
    past_len = page_table.shape[1] * cache_sb_k.shape[1]

    mem_rows = mem_prompt.reshape(bp * n_mem, d)
    p_mem_k = jnp.stack([linear(mem_rows, mem_wk, layer=i, gain=mem_gk[i], hw=MEM_HD)
                         for i in range(DEPTH)]).reshape(DEPTH, bp, n_mem, d)
    p_mem_v = jnp.stack([linear(mem_rows, mem_wv, layer=i)
                         for i in range(DEPTH)]).reshape(DEPTH, bp, n_mem, d)
    y_p, sp = _run_group(
        x_prompt, 0, p_mem_k, p_mem_v,
        jnp.zeros((bp, d), F32), jnp.zeros((bp, d // RWKV_HD, RWKV_HD, RWKV_HD), F32),
        None, None,
        jnp.zeros((bp, SSM_CONV - 1, SSM_CONV_DIM), F32),
        jnp.zeros((bp, SSM_HEADS, SSM_HEADDIM, SSM_STATE), F32), W)

    y_s, ss = _run_group(
        x_sample, past_len, cache_mem_k.reshape(DEPTH, bs, n_mem, d), cache_mem_v.reshape(DEPTH, bs, n_mem, d),
        state_rwkv_shift, state_rwkv_wkv, (cache_sb_k, cache_sb_v, page_table),
        (cache_dil0_k, cache_dil0_v, cache_dil1_k, cache_dil1_v, cache_dil2_k, cache_dil2_v),
        state_ssm_conv, state_ssm_h, W)

    mem_shape = (DEPTH, bp, n_mem, MEM_HEADS, MEM_HD)
    dil = []
    for g in range(len(DIL_PATTERNS)):
        dil += [sp['dil'][2 * g], sp['dil'][2 * g + 1], ss['dil'][2 * g], ss['dil'][2 * g + 1]]
    return (y_p, y_s,
            sp['rwkv_shift'], ss['rwkv_shift'], sp['rwkv_wkv'], ss['rwkv_wkv'],
            sp['sb_k'], sp['sb_v'], ss['sb_k'], ss['sb_v'],
            *dil,
            sp['ssm_conv'], ss['ssm_conv'], sp['ssm_h'], ss['ssm_h'],
            p_mem_k.reshape(mem_shape), p_mem_v.reshape(mem_shape))
```

```python
import functools
import math

import jax
import jax.numpy as jnp
from jax import lax
from jax.experimental import pallas as pl
from jax.experimental.pallas import tpu as pltpu

F32 = jnp.float32
BF16 = jnp.bfloat16

D_MODEL = 2048
DEPTH = 4
NORM_EPS = 1e-6
PAGE_SIZE = 128
MEM_HEADS = 4
MEM_HD = D_MODEL // MEM_HEADS
RWKV_HD = 64
GN_EPS = 64e-5
SB_HD = 128
SB_HEADS = D_MODEL // SB_HD
DIL_PATTERNS = ((128, 1), (512, 4), (2048, 16))
DIL_HD = 128
DIL_HEADS = 8
ROPE_THETA = 500000.0
ROPE_DIM = DIL_HD // 4
SSM_DINNER = 2 * D_MODEL
SSM_HEADDIM = 64
SSM_HEADS = SSM_DINNER // SSM_HEADDIM
SSM_STATE = 128
SSM_GROUPS = 8
SSM_CONV = 4
SSM_CONV_DIM = SSM_DINNER + 2 * SSM_GROUPS * SSM_STATE
SSM_CHUNK = 128
SSM_NORM_EPS = 1e-5

LANES = 128
V7X_VMEM_BYTES = 64 << 20
VMEM_LIMIT = V7X_VMEM_BYTES - (8 << 20)
DIL_BLOCKS_PER_ITER = 4
LINEAR_TN = 512
LINEAR_TN_WIDE = 1024
RWKV_CHUNK = 64
RWKV_PAIRS_PER_ITER = 4
RWKV_PAIRS_PER_DOT = 2
NEG_BIG = -1e30


def _params(*sem):
    return pltpu.CompilerParams(dimension_semantics=sem, vmem_limit_bytes=VMEM_LIMIT)


def _dot(a, b):
    return jnp.dot(a.astype(BF16), b.astype(BF16), preferred_element_type=F32)


def _dot_nt(a, b):
    return lax.dot_general(a.astype(BF16), b.astype(BF16), (((1,), (1,)), ((), ())),
                           preferred_element_type=F32)


def _split3(x):
    hi = x.astype(BF16)
    r1 = x - hi.astype(F32)
    mid = r1.astype(BF16)
    lo = (r1 - mid.astype(F32)).astype(BF16)
    return hi, mid, lo


def _dot_exact_rhs(x, m_bf16):
    hi, mid, lo = _split3(x)
    d = lambda p: jnp.dot(p, m_bf16, preferred_element_type=F32)
    return d(hi) + d(mid) + d(lo)


def _dot_exact_lhs(m_bf16, x):
    hi, mid, lo = _split3(x)
    d = lambda p: jnp.dot(m_bf16, p, preferred_element_type=F32)
    return d(hi) + d(mid) + d(lo)


def _iota(shape, dim):
    return lax.broadcasted_iota(jnp.int32, shape, dim)


def _softplus(x):
    return jnp.maximum(x, 0.0) + jnp.log1p(jnp.exp(-jnp.abs(x)))


def _log_sigmoid(x):
    return jnp.minimum(x, 0.0) - jnp.log(1.0 + jnp.exp(-jnp.abs(x)))


def _silu(x):
    return x * jax.nn.sigmoid(x)


def _linear_body(*refs, has_gain, has_rope, has_res, act, hw, res_scale):
    it = iter(refs)
    x_ref, w_ref = next(it), next(it)
    gain_ref = next(it) if has_gain else None
    cos_ref, sin_ref = (next(it), next(it)) if has_rope else (None, None)
    res_ref = next(it) if has_res else None
    o_ref, wbf_ref = next(it), next(it)

    @pl.when(pl.program_id(1) == 0)
    def _():
        wbf_ref[...] = w_ref[...].astype(BF16)

    acc = jnp.dot(x_ref[...].astype(BF16), wbf_ref[...], preferred_element_type=F32)
    if act is not None:
        acc = act(acc)
    if has_res:
        acc = res_ref[...] + res_scale * acc
    if has_gain:
        for s in range(acc.shape[1] // hw):
            y = acc[:, s * hw:(s + 1) * hw]
            ms = jnp.mean(y * y, axis=-1, keepdims=True)
            y = y * lax.rsqrt(ms + NORM_EPS) * gain_ref[...]
            if has_rope:
                lane = _iota(y.shape, 1)
                half = ROPE_DIM // 2
                rot = jnp.where(lane < half, pltpu.roll(y, hw - half, 1), pltpu.roll(y, half, 1))
                y = y * cos_ref[...] + rot * sin_ref[...]
            o_ref[:, s * hw:(s + 1) * hw] = y.astype(o_ref.dtype)
    else:
        o_ref[...] = acc.astype(o_ref.dtype)


def linear(x, w, *, col0=0, n=None, out_dtype=F32, gain=None, hw=None, rope=None, res=None,
           res_scale=1.0, act=None, layer=None, tm=512, tn=None, name="linear"):
    m, k = x.shape
    n = w.shape[-1] - col0 if n is None else n
    if tn is None:
        tn = LINEAR_TN_WIDE if (n % LINEAR_TN_WIDE == 0 and col0 % LINEAR_TN_WIDE == 0
                                and k * LINEAR_TN_WIDE * 10 <= VMEM_LIMIT // 3 * 2) else LINEAR_TN
    tm, tn = min(tm, m), min(tn, n)
    assert m % tm == 0 and n % tn == 0 and col0 % tn == 0 and w.shape[-2] == k
    cb = col0 // tn
    if layer is None:
        w_spec = pl.BlockSpec((k, tn), lambda j, i: (0, j + cb))
    else:
        w_spec = pl.BlockSpec((None, k, tn), lambda j, i: (layer, 0, j + cb))
    in_specs = [pl.BlockSpec((tm, k), lambda j, i: (i, 0)), w_spec]
    args = [x, w]
    if gain is not None:
        assert tn % hw == 0
        in_specs.append(pl.BlockSpec((1, hw), lambda j, i: (0, 0)))
        args.append(gain.reshape(1, hw).astype(F32))
    if rope is not None:
        cos, sin = rope
        nt = cos.shape[0] // tm
        assert hw == LANES and cos.shape[0] % tm == 0
        in_specs += [pl.BlockSpec((tm, LANES), lambda j, i: (i % nt, 0))] * 2
        args += [cos, sin]
    if res is not None:
        in_specs.append(pl.BlockSpec((tm, tn), lambda j, i: (i, j)))
        args.append(res)
    body = functools.partial(_linear_body, has_gain=gain is not None, has_rope=rope is not None,
                             has_res=res is not None, act=act, hw=hw, res_scale=res_scale)
    return pl.pallas_call(
        body,
        grid=(n // tn, m // tm),
        in_specs=in_specs,
        out_specs=pl.BlockSpec((tm, tn), lambda j, i: (i, j)),
        out_shape=jax.ShapeDtypeStruct((m, n), out_dtype),
        scratch_shapes=[pltpu.VMEM((k, tn), BF16)],
        compiler_params=_params("parallel", "arbitrary"),
        name=name,
    )(*args)


def _rmsnorm_body(x_ref, g_ref, o_ref):
    x = x_ref[...]
    ms = jnp.mean(x * x, axis=-1, keepdims=True)
    o_ref[...] = (x * lax.rsqrt(ms + NORM_EPS) * g_ref[...]).astype(o_ref.dtype)


def rmsnorm(x, g, out_dtype=BF16, tm=512):
    m, d = x.shape
    tm = min(tm, m)
    assert m % tm == 0
    return pl.pallas_call(
        _rmsnorm_body,
        grid=(m // tm,),
        in_specs=[pl.BlockSpec((tm, d), lambda i: (i, 0)), pl.BlockSpec((1, d), lambda i: (0, 0))],
        out_specs=pl.BlockSpec((tm, d), lambda i: (i, 0)),
        out_shape=jax.ShapeDtypeStruct((m, d), out_dtype),
        compiler_params=_params("parallel"),
        name="rmsnorm",
    )(x, g.reshape(1, d))


def _ffn_body(x_ref, g_ref, wg_ref, wu_ref, wd_ref, o_ref, xn_ref):
    @pl.when(pl.program_id(1) == 0)
    def _():
        x = x_ref[...]
        ms = jnp.mean(x * x, axis=-1, keepdims=True)
        xn_ref[...] = (x * lax.rsqrt(ms + NORM_EPS) * g_ref[...]).astype(BF16)
        o_ref[...] = x

    xn = xn_ref[...]
    gate = jnp.dot(xn, wg_ref[...].astype(BF16), preferred_element_type=F32)
    up = jnp.dot(xn, wu_ref[...].astype(BF16), preferred_element_type=F32)
    h = (0.5 * _silu(gate) * up).astype(BF16)
    o_ref[...] += jnp.dot(h, wd_ref[...].astype(BF16), preferred_element_type=F32)


def ffn(x, g, w_gate, w_up, w_down, layer, tm=1024, tf=256):
    m, d = x.shape
    f = w_gate.shape[-1]
    tm = min(tm, m)
    assert m % tm == 0 and f % tf == 0
    return pl.pallas_call(
        _ffn_body,
        grid=(m // tm, f // tf),
        in_specs=[pl.BlockSpec((tm, d), lambda i, j: (i, 0)),
                  pl.BlockSpec((1, d), lambda i, j: (0, 0)),
                  pl.BlockSpec((None, d, tf), lambda i, j: (layer, 0, j)),
                  pl.BlockSpec((None, d, tf), lambda i, j: (layer, 0, j)),
                  pl.BlockSpec((None, tf, d), lambda i, j: (layer, j, 0))],
        out_specs=pl.BlockSpec((tm, d), lambda i, j: (i, 0)),
        out_shape=jax.ShapeDtypeStruct((m, d), F32),
        scratch_shapes=[pltpu.VMEM((tm, d), BF16)],
        compiler_params=_params("parallel", "arbitrary"),
        name="ffn",
    )(x, g.reshape(1, d), w_gate, w_up, w_down)


def _memattn_body(q_ref, k_ref, v_ref, o_ref, *, scale):
    s = _dot_nt(q_ref[0], k_ref[0, 0]) * scale
    m = jnp.max(s, axis=-1, keepdims=True)
    p = jnp.exp(s - m)
    l = jnp.sum(p, axis=-1, keepdims=True)
    o_ref[0] = (_dot(p, v_ref[0, 0]) / l).astype(o_ref.dtype)


def mem_attention(q, mem_k, mem_v, layer, tq=512):
    b, t, d = q.shape
    n_mem = mem_k.shape[2]
    tq = min(tq, t)
    assert t % tq == 0
    kv_spec = pl.BlockSpec((1, 1, n_mem, MEM_HD), lambda bi, ti, h: (layer, bi, 0, h))
    return pl.pallas_call(
        functools.partial(_memattn_body, scale=MEM_HD ** -0.5),
        grid=(b, t // tq, MEM_HEADS),
        in_specs=[pl.BlockSpec((1, tq, MEM_HD), lambda bi, ti, h: (bi, ti, h)), kv_spec, kv_spec],
        out_specs=pl.BlockSpec((1, tq, MEM_HD), lambda bi, ti, h: (bi, ti, h)),
        out_shape=jax.ShapeDtypeStruct((b, t, d), BF16),
        compiler_params=_params("parallel", "parallel", "parallel"),
        name="mem_attention",
    )(q, mem_k, mem_v)


def rope_tables(pos):
    half = ROPE_DIM // 2
    inv_freq = ROPE_THETA ** (-jnp.arange(half, dtype=F32) / half)
    ang = pos.astype(F32)[:, None] * inv_freq[None, :]
    cos, sin = jnp.cos(ang), jnp.sin(ang)
    rest = DIL_HD - ROPE_DIM
    n = pos.shape[0]
    cos_t = jnp.concatenate([cos, cos, jnp.ones((n, rest), F32)], axis=1)
    sin_t = jnp.concatenate([-sin, sin, jnp.zeros((n, rest), F32)], axis=1)
    return cos_t, sin_t


def _sb_tiles(qs, ks, vs, biases, causal, carries, accs, upper):
    each = lambda f, *cols: [f(*args) for args in zip(*cols)]
    mask = (lambda t: t) if causal is None else (lambda t: jnp.where(causal, t, 0.0))
    z = each(lambda q, k, b: _dot_nt(q, k) * (SB_HD ** -0.5) + b, qs, ks, biases)
    ls = each(_log_sigmoid, z)
    log_keep = each(lambda l, t: mask(l - t), ls, z)
    hi = each(lambda t: t.astype(BF16), log_keep)
    lo = each(lambda t, h: (t - h.astype(F32)).astype(BF16), log_keep, hi)
    local = each(lambda h, l: jnp.dot(jnp.concatenate([h, l], axis=0), upper, preferred_element_type=F32),
                 hi, lo)
    rows = qs[0].shape[0]
    att = each(lambda l, s, c: mask(jnp.exp(l + s[:rows] + s[rows:] + c)), ls, local, carries)
    accs = each(lambda a, p, v: a + _dot(p, v), accs, att, vs)
    carries = each(lambda c, t: c + jnp.sum(t, axis=-1, keepdims=True), carries, log_keep)
    return carries, accs


def _sb_tile(q, k, v, bias, causal, carry, acc, upper):
    carries, accs = _sb_tiles([q], [k], [v], [bias], causal, [carry], [acc], upper)
    return carries[0], accs[0]


def _upper_ones(n):
    return jnp.where(_iota((n, n), 0) > _iota((n, n), 1), 1.0, 0.0).astype(BF16)


def _sb_prompt_body(bias_ref, q_ref, k_ref, v_ref, o_ref, *, tq, heads):
    hb, qi = pl.program_id(1), pl.program_id(2)
    upper = _upper_ones(tq)
    lanes = [slice(u * SB_HD, (u + 1) * SB_HD) for u in range(heads)]
    qs = [q_ref[0, :, sl] for sl in lanes]
    biases = [bias_ref[hb * heads + u] for u in range(heads)]

    def tiles(j, state, causal):
        start = pl.multiple_of(j * tq, tq)
        ks = [k_ref[0, pl.ds(start, tq), sl] for sl in lanes]
        vs = [v_ref[0, pl.ds(start, tq), sl] for sl in lanes]
        return _sb_tiles(qs, ks, vs, biases, causal, state[0], state[1], upper)

    init = ([jnp.zeros((tq, 1), F32) for _ in lanes], [jnp.zeros((tq, SB_HD), F32) for _ in lanes])
    state = tiles(qi, init, _iota((tq, tq), 1) < _iota((tq, tq), 0))
    _, accs = lax.fori_loop(0, qi, lambda jj, st: tiles(qi - 1 - jj, st, None), state)
    for sl, acc in zip(lanes, accs):
        o_ref[0, :, sl] = acc.astype(o_ref.dtype)


def sb_attention_prompt(q, k, v, bias, tq=256, heads=2):
    b, t, d = q.shape
    tq = min(tq, t)
    w = heads * SB_HD
    assert t % tq == 0 and d % w == 0
    kv_spec = pl.BlockSpec((1, t, w), lambda bi, h, qi: (bi, 0, h))
    io_spec = pl.BlockSpec((1, tq, w), lambda bi, h, qi: (bi, qi, h))
    return pl.pallas_call(
        functools.partial(_sb_prompt_body, tq=tq, heads=heads),
        grid=(b, d // w, t // tq),
        in_specs=[pl.BlockSpec(memory_space=pltpu.SMEM), io_spec, kv_spec, kv_spec],
        out_specs=io_spec,
        out_shape=jax.ShapeDtypeStruct((b, t, d), BF16),
        compiler_params=_params("parallel", "parallel", "arbitrary"),
        name="sb_attention_prompt",
    )(bias, q, k, v)


def _sb_decode_body(table_ref, bias_ref, q_ref, kn_ref, vn_ref, *rest, n_pages, per_step):
    del table_ref
    page_refs = rest[:2 * per_step]
    o_ref, acc_ref, carry_ref = rest[2 * per_step:]
    j = pl.program_id(1)
    nh, cols = SB_HEADS, PAGE_SIZE * SB_HEADS
    n_sub = cols // LANES
    bias = bias_ref[...]
    past_len = n_pages * PAGE_SIZE
    scale = SB_HD ** -0.5
    q = q_ref[0]

    @pl.when(j == 0)
    def _():
        z = jnp.sum(q * kn_ref[0], axis=-1, keepdims=True) * scale + bias
        causal = jnp.full((nh, 1), past_len, jnp.int32) < past_len
        ls = _log_sigmoid(z)
        carry_ref[...] = jnp.where(causal, ls - z, 0.0)
        acc_ref[...] = jnp.where(causal, jnp.exp(ls), 0.0) * vn_ref[0]

    each = lambda f, *c: [f(*args) for args in zip(*c)]
    col = _iota((nh, cols), 1)
    own = (col & (nh - 1)) == _iota((nh, cols), 0)
    upper = _upper_ones(LANES)
    pages = [n_pages - 1 - (j * per_step + u) for u in range(per_step)]
    masks = [own & ((p * PAGE_SIZE + col // nh) < past_len) for p in pages]
    ks = [page_refs[2 * u][...] for u in range(per_step)]
    vs = [page_refs[2 * u + 1][...] for u in range(per_step)]
    z = each(lambda k: _dot_nt(q, k) * scale + bias, ks)
    ls = each(_log_sigmoid, z)
    log_keep = each(lambda l, t, m: jnp.where(m, l - t, 0.0), ls, z, masks)

    def local_suffix(t):
        x = jnp.concatenate([t[:, c * LANES:(c + 1) * LANES] for c in range(n_sub)], axis=0)
        hi = x.astype(BF16)
        lo = (x - hi.astype(F32)).astype(BF16)
        y = jnp.dot(jnp.concatenate([hi, lo], axis=0), upper, preferred_element_type=F32)
        y = y[:n_sub * nh] + y[n_sub * nh:]
        return [y[c * nh:(c + 1) * nh] for c in range(n_sub)]

    local = each(local_suffix, log_keep)
    carry = carry_ref[...]
    between = []
    for t, loc in zip(log_keep, local):
        pieces = [None] * n_sub
        for c in reversed(range(n_sub)):
            pieces[c] = loc[c] + carry
            carry = carry + jnp.sum(t[:, c * LANES:(c + 1) * LANES], axis=-1, keepdims=True)
        between.append(jnp.concatenate(pieces, axis=1))
    carry_ref[...] = carry
    att = each(lambda l, s, m: jnp.where(m, jnp.exp(l + s), 0.0), ls, between, masks)
    acc_ref[...] += sum(each(_dot, att, vs))

    @pl.when(j == pl.num_programs(1) - 1)
    def _():
        o_ref[0] = acc_ref[...]


def sb_attention_decode(q, k_new, v_new, pool_k, pool_v, table, bias, per_step=2):
    b = q.shape[0]
    n_pages = table.shape[1]
    assert n_pages % per_step == 0 and pool_k.shape[1:] == (PAGE_SIZE, SB_HEADS, SB_HD)
    assert SB_HEADS & (SB_HEADS - 1) == 0 and (PAGE_SIZE * SB_HEADS) % LANES == 0
    page_rows = PAGE_SIZE * SB_HEADS
    pool_k = pool_k.reshape(-1, SB_HD)
    pool_v = pool_v.reshape(-1, SB_HD)
    row_spec = pl.BlockSpec((1, SB_HEADS, SB_HD), lambda bi, j, tab: (bi, 0, 0))

    def page_spec(u):
        return pl.BlockSpec((page_rows, SB_HD),
                            lambda bi, j, tab: (tab[bi, n_pages - 1 - (j * per_step + u)], 0))

    pages = [page_spec(u) for u in range(per_step) for _ in range(2)]
    grid_spec = pltpu.PrefetchScalarGridSpec(
        num_scalar_prefetch=1,
        grid=(b, n_pages // per_step),
        in_specs=[pl.BlockSpec((SB_HEADS, 1), lambda bi, j, tab: (0, 0)),
                  row_spec, row_spec, row_spec] + pages,
        out_specs=row_spec,
        scratch_shapes=[pltpu.VMEM((SB_HEADS, SB_HD), F32), pltpu.VMEM((SB_HEADS, 1), F32)],
    )
    return pl.pallas_call(
        functools.partial(_sb_decode_body, n_pages=n_pages, per_step=per_step),
        grid_spec=grid_spec,
        out_shape=jax.ShapeDtypeStruct((b, SB_HEADS, SB_HD), F32),
        compiler_params=_params("parallel", "arbitrary"),
        name="sb_attention_decode",
    )(table, bias.reshape(SB_HEADS, 1), q, k_new, v_new, *([pool_k, pool_v] * per_step))


def _dil_prompt_body(*refs, t, blk):
    n_grp = len(DIL_PATTERNS)
    q_refs, k_refs, v_refs = refs[:n_grp], refs[n_grp:2 * n_grp], refs[2 * n_grp:3 * n_grp]
    o_ref, m_s, l_s, acc_s = refs[3 * n_grp:3 * n_grp + 4]
    scale = DIL_HD ** -0.5
    order = sorted(range(n_grp), key=lambda g: -DIL_PATTERNS[g][1])
    for n, g in enumerate(order):
        win, dil = DIL_PATTERNS[g]
        band, cls_len = win // dil, t // dil
        n_blk = cls_len // blk
        n_keys = min(2 * blk, cls_len)
        first, last = n == 0, n == n_grp - 1
        assert band == blk and (not last or dil == 1) and n_blk * blk == cls_len
        q_ref, k_ref, v_ref = q_refs[g], k_refs[g], v_refs[g]
        rel = _iota((blk, n_keys), 0) - _iota((blk, n_keys), 1)

        def rows(start, size, dil=dil):
            return pl.ds(start, size, stride=dil) if dil > 1 else pl.ds(start, size)

        def blocks(i, _, q_ref=q_ref, k_ref=k_ref, v_ref=v_ref, dil=dil, n_blk=n_blk, n_keys=n_keys,
                   rel=rel, rows=rows, first=first, last=last, band=band):
            each = lambda f, *cols: [f(*args) for args in zip(*cols)]
            idxs = [i * DIL_BLOCKS_PER_ITER + u for u in range(DIL_BLOCKS_PER_ITER)]
            lbs = [idx % n_blk for idx in idxs]
            k_cls = [jnp.maximum(lb - 1, 0) * blk for lb in lbs]
            if dil == 1:
                q_tok = [pl.multiple_of(lb * blk, blk) for lb in lbs]
                k_tok = [pl.multiple_of(kc, blk) for kc in k_cls]
            else:
                q_tok = [idx // n_blk + lb * blk * dil for idx, lb in zip(idxs, lbs)]
                k_tok = [idx // n_blk + kc * dil for idx, kc in zip(idxs, k_cls)]
            sel = [rows(t, blk) for t in q_tok]
            q = [q_ref[0, s, :] for s in sel]
            k = [k_ref[0, rows(t, n_keys), :] for t in k_tok]
            v = [v_ref[0, rows(t, n_keys), :] for t in k_tok]
            old = None if first else [(m_s[s, :], l_s[s, :], acc_s[s, :]) for s in sel]

            def scores(qq, kk, lb, kc):
                dist = rel + (lb * blk - kc)
                valid = jnp.where(dist >= 0, dist, band + 1) <= band
                return jnp.where(valid, _dot_nt(qq, kk) * scale, NEG_BIG)

            s = each(scores, q, k, lbs, k_cls)
            m = each(lambda t: jnp.max(t, axis=-1, keepdims=True), s)
            p = each(lambda t, mm: jnp.exp(t - mm), s, m)
            l = each(lambda t: jnp.sum(t, axis=-1, keepdims=True), p)
            acc = each(_dot, p, v)
            if not first:
                m_new = each(lambda o, mm: jnp.maximum(o[0], mm), old, m)
                e0 = each(lambda o, mn: jnp.exp(o[0] - mn), old, m_new)
                e1 = each(lambda mm, mn: jnp.exp(mm - mn), m, m_new)
                l = each(lambda o, a, b, t: a * o[1] + b * t, old, e0, e1, l)
                acc = each(lambda o, a, b, t: a * o[2] + b * t, old, e0, e1, acc)
                m = m_new
            for u, sl in enumerate(sel):
                if last:
                    o_ref[0, sl, :] = (acc[u] / l[u]).astype(o_ref.dtype)
                else:
                    m_s[sl, :], l_s[sl, :], acc_s[sl, :] = m[u], l[u], acc[u]
            return 0

        assert (dil * n_blk) % DIL_BLOCKS_PER_ITER == 0
        lax.fori_loop(0, dil * n_blk // DIL_BLOCKS_PER_ITER, blocks, 0)


def dil_attention_prompt(q, k, v, blk=128):
    b, t, _ = q.shape
    n_grp = len(DIL_PATTERNS)
    assert t % blk == 0

    def spec(g):
        return pl.BlockSpec((1, t, DIL_HD), lambda bi, h: (bi, 0, g * DIL_HEADS + h))

    return pl.pallas_call(
        functools.partial(_dil_prompt_body, t=t, blk=blk),
        grid=(b, DIL_HEADS),
        in_specs=[spec(g) for g in range(n_grp)] * 3,
        out_specs=pl.BlockSpec((1, t, DIL_HD), lambda bi, h: (bi, 0, h)),
        out_shape=jax.ShapeDtypeStruct((b, t, DIL_HEADS * DIL_HD), BF16),
        scratch_shapes=[pltpu.VMEM((t, 1), F32), pltpu.VMEM((t, 1), F32), pltpu.VMEM((t, DIL_HD), F32)],
        compiler_params=_params("parallel", "parallel"),
        name="dil_attention_prompt",
    )(*([q] * n_grp + [k] * n_grp + [v] * n_grp))


def _dil_decode_body(*refs):
    n_grp = len(DIL_PATTERNS)
    q_ref, kn_ref, vn_ref = refs[:3]
    kb_refs, vb_refs = refs[3:3 + n_grp], refs[3 + n_grp:3 + 2 * n_grp]
    o_ref = refs[3 + 2 * n_grp]
    scale = DIL_HD ** -0.5
    nh = DIL_HEADS
    scores, news, masks = [], [], []
    for g in range(n_grp):
        q = q_ref[0, g]
        band = kb_refs[g].shape[1]
        k2 = kb_refs[g][0].reshape(band * nh, DIL_HD)
        own = (_iota((nh, band * nh), 1) & (nh - 1)) == _iota((nh, band * nh), 0)
        masks.append(own)
        scores.append(jnp.where(own, _dot_nt(q, k2) * scale, NEG_BIG))
        news.append(jnp.sum(q * kn_ref[0, g], axis=-1, keepdims=True) * scale)
    m = functools.reduce(jnp.maximum, [jnp.max(s, axis=-1, keepdims=True) for s in scores] + news)
    l = jnp.zeros((nh, 1), F32)
    acc = jnp.zeros((nh, DIL_HD), F32)
    for g in range(n_grp):
        p = jnp.where(masks[g], jnp.exp(scores[g] - m), 0.0)
        pn = jnp.exp(news[g] - m)
        v2 = vb_refs[g][0].reshape(p.shape[1], DIL_HD)
        l = l + jnp.sum(p, axis=-1, keepdims=True) + pn
        acc = acc + _dot(p, v2) + pn * vn_ref[0, g]
    o_ref[0] = acc / l


def dil_attention_decode(q, k_new, v_new, bufs):
    b = q.shape[0]
    n_grp = len(DIL_PATTERNS)
    assert DIL_HEADS & (DIL_HEADS - 1) == 0
    row_spec = pl.BlockSpec((1, n_grp, DIL_HEADS, DIL_HD), lambda bi: (bi, 0, 0, 0))
    views, specs = [], []
    for which in range(2):
        for g, (win, dil) in enumerate(DIL_PATTERNS):
            buf = bufs[2 * g + which]
            assert buf.shape[1] == win and win % dil == 0
            views.append(buf.reshape(b, win // dil, dil * DIL_HEADS, DIL_HD))
            specs.append(pl.BlockSpec((1, win // dil, DIL_HEADS, DIL_HD), lambda bi: (bi, 0, 0, 0)))
    return pl.pallas_call(
        _dil_decode_body,
        grid=(b,),
        in_specs=[row_spec] * 3 + specs,
        out_specs=pl.BlockSpec((1, DIL_HEADS, DIL_HD), lambda bi: (bi, 0, 0)),
        out_shape=jax.ShapeDtypeStruct((b, DIL_HEADS, DIL_HD), F32),
        compiler_params=_params("parallel"),
        name="dil_attention_decode",
    )(q, k_new, v_new, *views)


def _rwkv_mix_body(x_ref, xp_ref, mu_ref, *o_refs):
    x = x_ref[...]
    xx = xp_ref[...] - x
    for i, o_ref in enumerate(o_refs):
        o_ref[...] = (x + xx * mu_ref[i:i + 1, :]).astype(o_ref.dtype)


def rwkv_mix(xn, x_prev, mu, tm=512):
    m, d = xn.shape
    tm = min(tm, m)
    n_mix = mu.shape[0]
    row = pl.BlockSpec((tm, d), lambda i: (i, 0))
    return pl.pallas_call(
        _rwkv_mix_body,
        grid=(m // tm,),
        in_specs=[row, row, pl.BlockSpec((n_mix, d), lambda i: (0, 0))],
        out_specs=[row] * n_mix,
        out_shape=[jax.ShapeDtypeStruct((m, d), BF16)] * n_mix,
        compiler_params=_params("parallel"),
        name="rwkv_mix",
    )(xn, x_prev, mu)


def _rwkv_scan_body(r_ref, k_ref, v_ref, wl_ref, al_ref, g_ref, prm_ref, s0_ref, y_ref, sf_ref,
                    st_ref, *, t_valid, t_total):
    c = pl.program_id(1)
    C, hd = RWKV_CHUNK, RWKV_HD
    n2 = 2 * C
    grp = RWKV_PAIRS_PER_DOT
    rows = grp * n2

    @pl.when(c == 0)
    def _():
        st_ref[...] = s0_ref[0]

    left = _iota((C, LANES), 1) < hd
    ri, ci = _iota((rows, rows), 0), _iota((rows, rows), 1)
    strict = (ci & (C - 1)) < (ri & (C - 1))
    incl = (ci & (C - 1)) <= (ri & (C - 1))
    tri = jnp.where(_iota((C, C), 1) <= _iota((C, C), 0), 1.0, 0.0).astype(BF16)
    same_head = (_iota((LANES, LANES), 0) // hd) == (_iota((LANES, LANES), 1) // hd)
    seg_ones = jnp.where(same_head, 1.0, 0.0).astype(BF16)
    masked = t_valid < t_total
    rows_valid = (c * C + _iota((C, LANES), 0)) < t_valid
    zero_slab = jnp.zeros((n2, LANES), F32)

    def seg_sum(x):
        hi = x.astype(BF16)
        lo = (x - hi.astype(F32)).astype(BF16)
        y = jnp.dot(jnp.concatenate([hi, lo], axis=0), seg_ones, preferred_element_type=F32)
        return y[:x.shape[0]] + y[x.shape[0]:]

    def cumsum_steps(x):
        w = x.shape[1]
        y = jnp.dot(tri, jnp.concatenate(_split3(x), axis=1), preferred_element_type=F32)
        return y[:, :w] + y[:, w:2 * w] + y[:, 2 * w:]

    def stack(x):
        return jnp.concatenate([jnp.where(left, x, 0.0), jnp.where(left, 0.0, x)], axis=0)

    def spread(slabs):
        return jnp.concatenate(
            [jnp.concatenate([s if q == j else zero_slab for j in range(grp)], axis=1)
             for q, s in enumerate(slabs)], axis=0)

    def own_lanes(x):
        return jnp.concatenate([x[q * n2:(q + 1) * n2, q * LANES:(q + 1) * LANES] for q in range(grp)],
                               axis=0)

    def core(groups):
        each = lambda f, *cols: [f(*args) for args in zip(*cols)]
        dss, stss = [g[0] for g in groups], [g[1] for g in groups]
        ar = each(lambda ds: jnp.concatenate([spread([stack(d['a_t']) for d in ds]),
                                              spread([stack(d['r_t']) for d in ds])],
                                             axis=0).astype(BF16), dss)
        bk = each(lambda ds: jnp.concatenate([spread([stack(d['b_t']) for d in ds]),
                                              spread([stack(d['k_t']) for d in ds])],
                                             axis=0).astype(BF16), dss)
        gram = each(_dot_nt, ar, bk)
        a_s = each(lambda a, sts: _dot_nt(a, spread(sts)), ar, stss)
        vv = each(lambda ds: jnp.concatenate([d['v'] for d in ds for _ in range(2)], axis=0), dss)
        y = each(lambda g, v, s: _dot(jnp.where(strict, g[:rows, rows:], 0.0), v) + own_lanes(s[:rows]),
                 gram, vv, a_s)
        pw = each(lambda g: jnp.where(strict, g[:rows, :rows], 0.0), gram)
        n = 1
        while n < C:
            y = each(lambda p, t: t + _dot(p, t), pw, y)
            n *= 2
            if n < C:
                pw = each(lambda p: _dot(p, p), pw)
        o2 = each(lambda g, t, v, s: _dot(
            jnp.concatenate([jnp.where(incl, g[rows:, :rows], 0.0),
                             jnp.where(incl, g[rows:, rows:], 0.0)], axis=1),
            jnp.concatenate([t, v], axis=0)) + own_lanes(s[rows:]), gram, y, vv, a_s)
        pick = lambda t, q: jnp.where(left, t[q * n2:q * n2 + C], t[q * n2 + C:(q + 1) * n2])
        upd = each(lambda ds, t: _dot(
            jnp.concatenate([jnp.concatenate([pick(t, q), d['v']], axis=0)
                             for q, d in enumerate(ds)], axis=1).T,
            jnp.concatenate([jnp.concatenate([d['b_end'], d['k_end']], axis=0) for d in ds], axis=1)),
            dss, y)
        os, s_new = [], []
        for ds, sts, o, up in zip(dss, stss, o2, upd):
            os += [pick(o, q) for q in range(grp)]
            s_new += [jnp.where(same_head, st * d['decay_end']
                                + up[q * LANES:(q + 1) * LANES, q * LANES:(q + 1) * LANES], 0.0)
                      for q, (d, st) in enumerate(zip(ds, sts))]
        return os, s_new

    def pairs(i, _):
        ps = [i * RWKV_PAIRS_PER_ITER + u for u in range(RWKV_PAIRS_PER_ITER)]
        offs = [pl.multiple_of(p * LANES, LANES) for p in ps]
        sts = [st_ref[p] for p in ps]
        ds = []
        for off in offs:
            r, k, v, wl, al, g = [ref[0, :, pl.ds(off, LANES)]
                                  for ref in (r_ref, k_ref, v_ref, wl_ref, al_ref, g_ref)]
            prm = prm_ref[:, pl.ds(off, LANES)]
            w0, a0, k_k, k_a, r_k, gn_w, gn_b = [prm[j:j + 1] for j in range(7)]
            a = jax.nn.sigmoid(a0 + al)
            ds.append(dict(r=r, v=v, g=g, a=a, gn_w=gn_w, gn_b=gn_b, r_k=r_k, kkr=k * k_k,
                           lw=-jnp.exp(-_softplus(-(w0 + wl)) - 0.5),
                           kmod=k * (1.0 + (a - 1.0) * k_a)))
        sums = seg_sum(jnp.concatenate([d['kkr'] * d['kkr'] for d in ds]
                                       + [d['r'] * d['kmod'] * d['r_k'] for d in ds], axis=0))
        n_p = len(ds)
        for u, d in enumerate(ds):
            d['kk'] = d['kkr'] * lax.rsqrt(jnp.maximum(sums[u * C:(u + 1) * C], 1e-24))
            d['bonus'] = sums[(n_p + u) * C:(n_p + u + 1) * C] * d['v']
            if masked:
                for name in ('lw', 'kk', 'kmod', 'v'):
                    d[name] = jnp.where(rows_valid, d[name], 0.0)
        lc_all = cumsum_steps(jnp.concatenate([d['lw'] for d in ds], axis=1))
        for u, d in enumerate(ds):
            lc = lc_all[:, u * LANES:(u + 1) * LANES]
            lend = lc[C - 1:C, :]
            kka = d['kk'] * d['a']
            e_inv, e_end = jnp.exp(-lc), jnp.exp(lend - lc)
            d.update(a_t=-d['kk'] * jnp.exp(lc - d['lw']), r_t=d['r'] * jnp.exp(lc),
                     b_t=kka * e_inv, k_t=d['kmod'] * e_inv, b_end=kka * e_end,
                     k_end=d['kmod'] * e_end, decay_end=jnp.exp(lend))
        os, s_news = core([(ds[q:q + grp], sts[q:q + grp]) for q in range(0, n_p, grp)])
        stats = seg_sum(jnp.concatenate(os + [o * o for o in os], axis=0))
        for u, (p, off, d, o) in enumerate(zip(ps, offs, ds, os)):
            mean = stats[u * C:(u + 1) * C] * (1.0 / hd)
            var = stats[(n_p + u) * C:(n_p + u + 1) * C] * (1.0 / hd) - mean * mean
            on = (o - mean) * lax.rsqrt(var + GN_EPS) * d['gn_w'] + d['gn_b']
            st_ref[p] = s_news[u]
            y_ref[0, :, pl.ds(off, LANES)] = ((on + d['bonus']) * d['g']).astype(y_ref.dtype)
        return 0

    lax.fori_loop(0, D_MODEL // LANES // RWKV_PAIRS_PER_ITER, pairs, 0)

    @pl.when(c == pl.num_programs(1) - 1)
    def _():
        sf_ref[0] = st_ref[...]


def rwkv_scan(r, k, v, wl, al, g, prm, s0, t_valid):
    b, t, d = r.shape
    C = RWKV_CHUNK
    assert t % C == 0 and d == D_MODEL
    seq = pl.BlockSpec((1, C, d), lambda bi, c: (bi, c, 0))
    st_spec = pl.BlockSpec((1, d // LANES, LANES, LANES), lambda bi, c: (bi, 0, 0, 0))
    return pl.pallas_call(
        functools.partial(_rwkv_scan_body, t_valid=t_valid, t_total=t),
        grid=(b, t // C),
        in_specs=[seq] * 6 + [pl.BlockSpec((8, d), lambda bi, c: (0, 0)), st_spec],
        out_specs=[seq, st_spec],
        out_shape=[jax.ShapeDtypeStruct((b, t, d), BF16),
                   jax.ShapeDtypeStruct((b, d // LANES, LANES, LANES), F32)],
        scratch_shapes=[pltpu.VMEM((d // LANES, LANES, LANES), F32)],
        compiler_params=_params("parallel", "arbitrary"),
        name="rwkv_scan",
    )(r, k, v, wl, al, g, prm, s0)


def _conv_body(x_ref, prev_ref, w_ref, b_ref, o_ref, buf_ref, *, tt):
    halo = 8

    @pl.when(pl.program_id(2) == 0)
    def _():
        buf_ref[0:halo, :] = prev_ref[0]

    buf_ref[halo:halo + tt, :] = x_ref[0]
    acc = b_ref[...] + buf_ref[halo:halo + tt, :] * w_ref[SSM_CONV - 1:SSM_CONV, :]
    for back in range(1, SSM_CONV):
        tap = SSM_CONV - 1 - back
        acc = acc + buf_ref[halo - back:halo - back + tt, :] * w_ref[tap:tap + 1, :]
    o_ref[0] = _silu(acc)
    buf_ref[0:halo, :] = buf_ref[tt:tt + halo, :]


def causal_conv_silu(x, prev8, w, bias, tt=256, tc=1024):
    b, t, ch = x.shape
    tt = min(tt, t)
    assert t % tt == 0 and ch % tc == 0
    return pl.pallas_call(
        functools.partial(_conv_body, tt=tt),
        grid=(b, ch // tc, t // tt),
        in_specs=[pl.BlockSpec((1, tt, tc), lambda bi, ci, ti: (bi, ti, ci)),
                  pl.BlockSpec((1, 8, tc), lambda bi, ci, ti: (bi, 0, ci)),
                  pl.BlockSpec((SSM_CONV, tc), lambda bi, ci, ti: (0, ci)),
                  pl.BlockSpec((1, tc), lambda bi, ci, ti: (0, ci))],
        out_specs=pl.BlockSpec((1, tt, tc), lambda bi, ci, ti: (bi, ti, ci)),
        out_shape=jax.ShapeDtypeStruct((b, t, ch), F32),
        scratch_shapes=[pltpu.VMEM((tt + 8, tc), F32)],
        compiler_params=_params("parallel", "parallel", "arbitrary"),
        name="causal_conv_silu",
    )(x, prev8, w, bias.reshape(1, ch))


def _ssd_body(xa_ref, z_ref, dt_ref, dtt_ref, dtb_ref, dtbt_ref, al_ref, alt_ref, dsk_ref, nw_ref,
              h0_ref, y_ref, hf_ref, h_ref, *, t_valid, t_total):
    c = pl.program_id(1)
    Q, P = SSM_CHUNK, SSM_HEADDIM
    per_grp = SSM_HEADS // SSM_GROUPS
    gw = per_grp * P

    @pl.when(c == 0)
    def _():
        h_ref[...] = h0_ref[0]

    tri = _iota((Q, Q), 1) <= _iota((Q, Q), 0)
    tri_b = jnp.where(tri, 1.0, 0.0).astype(BF16)
    upp_b = jnp.where(_iota((Q, Q), 0) <= _iota((Q, Q), 1), 1.0, 0.0).astype(BF16)
    left = _iota((Q, LANES), 1) < P
    masked = t_valid < t_total

    def group(g, _):
        dt = _softplus(dt_ref[0, g] + dtb_ref[g])
        dtt = _softplus(dtt_ref[0, g] + dtbt_ref[g])
        if masked:
            dt = jnp.where(c * Q + _iota(dt.shape, 0) < t_valid, dt, 0.0)
            dtt = jnp.where(c * Q + _iota(dtt.shape, 1) < t_valid, dtt, 0.0)
        cum = _dot_exact_lhs(tri_b, dt * -jnp.exp(al_ref[g]))
        cumt = _dot_exact_rhs(dtt * -jnp.exp(alt_ref[g]), upp_b)
        ecum = jnp.exp(cum)
        tail = jnp.exp(cum[Q - 1:Q, :] - cum) * dt
        boff = pl.multiple_of(SSM_DINNER + g * SSM_STATE, SSM_STATE)
        coff = pl.multiple_of(SSM_DINNER + SSM_GROUPS * SSM_STATE + g * SSM_STATE, SSM_STATE)
        roff = pl.multiple_of(g * gw, gw)
        bm = xa_ref[0, :, pl.ds(boff, SSM_STATE)]
        cm = xa_ref[0, :, pl.ds(coff, SSM_STATE)]
        cb = _dot_nt(cm, bm)
        hg = h_ref[pl.ds(roff, gw), :]
        y_state = _dot_nt(cm, hg)
        xg = xa_ref[0, :, pl.ds(roff, gw)]
        ys, xts = [], []
        for q in range(gw // LANES):
            xp = xg[:, q * LANES:(q + 1) * LANES]
            halves = []
            for e in range(2):
                hh = 2 * q + e
                seg = cum[:, hh:hh + 1] - cumt[hh:hh + 1, :]
                dec = jnp.where(tri, jnp.exp(jnp.where(tri, seg, 0.0)), 0.0)
                halves.append(_dot(cb * dec * dtt[hh:hh + 1, :], xp))
            pick = lambda t: jnp.where(left, t[:, 2 * q:2 * q + 1], t[:, 2 * q + 1:2 * q + 2])
            ys.append(jnp.where(left, halves[0], halves[1])
                      + y_state[:, q * LANES:(q + 1) * LANES] * pick(ecum))
            xts.append(xp * pick(tail))
        upd = _dot(jnp.concatenate(xts, axis=1).T, bm)
        for hh in range(per_grp):
            rows = slice(hh * P, (hh + 1) * P)
            h_ref[pl.ds(pl.multiple_of(roff + hh * P, P), P), :] = (
                hg[rows] * jnp.exp(cumt[hh:hh + 1, Q - 1:Q]) + upd[rows])
        yg = jnp.concatenate(ys, axis=1)
        yg = (yg + xg * dsk_ref[:, pl.ds(roff, gw)]) * _silu(z_ref[0, :, pl.ds(roff, gw)])
        ms = jnp.mean(yg * yg, axis=-1, keepdims=True)
        y_ref[0, :, pl.ds(roff, gw)] = (
            yg * lax.rsqrt(ms + SSM_NORM_EPS) * nw_ref[:, pl.ds(roff, gw)]).astype(y_ref.dtype)
        return 0

    lax.fori_loop(0, SSM_GROUPS, group, 0)

    @pl.when(c == pl.num_programs(1) - 1)
    def _():
        hf_ref[0] = h_ref[...]


def ssd_scan(xa, z, dt_raw, dt_bias, a_log, d_skip, norm_w, h0, t_valid):
    b, t, _ = xa.shape
    Q = SSM_CHUNK
    per_grp = SSM_HEADS // SSM_GROUPS
    assert t % Q == 0
    dt_g = dt_raw.reshape(b, t, SSM_GROUPS, per_grp).transpose(0, 2, 1, 3)
    dt_gt = dt_g.transpose(0, 1, 3, 2)
    grp = lambda p: p.reshape(SSM_GROUPS, 1, per_grp)
    grp_t = lambda p: p.reshape(SSM_GROUPS, per_grp, 1)
    full3 = lambda s: pl.BlockSpec(s, lambda bi, c: (0, 0, 0))
    lanes = lambda w: pl.BlockSpec((1, w), lambda bi, c: (0, 0))
    st_spec = pl.BlockSpec((1, SSM_DINNER, SSM_STATE), lambda bi, c: (bi, 0, 0))
    return pl.pallas_call(
        functools.partial(_ssd_body, t_valid=t_valid, t_total=t),
        grid=(b, t // Q),
        in_specs=[pl.BlockSpec((1, Q, SSM_CONV_DIM), lambda bi, c: (bi, c, 0)),
                  pl.BlockSpec((1, Q, SSM_DINNER), lambda bi, c: (bi, c, 0)),
                  pl.BlockSpec((1, SSM_GROUPS, Q, per_grp), lambda bi, c: (bi, 0, c, 0)),
                  pl.BlockSpec((1, SSM_GROUPS, per_grp, Q), lambda bi, c: (bi, 0, 0, c)),
                  full3((SSM_GROUPS, 1, per_grp)), full3((SSM_GROUPS, per_grp, 1)),
                  full3((SSM_GROUPS, 1, per_grp)), full3((SSM_GROUPS, per_grp, 1)),
                  lanes(SSM_DINNER), lanes(SSM_DINNER), st_spec],
        out_specs=[pl.BlockSpec((1, Q, SSM_DINNER), lambda bi, c: (bi, c, 0)), st_spec],
        out_shape=[jax.ShapeDtypeStruct((b, t, SSM_DINNER), BF16),
                   jax.ShapeDtypeStruct((b, SSM_DINNER, SSM_STATE), F32)],
        scratch_shapes=[pltpu.VMEM((SSM_DINNER, SSM_STATE), F32)],
        compiler_params=_params("parallel", "arbitrary"),
        name="ssd_scan",
    )(xa, z, dt_g, dt_gt, grp(dt_bias), grp_t(dt_bias), grp(a_log), grp_t(a_log),
      jnp.repeat(d_skip, SSM_HEADDIM).reshape(1, SSM_DINNER), norm_w.reshape(1, SSM_DINNER), h0)


def _pad_time(x3, mult):
    t = x3.shape[1]
    tp = -(-t // mult) * mult
    return x3 if tp == t else jnp.pad(x3, ((0, 0), (0, tp - t), (0, 0)))


def _rwkv_mixer(xf, b, t, ln, shift0, wkv0, W):
    d = D_MODEL
    xn = rmsnorm(xf, ln, out_dtype=F32).reshape(b, t, d)
    x_prev = jnp.concatenate([shift0[:, None].astype(F32), xn[:, :-1]], axis=1)
    xn_p, xp_p = _pad_time(xn, RWKV_CHUNK), _pad_time(x_prev, RWKV_CHUNK)
    tp = xn_p.shape[1]
    xr, xw, xk, xv, xa, xg = rwkv_mix(xn_p.reshape(-1, d), xp_p.reshape(-1, d), W['rwkv_mu'])
    r = linear(xr, W['rwkv_wr'])
    k = linear(xk, W['rwkv_wk'])
    v = linear(xv, W['rwkv_wv'])
    wl = linear(linear(xw, W['rwkv_w1'], act=jnp.tanh, out_dtype=BF16), W['rwkv_w2'])
    al = linear(linear(xa, W['rwkv_a1'], out_dtype=BF16), W['rwkv_a2'])
    g = linear(linear(xg, W['rwkv_g1'], act=jax.nn.sigmoid, out_dtype=BF16), W['rwkv_g2'])
    prm = jnp.stack([W['rwkv_w0'], W['rwkv_a0'], W['rwkv_kk'], W['rwkv_ka'], W['rwkv_rk'].reshape(d),
                     W['rwkv_gn_w'], W['rwkv_gn_b'], jnp.zeros((d,), F32)])
    n_pair = d // LANES
    s4 = wkv0.astype(F32).reshape(b, n_pair, 2, RWKV_HD, RWKV_HD)
    zero = jnp.zeros_like(s4[:, :, 0])
    s0 = jnp.concatenate([jnp.concatenate([s4[:, :, 0], zero], axis=-1),
                          jnp.concatenate([zero, s4[:, :, 1]], axis=-1)], axis=-2)
    to3 = lambda a: a.reshape(b, tp, d)
    y, sf = rwkv_scan(to3(r), to3(k), to3(v), to3(wl), to3(al), to3(g), prm, s0, t_valid=t)
    wkv = jnp.stack([sf[:, :, :RWKV_HD, :RWKV_HD], sf[:, :, RWKV_HD:, RWKV_HD:]], axis=2)
    y = y[:, :t].reshape(b * t, d)
    return linear(y, W['rwkv_wo'], res=xf), xn[:, -1], wkv.reshape(b, d // RWKV_HD, RWKV_HD, RWKV_HD)


def _sb_mixer(xf, b, t, ln, sb_past, W):
    d = D_MODEL
    xn = rmsnorm(xf, ln)
    qdt = BF16 if sb_past is None else F32
    q = linear(xn, W['sb_wqkv'], col0=0, n=d, gain=W['sb_gq'], hw=SB_HD, out_dtype=qdt)
    k = linear(xn, W['sb_wqkv'], col0=d, n=d, gain=W['sb_gk'], hw=SB_HD)
    v = linear(xn, W['sb_wqkv'], col0=2 * d, n=d)
    to3 = lambda a: a.reshape(b, t, d)
    if sb_past is None:
        o = sb_attention_prompt(to3(q), to3(k), to3(v), W['sb_bias'])
    else:
        pool_k, pool_v, table = sb_past
        assert t == 1
        heads = lambda a: a.reshape(b, SB_HEADS, SB_HD)
        o = sb_attention_decode(heads(q), heads(k), heads(v), pool_k, pool_v, table, W['sb_bias'])
    xf = linear(o.reshape(b * t, d), W['sb_wo'], res=xf)
    return xf, k.reshape(b, t, SB_HEADS, SB_HD), v.reshape(b, t, SB_HEADS, SB_HD)


def _dil_mixer(xf, b, t, pos0, ln, dil_bufs, W):
    n_grp = len(DIL_PATTERNS)
    gd = n_grp * DIL_HEADS * DIL_HD
    xn = rmsnorm(xf, ln)
    rows = t if t >= 8 else b * t
    rope = rope_tables(pos0 + (jnp.arange(rows, dtype=jnp.int32) % t))
    tm = min(rows, 512)
    q = linear(xn, W['dil_wqkv'], col0=0, n=gd, gain=W['dil_gq'], hw=DIL_HD, rope=rope, tm=tm)
    k = linear(xn, W['dil_wqkv'], col0=gd, n=gd, gain=W['dil_gk'], hw=DIL_HD, rope=rope, tm=tm)
    v = linear(xn, W['dil_wqkv'], col0=2 * gd, n=gd, tm=tm)
    to3 = lambda a: a.reshape(b, t, gd)
    if dil_bufs is None:
        o = dil_attention_prompt(to3(q), to3(k), to3(v))
    else:
        assert t == 1
        heads = lambda a: a.reshape(b, n_grp, DIL_HEADS, DIL_HD)
        o = dil_attention_decode(heads(q), heads(k), heads(v), dil_bufs)
    xf = linear(o.reshape(b * t, DIL_HEADS * DIL_HD), W['dil_wo'], res=xf)
    k5 = k.reshape(b, t, n_grp, DIL_HEADS, DIL_HD)
    v5 = v.reshape(b, t, n_grp, DIL_HEADS, DIL_HD)
    states = []
    for g, (win, _) in enumerate(DIL_PATTERNS):
        keep = min(win, t)
        states += [k5[:, t - keep:, g], v5[:, t - keep:, g]]
    return xf, states


def _ssd_mixer(xf, b, t, ln, conv0, h0, W):
    d = D_MODEL
    xn = _pad_time(rmsnorm(xf, ln).reshape(b, t, d), SSM_CHUNK)
    tp = xn.shape[1]
    xn = xn.reshape(b * tp, d)
    z = linear(xn, W['ssm_win'], col0=0, n=SSM_DINNER)
    xbc = linear(xn, W['ssm_win'], col0=SSM_DINNER, n=SSM_CONV_DIM).reshape(b, tp, SSM_CONV_DIM)
    dt_raw = linear(xn, W['ssm_win'][:, SSM_DINNER + SSM_CONV_DIM:])
    prev8 = jnp.pad(conv0.astype(F32), ((0, 0), (8 - (SSM_CONV - 1), 0), (0, 0)))
    xa = causal_conv_silu(xbc, prev8, W['ssm_conv_w'], W['ssm_conv_b'])
    y, hf = ssd_scan(xa, z.reshape(b, tp, SSM_DINNER), dt_raw.reshape(b, tp, SSM_HEADS),
                     W['ssm_dt_bias'], W['ssm_a_log'], W['ssm_d'], W['ssm_norm_w'],
                     h0.astype(F32).reshape(b, SSM_DINNER, SSM_STATE), t_valid=t)
    xf = linear(y[:, :t].reshape(b * t, SSM_DINNER), W['ssm_wout'], res=xf)
    conv_state = jnp.concatenate([conv0.astype(F32), xbc[:, :t]], axis=1)[:, t:]
    return xf, conv_state, hf.reshape(b, SSM_HEADS, SSM_HEADDIM, SSM_STATE)


def _run_group(x, pos0, mem_k, mem_v, shift0, wkv0, sb_past, dil_bufs, conv0, h0, W):
    b, t, d = x.shape
    xf = x.reshape(b * t, d)
    st = {}
    for i in range(DEPTH):
        xf = ffn(xf, W['ln_ffn1'][i], W['ffn1_gate'], W['ffn1_up'], W['ffn1_down'], i)
        kind = i % 4
        ln = W['ln_mix'][i]
        if kind == 0:
            xf, st['rwkv_shift'], st['rwkv_wkv'] = _rwkv_mixer(xf, b, t, ln, shift0, wkv0, W)
        elif kind == 1:
            xf, st['sb_k'], st['sb_v'] = _sb_mixer(xf, b, t, ln, sb_past, W)
        elif kind == 2:
            xf, st['dil'] = _dil_mixer(xf, b, t, pos0, ln, dil_bufs, W)
        else:
            xf, st['ssm_conv'], st['ssm_h'] = _ssd_mixer(xf, b, t, ln, conv0, h0, W)
        xn = rmsnorm(xf, W['ln_mem'][i])
        q = linear(xn, W['mem_wq'], layer=i, gain=W['mem_gq'][i], hw=MEM_HD, out_dtype=BF16)
        q = _pad_time(q.reshape(b, t, d), 8)
        o = mem_attention(q, mem_k, mem_v, i)[:, :t].reshape(b * t, d)
        xf = linear(o, W['mem_wo'], layer=i, res=xf)
        xf = ffn(xf, W['ln_ffn2'][i], W['ffn2_gate'], W['ffn2_up'], W['ffn2_down'], i)
    return xf.reshape(b, t, d), st


def kernel(x_prompt, x_sample, state_rwkv_shift, state_rwkv_wkv, cache_sb_k, cache_sb_v, cache_dil0_k, cache_dil0_v, cache_dil1_k, cache_dil1_v, cache_dil2_k, cache_dil2_v, state_ssm_conv, state_ssm_h, cache_mem_k, cache_mem_v, page_table, mem_prompt, ln_ffn1, ffn1_gate, ffn1_up, ffn1_down, ln_mix, ln_mem, mem_wq, mem_gq, mem_wk, mem_gk, mem_wv, mem_wo, ln_ffn2, ffn2_gate, ffn2_up, ffn2_down, rwkv_mu, rwkv_wr, rwkv_wk, rwkv_wv, rwkv_wo, rwkv_w0, rwkv_w1, rwkv_w2, rwkv_a0, rwkv_a1, rwkv_a2, rwkv_g1, rwkv_g2, rwkv_kk, rwkv_ka, rwkv_rk, rwkv_gn_w, rwkv_gn_b, sb_wqkv, sb_gq, sb_gk, sb_bias, sb_wo, dil_wqkv, dil_gq, dil_gk, dil_wo, ssm_win, ssm_conv_w, ssm_conv_b, ssm_dt_bias, ssm_a_log, ssm_d, ssm_norm_w, ssm_wout):
    W = dict(locals())
    bp, _, d = x_prompt.shape
    bs = x_sample.shape[0]
    n_mem = mem_prompt.shape[1]
    past_len = page_table.shape[1] * cache_sb_k.shape[1]

    mem_rows = mem_prompt.reshape(bp * n_mem, d)
    p_mem_k = jnp.stack([linear(mem_rows, mem_wk, layer=i, gain=mem_gk[i], hw=MEM_HD)
                         for i in range(DEPTH)]).reshape(DEPTH, bp, n_mem, d)
    p_mem_v = jnp.stack([linear(mem_rows, mem_wv, layer=i)
                         for i in range(DEPTH)]).reshape(DEPTH, bp, n_mem, d)
    y_p, sp = _run_group(
        x_prompt, 0, p_mem_k, p_mem_v,
        jnp.zeros((bp, d), F32), jnp.zeros((bp, d // RWKV_HD, RWKV_HD, RWKV_HD), F32),
        None, None,
        jnp.zeros((bp, SSM_CONV - 1, SSM_CONV_DIM), F32),
        jnp.zeros((bp, SSM_HEADS, SSM_HEADDIM, SSM_STATE), F32), W)

    y_s, ss = _run_group(
        x_sample, past_len, cache_mem_k.reshape(DEPTH, bs, n_mem, d), cache_mem_v.reshape(DEPTH, bs, n_mem, d),
        state_rwkv_shift, state_rwkv_wkv, (cache_sb_k, cache_sb_v, page_table),
        (cache_dil0_k, cache_dil0_v, cache_dil1_k, cache_dil1_v, cache_dil2_k, cache_dil2_v),
        state_ssm_conv, state_ssm_h, W)

    mem_shape = (DEPTH, bp, n_mem, MEM_HEADS, MEM_HD)
    dil = []
    for g in range(len(DIL_PATTERNS)):
        dil += [sp['dil'][2 * g], sp['dil'][2 * g + 1], ss['dil'][2 * g], ss['dil'][2 * g + 1]]
    return (y_p, y_s,
            sp['rwkv_shift'], ss['rwkv_shift'], sp['rwkv_wkv'], ss['rwkv_wkv'],
            sp['sb_k'], sp['sb_v'], ss['sb_k'], ss['sb_v'],
            *dil,
            sp['ssm_conv'], ss['ssm_conv'], sp['ssm_h'], ss['ssm_h'],
            p_mem_k.reshape(mem_shape), p_mem_v.reshape(mem_shape))
```

```python
import functools
import math

import jax
import jax.numpy as jnp
from jax import lax
from jax.experimental import pallas as pl
from jax.experimental.pallas import tpu as pltpu

F32 = jnp.float32
BF16 = jnp.bfloat16

D_MODEL = 2048
DEPTH = 4
NORM_EPS = 1e-6
PAGE_SIZE = 128
MEM_HEADS = 4
MEM_HD = D_MODEL // MEM_HEADS
RWKV_HD = 64
GN_EPS = 64e-5
SB_HD = 128
SB_HEADS = D_MODEL // SB_HD
DIL_PATTERNS = ((128, 1), (512, 4), (2048, 16))
DIL_HD = 128
DIL_HEADS = 8
ROPE_THETA = 500000.0
ROPE_DIM = DIL_HD // 4
SSM_DINNER = 2 * D_MODEL
SSM_HEADDIM = 64
SSM_HEADS = SSM_DINNER // SSM_HEADDIM
SSM_STATE = 128
SSM_GROUPS = 8
SSM_CONV = 4
SSM_CONV_DIM = SSM_DINNER + 2 * SSM_GROUPS * SSM_STATE
SSM_CHUNK = 128
SSM_NORM_EPS = 1e-5

LANES = 128
BF16_ROWS = 16
V7X_VMEM_BYTES = 64 << 20
VMEM_LIMIT = V7X_VMEM_BYTES - (8 << 20)
FFN_VMEM_LIMIT = V7X_VMEM_BYTES - (4 << 20)
DIL_BLOCKS_PER_ITER = 4
LINEAR_TN = 512
LINEAR_TN_WIDE = 1024
RWKV_CHUNK = 64
RWKV_PAIRS_PER_ITER = 4
RWKV_PAIRS_PER_DOT = 2
NEG_BIG = -1e30


def _params(*sem, vmem=VMEM_LIMIT):
    return pltpu.CompilerParams(dimension_semantics=sem, vmem_limit_bytes=vmem)


def _dot(a, b):
    return jnp.dot(a.astype(BF16), b.astype(BF16), preferred_element_type=F32)


def _dot_nt(a, b):
    return lax.dot_general(a.astype(BF16), b.astype(BF16), (((1,), (1,)), ((), ())),
                           preferred_element_type=F32)


def _split3(x):
    hi = x.astype(BF16)
    r1 = x - hi.astype(F32)
    mid = r1.astype(BF16)
    lo = (r1 - mid.astype(F32)).astype(BF16)
    return hi, mid, lo


def _dot_exact_rhs(x, m_bf16):
    hi, mid, lo = _split3(x)
    d = lambda p: jnp.dot(p, m_bf16, preferred_element_type=F32)
    return d(hi) + d(mid) + d(lo)


def _dot_exact_lhs(m_bf16, x):
    hi, mid, lo = _split3(x)
    d = lambda p: jnp.dot(m_bf16, p, preferred_element_type=F32)
    return d(hi) + d(mid) + d(lo)


def _iota(shape, dim):
    return lax.broadcasted_iota(jnp.int32, shape, dim)


def _softplus(x):
    return jnp.maximum(x, 0.0) + jnp.log1p(jnp.exp(-jnp.abs(x)))


def _log_sigmoid(x):
    return jnp.minimum(x, 0.0) - jnp.log(1.0 + jnp.exp(-jnp.abs(x)))


def _silu(x):
    return x * jax.nn.sigmoid(x)


def _linear_body(*refs, has_gain, has_rope, has_res, act, hw, res_scale):
    it = iter(refs)
    x_ref, w_ref = next(it), next(it)
    gain_ref = next(it) if has_gain else None
    cos_ref, sin_ref = (next(it), next(it)) if has_rope else (None, None)
    res_ref = next(it) if has_res else None
    o_ref, wbf_ref = next(it), next(it)

    @pl.when(pl.program_id(1) == 0)
    def _():
        wbf_ref[...] = w_ref[...].astype(BF16)

    acc = jnp.dot(x_ref[...].astype(BF16), wbf_ref[...], preferred_element_type=F32)
    if act is not None:
        acc = act(acc)
    if has_res:
        acc = res_ref[...] + res_scale * acc
    if has_gain:
        for s in range(acc.shape[1] // hw):
            y = acc[:, s * hw:(s + 1) * hw]
            ms = jnp.mean(y * y, axis=-1, keepdims=True)
            y = y * lax.rsqrt(ms + NORM_EPS) * gain_ref[...]
            if has_rope:
                lane = _iota(y.shape, 1)
                half = ROPE_DIM // 2
                rot = jnp.where(lane < half, pltpu.roll(y, hw - half, 1), pltpu.roll(y, half, 1))
                y = y * cos_ref[...] + rot * sin_ref[...]
            o_ref[:, s * hw:(s + 1) * hw] = y.astype(o_ref.dtype)
    else:
        o_ref[...] = acc.astype(o_ref.dtype)


def linear(x, w, *, col0=0, n=None, out_dtype=F32, gain=None, hw=None, rope=None, res=None,
           res_scale=1.0, act=None, layer=None, tm=512, tn=None, name="linear"):
    m, k = x.shape
    n = w.shape[-1] - col0 if n is None else n
    if tn is None:
        tn = LINEAR_TN_WIDE if (n % LINEAR_TN_WIDE == 0 and col0 % LINEAR_TN_WIDE == 0
                                and k * LINEAR_TN_WIDE * 10 <= VMEM_LIMIT // 3 * 2) else LINEAR_TN
    tm, tn = min(tm, m), min(tn, n)
    assert m % tm == 0 and n % tn == 0 and col0 % tn == 0 and w.shape[-2] == k
    cb = col0 // tn
    if layer is None:
        w_spec = pl.BlockSpec((k, tn), lambda j, i: (0, j + cb))
    else:
        w_spec = pl.BlockSpec((None, k, tn), lambda j, i: (layer, 0, j + cb))
    in_specs = [pl.BlockSpec((tm, k), lambda j, i: (i, 0)), w_spec]
    args = [x, w]
    if gain is not None:
        assert tn % hw == 0
        in_specs.append(pl.BlockSpec((1, hw), lambda j, i: (0, 0)))
        args.append(gain.reshape(1, hw).astype(F32))
    if rope is not None:
        cos, sin = rope
        nt = cos.shape[0] // tm
        assert hw == LANES and cos.shape[0] % tm == 0
        in_specs += [pl.BlockSpec((tm, LANES), lambda j, i: (i % nt, 0))] * 2
        args += [cos, sin]
    if res is not None:
        in_specs.append(pl.BlockSpec((tm, tn), lambda j, i: (i, j)))
        args.append(res)
    body = functools.partial(_linear_body, has_gain=gain is not None, has_rope=rope is not None,
                             has_res=res is not None, act=act, hw=hw, res_scale=res_scale)
    return pl.pallas_call(
        body,
        grid=(n // tn, m // tm),
        in_specs=in_specs,
        out_specs=pl.BlockSpec((tm, tn), lambda j, i: (i, j)),
        out_shape=jax.ShapeDtypeStruct((m, n), out_dtype),
        scratch_shapes=[pltpu.VMEM((k, tn), BF16)],
        compiler_params=_params("parallel", "arbitrary"),
        name=name,
    )(*args)


def _rmsnorm_body(x_ref, g_ref, o_ref):
    x = x_ref[...]
    ms = jnp.mean(x * x, axis=-1, keepdims=True)
    o_ref[...] = (x * lax.rsqrt(ms + NORM_EPS) * g_ref[...]).astype(o_ref.dtype)


def rmsnorm(x, g, out_dtype=BF16, tm=512):
    m, d = x.shape
    tm = min(tm, m)
    assert m % tm == 0
    return pl.pallas_call(
        _rmsnorm_body,
        grid=(m // tm,),
        in_specs=[pl.BlockSpec((tm, d), lambda i: (i, 0)), pl.BlockSpec((1, d), lambda i: (0, 0))],
        out_specs=pl.BlockSpec((tm, d), lambda i: (i, 0)),
        out_shape=jax.ShapeDtypeStruct((m, d), out_dtype),
        compiler_params=_params("parallel"),
        name="rmsnorm",
    )(x, g.reshape(1, d))


def _ffn_body(x_ref, xr_ref, g_ref, wg_ref, wu_ref, wd_ref, o_ref, or_ref, xn_ref, *, tm):
    i, f = pl.program_id(0), pl.program_id(1)

    def start(src_ref, rows, dst_ref):
        x = src_ref[...]
        ms = jnp.mean(x * x, axis=-1, keepdims=True)
        xn_ref[rows, :] = (x * lax.rsqrt(ms + NORM_EPS) * g_ref[...]).astype(BF16)
        dst_ref[...] = x

    @pl.when(f == 0)
    def _():
        start(x_ref, slice(0, tm), o_ref)

    @pl.when((f == 0) & (i == 0))
    def _():
        start(xr_ref, slice(tm, tm + xr_ref.shape[0]), or_ref)

    xn = xn_ref[...]
    gate = jnp.dot(xn, wg_ref[...].astype(BF16), preferred_element_type=F32)
    up = jnp.dot(xn, wu_ref[...].astype(BF16), preferred_element_type=F32)
    h = (0.5 * _silu(gate) * up).astype(BF16)
    y = jnp.dot(h, wd_ref[...].astype(BF16), preferred_element_type=F32)
    o_ref[...] += y[:tm]
    or_ref[...] += jnp.where(i == 0, y[tm:], 0.0)


def ffn(x, x_rider, g, w_gate, w_up, w_down, layer, tm=1024, tf=512):
    m, d = x.shape
    mr = x_rider.shape[0]
    mr_pad = -(-mr // BF16_ROWS) * BF16_ROWS
    x_rider = jnp.pad(x_rider, ((0, mr_pad - mr), (0, 0)))
    f = w_gate.shape[-1]
    tm = min(tm, m)
    assert m % tm == 0 and f % tf == 0
    out, out_rider = pl.pallas_call(
        functools.partial(_ffn_body, tm=tm),
        grid=(m // tm, f // tf),
        in_specs=[pl.BlockSpec((tm, d), lambda i, j: (i, 0), pipeline_mode=pl.Buffered(1)),
                  pl.BlockSpec((mr_pad, d), lambda i, j: (0, 0)),
                  pl.BlockSpec((1, d), lambda i, j: (0, 0)),
                  pl.BlockSpec((None, d, tf), lambda i, j: (layer, 0, j)),
                  pl.BlockSpec((None, d, tf), lambda i, j: (layer, 0, j)),
                  pl.BlockSpec((None, tf, d), lambda i, j: (layer, j, 0))],
        out_specs=[pl.BlockSpec((tm, d), lambda i, j: (i, 0)),
                   pl.BlockSpec((mr_pad, d), lambda i, j: (0, 0))],
        out_shape=[jax.ShapeDtypeStruct((m, d), F32), jax.ShapeDtypeStruct((mr_pad, d), F32)],
        scratch_shapes=[pltpu.VMEM((tm + mr_pad, d), BF16)],
        compiler_params=_params("arbitrary", "arbitrary", vmem=FFN_VMEM_LIMIT),
        name="ffn",
    )(x, x_rider, g.reshape(1, d), w_gate, w_up, w_down)
    return out, out_rider[:mr]


def _memattn_body(q_ref, k_ref, v_ref, o_ref, *, scale, head_axis):
    each = lambda f, *cols: [f(*args) for args in zip(*cols)]
    lanes = [slice(h * MEM_HD, (h + 1) * MEM_HD) for h in range(MEM_HEADS)]
    if head_axis:
        ks = [k_ref[0, 0, :, h, :] for h in range(MEM_HEADS)]
        vs = [v_ref[0, 0, :, h, :] for h in range(MEM_HEADS)]
    else:
        ks = [k_ref[0, 0, :, sl] for sl in lanes]
        vs = [v_ref[0, 0, :, sl] for sl in lanes]
    s = each(lambda sl, k: _dot_nt(q_ref[0, :, sl], k) * scale, lanes, ks)
    m = each(lambda t: jnp.max(t, axis=-1, keepdims=True), s)
    p = each(lambda t, mm: jnp.exp(t - mm), s, m)
    l = each(lambda t: jnp.sum(t, axis=-1, keepdims=True), p)
    o = each(lambda t, v, ll: _dot(t, v) / ll, p, vs, l)
    for sl, t in zip(lanes, o):
        o_ref[0, :, sl] = t.astype(o_ref.dtype)


def mem_attention(q, mem_k, mem_v, layer, tq=512):
    b, t, d = q.shape
    n_mem = mem_k.shape[2]
    tq = min(tq, t)
    assert t % tq == 0
    head_axis = mem_k.ndim == 5
    if head_axis:
        kv_spec = pl.BlockSpec((1, 1, n_mem, MEM_HEADS, MEM_HD), lambda bi, ti: (layer, bi, 0, 0, 0))
    else:
        kv_spec = pl.BlockSpec((1, 1, n_mem, d), lambda bi, ti: (layer, bi, 0, 0))
    return pl.pallas_call(
        functools.partial(_memattn_body, scale=MEM_HD ** -0.5, head_axis=head_axis),
        grid=(b, t // tq),
        in_specs=[pl.BlockSpec((1, tq, d), lambda bi, ti: (bi, ti, 0)), kv_spec, kv_spec],
        out_specs=pl.BlockSpec((1, tq, d), lambda bi, ti: (bi, ti, 0)),
        out_shape=jax.ShapeDtypeStruct((b, t, d), BF16),
        compiler_params=_params("parallel", "parallel"),
        name="mem_attention",
    )(q, mem_k, mem_v)


def rope_tables(pos):
    half = ROPE_DIM // 2
    inv_freq = ROPE_THETA ** (-jnp.arange(half, dtype=F32) / half)
    ang = pos.astype(F32)[:, None] * inv_freq[None, :]
    cos, sin = jnp.cos(ang), jnp.sin(ang)
    rest = DIL_HD - ROPE_DIM
    n = pos.shape[0]
    cos_t = jnp.concatenate([cos, cos, jnp.ones((n, rest), F32)], axis=1)
    sin_t = jnp.concatenate([-sin, sin, jnp.zeros((n, rest), F32)], axis=1)
    return cos_t, sin_t


def _sb_tiles(qs, ks, vs, biases, causal, carries, accs, upper):
    each = lambda f, *cols: [f(*args) for args in zip(*cols)]
    mask = (lambda t: t) if causal is None else (lambda t: jnp.where(causal, t, 0.0))
    z = each(lambda q, k, b: _dot_nt(q, k) * (SB_HD ** -0.5) + b, qs, ks, biases)
    ls = each(_log_sigmoid, z)
    log_keep = each(lambda l, t: mask(l - t), ls, z)
    hi = each(lambda t: t.astype(BF16), log_keep)
    lo = each(lambda t, h: (t - h.astype(F32)).astype(BF16), log_keep, hi)
    local = each(lambda h, l: jnp.dot(jnp.concatenate([h, l], axis=0), upper, preferred_element_type=F32),
                 hi, lo)
    rows = qs[0].shape[0]
    att = each(lambda l, s, c: mask(jnp.exp(l + s[:rows] + s[rows:] + c)), ls, local, carries)
    accs = each(lambda a, p, v: a + _dot(p, v), accs, att, vs)
    carries = each(lambda c, t: c + jnp.sum(t, axis=-1, keepdims=True), carries, log_keep)
    return carries, accs


def _sb_tile(q, k, v, bias, causal, carry, acc, upper):
    carries, accs = _sb_tiles([q], [k], [v], [bias], causal, [carry], [acc], upper)
    return carries[0], accs[0]


def _upper_ones(n):
    return jnp.where(_iota((n, n), 0) > _iota((n, n), 1), 1.0, 0.0).astype(BF16)


def _sb_prompt_body(bias_ref, q_ref, k_ref, v_ref, o_ref, *, tq, heads):
    hb, qi = pl.program_id(1), pl.program_id(2)
    upper = _upper_ones(tq)
    lanes = [slice(u * SB_HD, (u + 1) * SB_HD) for u in range(heads)]
    qs = [q_ref[0, :, sl] for sl in lanes]
    biases = [bias_ref[hb * heads + u] for u in range(heads)]

    def tiles(j, state, causal):
        start = pl.multiple_of(j * tq, tq)
        ks = [k_ref[0, pl.ds(start, tq), sl] for sl in lanes]
        vs = [v_ref[0, pl.ds(start, tq), sl] for sl in lanes]
        return _sb_tiles(qs, ks, vs, biases, causal, state[0], state[1], upper)

    init = ([jnp.zeros((tq, 1), F32) for _ in lanes], [jnp.zeros((tq, SB_HD), F32) for _ in lanes])
    state = tiles(qi, init, _iota((tq, tq), 1) < _iota((tq, tq), 0))
    _, accs = lax.fori_loop(0, qi, lambda jj, st: tiles(qi - 1 - jj, st, None), state)
    for sl, acc in zip(lanes, accs):
        o_ref[0, :, sl] = acc.astype(o_ref.dtype)


def sb_attention_prompt(q, k, v, bias, tq=256, heads=2):
    b, t, d = q.shape
    tq = min(tq, t)
    w = heads * SB_HD
    assert t % tq == 0 and d % w == 0
    kv_spec = pl.BlockSpec((1, t, w), lambda bi, h, qi: (bi, 0, h))
    io_spec = pl.BlockSpec((1, tq, w), lambda bi, h, qi: (bi, qi, h))
    return pl.pallas_call(
        functools.partial(_sb_prompt_body, tq=tq, heads=heads),
        grid=(b, d // w, t // tq),
        in_specs=[pl.BlockSpec(memory_space=pltpu.SMEM), io_spec, kv_spec, kv_spec],
        out_specs=io_spec,
        out_shape=jax.ShapeDtypeStruct((b, t, d), BF16),
        compiler_params=_params("parallel", "parallel", "arbitrary"),
        name="sb_attention_prompt",
    )(bias, q, k, v)


def _sb_decode_body(table_ref, bias_ref, q_ref, kn_ref, vn_ref, *rest, n_pages, per_step):
    del table_ref
    page_refs = rest[:2 * per_step]
    o_ref, acc_ref, carry_ref = rest[2 * per_step:]
    j = pl.program_id(1)
    nh, cols = SB_HEADS, PAGE_SIZE * SB_HEADS
    n_sub = cols // LANES
    bias = bias_ref[...]
    past_len = n_pages * PAGE_SIZE
    scale = SB_HD ** -0.5
    q = q_ref[0]

    @pl.when(j == 0)
    def _():
        z = jnp.sum(q * kn_ref[0], axis=-1, keepdims=True) * scale + bias
        causal = jnp.full((nh, 1), past_len, jnp.int32) < past_len
        ls = _log_sigmoid(z)
        carry_ref[...] = jnp.where(causal, ls - z, 0.0)
        acc_ref[...] = jnp.where(causal, jnp.exp(ls), 0.0) * vn_ref[0]

    each = lambda f, *c: [f(*args) for args in zip(*c)]
    col = _iota((nh, cols), 1)
    own = (col & (nh - 1)) == _iota((nh, cols), 0)
    upper = _upper_ones(LANES)
    pages = [n_pages - 1 - (j * per_step + u) for u in range(per_step)]
    masks = [own & ((p * PAGE_SIZE + col // nh) < past_len) for p in pages]
    ks = [page_refs[2 * u][...] for u in range(per_step)]
    vs = [page_refs[2 * u + 1][...] for u in range(per_step)]
    z = each(lambda k: _dot_nt(q, k) * scale + bias, ks)
    ls = each(_log_sigmoid, z)
    log_keep = each(lambda l, t, m: jnp.where(m, l - t, 0.0), ls, z, masks)

    def local_suffix(t):
        x = jnp.concatenate([t[:, c * LANES:(c + 1) * LANES] for c in range(n_sub)], axis=0)
        hi = x.astype(BF16)
        lo = (x - hi.astype(F32)).astype(BF16)
        y = jnp.dot(jnp.concatenate([hi, lo], axis=0), upper, preferred_element_type=F32)
        y = y[:n_sub * nh] + y[n_sub * nh:]
        return [y[c * nh:(c + 1) * nh] for c in range(n_sub)]

    local = each(local_suffix, log_keep)
    carry = carry_ref[...]
    between = []
    for t, loc in zip(log_keep, local):
        pieces = [None] * n_sub
        for c in reversed(range(n_sub)):
            pieces[c] = loc[c] + carry
            carry = carry + jnp.sum(t[:, c * LANES:(c + 1) * LANES], axis=-1, keepdims=True)
        between.append(jnp.concatenate(pieces, axis=1))
    carry_ref[...] = carry
    att = each(lambda l, s, m: jnp.where(m, jnp.exp(l + s), 0.0), ls, between, masks)
    acc_ref[...] += sum(each(_dot, att, vs))

    @pl.when(j == pl.num_programs(1) - 1)
    def _():
        o_ref[0] = acc_ref[...]


def sb_attention_decode(q, k_new, v_new, pool_k, pool_v, table, bias, per_step=2):
    b = q.shape[0]
    n_pages = table.shape[1]
    assert n_pages % per_step == 0 and pool_k.shape[1:] == (PAGE_SIZE, SB_HEADS, SB_HD)
    assert SB_HEADS & (SB_HEADS - 1) == 0 and (PAGE_SIZE * SB_HEADS) % LANES == 0
    page_rows = PAGE_SIZE * SB_HEADS
    pool_k = pool_k.reshape(-1, SB_HD)
    pool_v = pool_v.reshape(-1, SB_HD)
    row_spec = pl.BlockSpec((1, SB_HEADS, SB_HD), lambda bi, j, tab: (bi, 0, 0))

    def page_spec(u):
        return pl.BlockSpec((page_rows, SB_HD),
                            lambda bi, j, tab: (tab[bi, n_pages - 1 - (j * per_step + u)], 0))

    pages = [page_spec(u) for u in range(per_step) for _ in range(2)]
    grid_spec = pltpu.PrefetchScalarGridSpec(
        num_scalar_prefetch=1,
        grid=(b, n_pages // per_step),
        in_specs=[pl.BlockSpec((SB_HEADS, 1), lambda bi, j, tab: (0, 0)),
                  row_spec, row_spec, row_spec] + pages,
        out_specs=row_spec,
        scratch_shapes=[pltpu.VMEM((SB_HEADS, SB_HD), F32), pltpu.VMEM((SB_HEADS, 1), F32)],
    )
    return pl.pallas_call(
        functools.partial(_sb_decode_body, n_pages=n_pages, per_step=per_step),
        grid_spec=grid_spec,
        out_shape=jax.ShapeDtypeStruct((b, SB_HEADS, SB_HD), F32),
        compiler_params=_params("parallel", "arbitrary"),
        name="sb_attention_decode",
    )(table, bias.reshape(SB_HEADS, 1), q, k_new, v_new, *([pool_k, pool_v] * per_step))


def _dil_prompt_body(*refs, t, blk):
    n_grp = len(DIL_PATTERNS)
    q_refs, k_refs, v_refs = refs[:n_grp], refs[n_grp:2 * n_grp], refs[2 * n_grp:3 * n_grp]
    o_ref, m_s, l_s, acc_s = refs[3 * n_grp:3 * n_grp + 4]
    scale = DIL_HD ** -0.5
    order = sorted(range(n_grp), key=lambda g: -DIL_PATTERNS[g][1])
    for n, g in enumerate(order):
        win, dil = DIL_PATTERNS[g]
        band, cls_len = win // dil, t // dil
        n_blk = cls_len // blk
        n_keys = min(2 * blk, cls_len)
        first, last = n == 0, n == n_grp - 1
        assert band == blk and (not last or dil == 1) and n_blk * blk == cls_len
        q_ref, k_ref, v_ref = q_refs[g], k_refs[g], v_refs[g]
        rel = _iota((blk, n_keys), 0) - _iota((blk, n_keys), 1)

        def rows(start, size, dil=dil):
            return pl.ds(start, size, stride=dil) if dil > 1 else pl.ds(start, size)

        def blocks(i, _, q_ref=q_ref, k_ref=k_ref, v_ref=v_ref, dil=dil, n_blk=n_blk, n_keys=n_keys,
                   rel=rel, rows=rows, first=first, last=last, band=band):
            each = lambda f, *cols: [f(*args) for args in zip(*cols)]
            idxs = [i * DIL_BLOCKS_PER_ITER + u for u in range(DIL_BLOCKS_PER_ITER)]
            lbs = [idx % n_blk for idx in idxs]
            k_cls = [jnp.maximum(lb - 1, 0) * blk for lb in lbs]
            if dil == 1:
                q_tok = [pl.multiple_of(lb * blk, blk) for lb in lbs]
                k_tok = [pl.multiple_of(kc, blk) for kc in k_cls]
            else:
                q_tok = [idx // n_blk + lb * blk * dil for idx, lb in zip(idxs, lbs)]
                k_tok = [idx // n_blk + kc * dil for idx, kc in zip(idxs, k_cls)]
            sel = [rows(t, blk) for t in q_tok]
            q = [q_ref[0, s, :] for s in sel]
            k = [k_ref[0, rows(t, n_keys), :] for t in k_tok]
            v = [v_ref[0, rows(t, n_keys), :] for t in k_tok]
            old = None if first else [(m_s[s, :], l_s[s, :], acc_s[s, :]) for s in sel]

            def scores(qq, kk, lb, kc):
                dist = rel + (lb * blk - kc)
                valid = jnp.where(dist >= 0, dist, band + 1) <= band
                return jnp.where(valid, _dot_nt(qq, kk) * scale, NEG_BIG)

            s = each(scores, q, k, lbs, k_cls)
            m = each(lambda t: jnp.max(t, axis=-1, keepdims=True), s)
            p = each(lambda t, mm: jnp.exp(t - mm), s, m)
            l = each(lambda t: jnp.sum(t, axis=-1, keepdims=True), p)
            acc = each(_dot, p, v)
            if not first:
                m_new = each(lambda o, mm: jnp.maximum(o[0], mm), old, m)
                e0 = each(lambda o, mn: jnp.exp(o[0] - mn), old, m_new)
                e1 = each(lambda mm, mn: jnp.exp(mm - mn), m, m_new)
                l = each(lambda o, a, b, t: a * o[1] + b * t, old, e0, e1, l)
                acc = each(lambda o, a, b, t: a * o[2] + b * t, old, e0, e1, acc)
                m = m_new
            for u, sl in enumerate(sel):
                if last:
                    o_ref[0, sl, :] = (acc[u] / l[u]).astype(o_ref.dtype)
                else:
                    m_s[sl, :], l_s[sl, :], acc_s[sl, :] = m[u], l[u], acc[u]
            return 0

        assert (dil * n_blk) % DIL_BLOCKS_PER_ITER == 0
        lax.fori_loop(0, dil * n_blk // DIL_BLOCKS_PER_ITER, blocks, 0)


def dil_attention_prompt(q, k, v, blk=128):
    b, t, _ = q.shape
    n_grp = len(DIL_PATTERNS)
    assert t % blk == 0

    def spec(g):
        return pl.BlockSpec((1, t, DIL_HD), lambda bi, h: (bi, 0, g * DIL_HEADS + h))

    return pl.pallas_call(
        functools.partial(_dil_prompt_body, t=t, blk=blk),
        grid=(b, DIL_HEADS),
        in_specs=[spec(g) for g in range(n_grp)] * 3,
        out_specs=pl.BlockSpec((1, t, DIL_HD), lambda bi, h: (bi, 0, h)),
        out_shape=jax.ShapeDtypeStruct((b, t, DIL_HEADS * DIL_HD), BF16),
        scratch_shapes=[pltpu.VMEM((t, 1), F32), pltpu.VMEM((t, 1), F32), pltpu.VMEM((t, DIL_HD), F32)],
        compiler_params=_params("parallel", "parallel"),
        name="dil_attention_prompt",
    )(*([q] * n_grp + [k] * n_grp + [v] * n_grp))


def _dil_decode_body(*refs):
    n_grp = len(DIL_PATTERNS)
    q_ref, kn_ref, vn_ref = refs[:3]
    kb_refs, vb_refs = refs[3:3 + n_grp], refs[3 + n_grp:3 + 2 * n_grp]
    o_ref = refs[3 + 2 * n_grp]
    scale = DIL_HD ** -0.5
    nh = DIL_HEADS
    scores, news, masks = [], [], []
    for g in range(n_grp):
        q = q_ref[0, g]
        band = kb_refs[g].shape[1]
        k2 = kb_refs[g][0].reshape(band * nh, DIL_HD)
        own = (_iota((nh, band * nh), 1) & (nh - 1)) == _iota((nh, band * nh), 0)
        masks.append(own)
        scores.append(jnp.where(own, _dot_nt(q, k2) * scale, NEG_BIG))
        news.append(jnp.sum(q * kn_ref[0, g], axis=-1, keepdims=True) * scale)
    m = functools.reduce(jnp.maximum, [jnp.max(s, axis=-1, keepdims=True) for s in scores] + news)
    l = jnp.zeros((nh, 1), F32)
    acc = jnp.zeros((nh, DIL_HD), F32)
    for g in range(n_grp):
        p = jnp.where(masks[g], jnp.exp(scores[g] - m), 0.0)
        pn = jnp.exp(news[g] - m)
        v2 = vb_refs[g][0].reshape(p.shape[1], DIL_HD)
        l = l + jnp.sum(p, axis=-1, keepdims=True) + pn
        acc = acc + _dot(p, v2) + pn * vn_ref[0, g]
    o_ref[0] = acc / l


def dil_attention_decode(q, k_new, v_new, bufs):
    b = q.shape[0]
    n_grp = len(DIL_PATTERNS)
    assert DIL_HEADS & (DIL_HEADS - 1) == 0
    row_spec = pl.BlockSpec((1, n_grp, DIL_HEADS, DIL_HD), lambda bi: (bi, 0, 0, 0))
    views, specs = [], []
    for which in range(2):
        for g, (win, dil) in enumerate(DIL_PATTERNS):
            buf = bufs[2 * g + which]
            assert buf.shape[1] == win and win % dil == 0
            views.append(buf.reshape(b, win // dil, dil * DIL_HEADS, DIL_HD))
            specs.append(pl.BlockSpec((1, win // dil, DIL_HEADS, DIL_HD), lambda bi: (bi, 0, 0, 0)))
    return pl.pallas_call(
        _dil_decode_body,
        grid=(b,),
        in_specs=[row_spec] * 3 + specs,
        out_specs=pl.BlockSpec((1, DIL_HEADS, DIL_HD), lambda bi: (bi, 0, 0)),
        out_shape=jax.ShapeDtypeStruct((b, DIL_HEADS, DIL_HD), F32),
        compiler_params=_params("parallel"),
        name="dil_attention_decode",
    )(q, k_new, v_new, *views)


def _rwkv_mix_body(x_ref, xp_ref, mu_ref, *o_refs):
    x = x_ref[...]
    xx = xp_ref[...] - x
    for i, o_ref in enumerate(o_refs):
        o_ref[...] = (x + xx * mu_ref[i:i + 1, :]).astype(o_ref.dtype)


def rwkv_mix(xn, x_prev, mu, tm=512):
    m, d = xn.shape
    tm = min(tm, m)
    n_mix = mu.shape[0]
    row = pl.BlockSpec((tm, d), lambda i: (i, 0))
    return pl.pallas_call(
        _rwkv_mix_body,
        grid=(m // tm,),
        in_specs=[row, row, pl.BlockSpec((n_mix, d), lambda i: (0, 0))],
        out_specs=[row] * n_mix,
        out_shape=[jax.ShapeDtypeStruct((m, d), BF16)] * n_mix,
        compiler_params=_params("parallel"),
        name="rwkv_mix",
    )(xn, x_prev, mu)


def _rwkv_scan_body(r_ref, k_ref, v_ref, wl_ref, al_ref, g_ref, prm_ref, s0_ref, y_ref, sf_ref,
                    st_ref, *, t_valid, t_total):
    c = pl.program_id(1)
    C, hd = RWKV_CHUNK, RWKV_HD
    n2 = 2 * C
    grp = RWKV_PAIRS_PER_DOT
    rows = grp * n2

    @pl.when(c == 0)
    def _():
        st_ref[...] = s0_ref[0]

    left = _iota((C, LANES), 1) < hd
    ri, ci = _iota((rows, rows), 0), _iota((rows, rows), 1)
    strict = (ci & (C - 1)) < (ri & (C - 1))
    incl = (ci & (C - 1)) <= (ri & (C - 1))
    tri = jnp.where(_iota((C, C), 1) <= _iota((C, C), 0), 1.0, 0.0).astype(BF16)
    same_head = (_iota((LANES, LANES), 0) // hd) == (_iota((LANES, LANES), 1) // hd)
    seg_ones = jnp.where(same_head, 1.0, 0.0).astype(BF16)
    masked = t_valid < t_total
    rows_valid = (c * C + _iota((C, LANES), 0)) < t_valid
    zero_slab = jnp.zeros((n2, LANES), F32)

    def seg_sum(x):
        hi = x.astype(BF16)
        lo = (x - hi.astype(F32)).astype(BF16)
        y = jnp.dot(jnp.concatenate([hi, lo], axis=0), seg_ones, preferred_element_type=F32)
        return y[:x.shape[0]] + y[x.shape[0]:]

    def cumsum_steps(x):
        w = x.shape[1]
        y = jnp.dot(tri, jnp.concatenate(_split3(x), axis=1), preferred_element_type=F32)
        return y[:, :w] + y[:, w:2 * w] + y[:, 2 * w:]

    def stack(x):
        return jnp.concatenate([jnp.where(left, x, 0.0), jnp.where(left, 0.0, x)], axis=0)

    def spread(slabs):
        return jnp.concatenate(
            [jnp.concatenate([s if q == j else zero_slab for j in range(grp)], axis=1)
             for q, s in enumerate(slabs)], axis=0)

    def own_lanes(x):
        return jnp.concatenate([x[q * n2:(q + 1) * n2, q * LANES:(q + 1) * LANES] for q in range(grp)],
                               axis=0)

    def core(groups):
        each = lambda f, *cols: [f(*args) for args in zip(*cols)]
        dss, stss = [g[0] for g in groups], [g[1] for g in groups]
        ar = each(lambda ds: jnp.concatenate([spread([stack(d['a_t']) for d in ds]),
                                              spread([stack(d['r_t']) for d in ds])],
                                             axis=0).astype(BF16), dss)
        bk = each(lambda ds: jnp.concatenate([spread([stack(d['b_t']) for d in ds]),
                                              spread([stack(d['k_t']) for d in ds])],
                                             axis=0).astype(BF16), dss)
        gram = each(_dot_nt, ar, bk)
        a_s = each(lambda a, sts: _dot_nt(a, spread(sts)), ar, stss)
        vv = each(lambda ds: jnp.concatenate([d['v'] for d in ds for _ in range(2)], axis=0), dss)
        y = each(lambda g, v, s: _dot(jnp.where(strict, g[:rows, rows:], 0.0), v) + own_lanes(s[:rows]),
                 gram, vv, a_s)
        pw = each(lambda g: jnp.where(strict, g[:rows, :rows], 0.0), gram)
        n = 1
        while n < C:
            y = each(lambda p, t: t + _dot(p, t), pw, y)
            n *= 2
            if n < C:
                pw = each(lambda p: _dot(p, p), pw)
        o2 = each(lambda g, t, v, s: _dot(
            jnp.concatenate([jnp.where(incl, g[rows:, :rows], 0.0),
                             jnp.where(incl, g[rows:, rows:], 0.0)], axis=1),
            jnp.concatenate([t, v], axis=0)) + own_lanes(s[rows:]), gram, y, vv, a_s)
        pick = lambda t, q: jnp.where(left, t[q * n2:q * n2 + C], t[q * n2 + C:(q + 1) * n2])
        upd = each(lambda ds, t: _dot(
            jnp.concatenate([jnp.concatenate([pick(t, q), d['v']], axis=0)
                             for q, d in enumerate(ds)], axis=1).T,
            jnp.concatenate([jnp.concatenate([d['b_end'], d['k_end']], axis=0) for d in ds], axis=1)),
            dss, y)
        os, s_new = [], []
        for ds, sts, o, up in zip(dss, stss, o2, upd):
            os += [pick(o, q) for q in range(grp)]
            s_new += [jnp.where(same_head, st * d['decay_end']
                                + up[q * LANES:(q + 1) * LANES, q * LANES:(q + 1) * LANES], 0.0)
                      for q, (d, st) in enumerate(zip(ds, sts))]
        return os, s_new

    def pairs(i, _):
        ps = [i * RWKV_PAIRS_PER_ITER + u for u in range(RWKV_PAIRS_PER_ITER)]
        offs = [pl.multiple_of(p * LANES, LANES) for p in ps]
        sts = [st_ref[p] for p in ps]
        ds = []
        for off in offs:
            r, k, v, wl, al, g = [ref[0, :, pl.ds(off, LANES)]
                                  for ref in (r_ref, k_ref, v_ref, wl_ref, al_ref, g_ref)]
            prm = prm_ref[:, pl.ds(off, LANES)]
            w0, a0, k_k, k_a, r_k, gn_w, gn_b = [prm[j:j + 1] for j in range(7)]
            a = jax.nn.sigmoid(a0 + al)
            ds.append(dict(r=r, v=v, g=g, a=a, gn_w=gn_w, gn_b=gn_b, r_k=r_k, kkr=k * k_k,
                           lw=-jnp.exp(-_softplus(-(w0 + wl)) - 0.5),
                           kmod=k * (1.0 + (a - 1.0) * k_a)))
        sums = seg_sum(jnp.concatenate([d['kkr'] * d['kkr'] for d in ds]
                                       + [d['r'] * d['kmod'] * d['r_k'] for d in ds], axis=0))
        n_p = len(ds)
        for u, d in enumerate(ds):
            d['kk'] = d['kkr'] * lax.rsqrt(jnp.maximum(sums[u * C:(u + 1) * C], 1e-24))
            d['bonus'] = sums[(n_p + u) * C:(n_p + u + 1) * C] * d['v']
            if masked:
                for name in ('lw', 'kk', 'kmod', 'v'):
                    d[name] = jnp.where(rows_valid, d[name], 0.0)
        lc_all = cumsum_steps(jnp.concatenate([d['lw'] for d in ds], axis=1))
        for u, d in enumerate(ds):
            lc = lc_all[:, u * LANES:(u + 1) * LANES]
            lend = lc[C - 1:C, :]
            kka = d['kk'] * d['a']
            e_inv, e_end = jnp.exp(-lc), jnp.exp(lend - lc)
            d.update(a_t=-d['kk'] * jnp.exp(lc - d['lw']), r_t=d['r'] * jnp.exp(lc),
                     b_t=kka * e_inv, k_t=d['kmod'] * e_inv, b_end=kka * e_end,
                     k_end=d['kmod'] * e_end, decay_end=jnp.exp(lend))
        os, s_news = core([(ds[q:q + grp], sts[q:q + grp]) for q in range(0, n_p, grp)])
        stats = seg_sum(jnp.concatenate(os + [o * o for o in os], axis=0))
        for u, (p, off, d, o) in enumerate(zip(ps, offs, ds, os)):
            mean = stats[u * C:(u + 1) * C] * (1.0 / hd)
            var = stats[(n_p + u) * C:(n_p + u + 1) * C] * (1.0 / hd) - mean * mean
            on = (o - mean) * lax.rsqrt(var + GN_EPS) * d['gn_w'] + d['gn_b']
            st_ref[p] = s_news[u]
            y_ref[0, :, pl.ds(off, LANES)] = ((on + d['bonus']) * d['g']).astype(y_ref.dtype)
        return 0

    lax.fori_loop(0, D_MODEL // LANES // RWKV_PAIRS_PER_ITER, pairs, 0)

    @pl.when(c == pl.num_programs(1) - 1)
    def _():
        sf_ref[0] = st_ref[...]


def rwkv_scan(r, k, v, wl, al, g, prm, s0, t_valid):
    b, t, d = r.shape
    C = RWKV_CHUNK
    assert t % C == 0 and d == D_MODEL
    seq = pl.BlockSpec((1, C, d), lambda bi, c: (bi, c, 0))
    st_spec = pl.BlockSpec((1, d // LANES, LANES, LANES), lambda bi, c: (bi, 0, 0, 0))
    return pl.pallas_call(
        functools.partial(_rwkv_scan_body, t_valid=t_valid, t_total=t),
        grid=(b, t // C),
        in_specs=[seq] * 6 + [pl.BlockSpec((8, d), lambda bi, c: (0, 0)), st_spec],
        out_specs=[seq, st_spec],
        out_shape=[jax.ShapeDtypeStruct((b, t, d), BF16),
                   jax.ShapeDtypeStruct((b, d // LANES, LANES, LANES), F32)],
        scratch_shapes=[pltpu.VMEM((d // LANES, LANES, LANES), F32)],
        compiler_params=_params("parallel", "arbitrary"),
        name="rwkv_scan",
    )(r, k, v, wl, al, g, prm, s0)


def _conv_body(x_ref, prev_ref, w_ref, b_ref, o_ref, buf_ref, *, tt):
    halo = 8

    @pl.when(pl.program_id(2) == 0)
    def _():
        buf_ref[0:halo, :] = prev_ref[0]

    buf_ref[halo:halo + tt, :] = x_ref[0]
    acc = b_ref[...] + buf_ref[halo:halo + tt, :] * w_ref[SSM_CONV - 1:SSM_CONV, :]
    for back in range(1, SSM_CONV):
        tap = SSM_CONV - 1 - back
        acc = acc + buf_ref[halo - back:halo - back + tt, :] * w_ref[tap:tap + 1, :]
    o_ref[0] = _silu(acc)
    buf_ref[0:halo, :] = buf_ref[tt:tt + halo, :]


def causal_conv_silu(x, prev8, w, bias, tt=256, tc=1024):
    b, t, ch = x.shape
    tt = min(tt, t)
    assert t % tt == 0 and ch % tc == 0
    return pl.pallas_call(
        functools.partial(_conv_body, tt=tt),
        grid=(b, ch // tc, t // tt),
        in_specs=[pl.BlockSpec((1, tt, tc), lambda bi, ci, ti: (bi, ti, ci)),
                  pl.BlockSpec((1, 8, tc), lambda bi, ci, ti: (bi, 0, ci)),
                  pl.BlockSpec((SSM_CONV, tc), lambda bi, ci, ti: (0, ci)),
                  pl.BlockSpec((1, tc), lambda bi, ci, ti: (0, ci))],
        out_specs=pl.BlockSpec((1, tt, tc), lambda bi, ci, ti: (bi, ti, ci)),
        out_shape=jax.ShapeDtypeStruct((b, t, ch), F32),
        scratch_shapes=[pltpu.VMEM((tt + 8, tc), F32)],
        compiler_params=_params("parallel", "parallel", "arbitrary"),
        name="causal_conv_silu",
    )(x, prev8, w, bias.reshape(1, ch))


def _ssd_body(xa_ref, z_ref, dt_ref, dtt_ref, dtb_ref, dtbt_ref, al_ref, alt_ref, dsk_ref, nw_ref,
              h0_ref, y_ref, hf_ref, h_ref, *, t_valid, t_total):
    c = pl.program_id(1)
    Q, P = SSM_CHUNK, SSM_HEADDIM
    per_grp = SSM_HEADS // SSM_GROUPS
    gw = per_grp * P

    @pl.when(c == 0)
    def _():
        h_ref[...] = h0_ref[0]

    tri = _iota((Q, Q), 1) <= _iota((Q, Q), 0)
    tri_b = jnp.where(tri, 1.0, 0.0).astype(BF16)
    upp_b = jnp.where(_iota((Q, Q), 0) <= _iota((Q, Q), 1), 1.0, 0.0).astype(BF16)
    left = _iota((Q, LANES), 1) < P
    masked = t_valid < t_total

    def group(g, _):
        dt = _softplus(dt_ref[0, g] + dtb_ref[g])
        dtt = _softplus(dtt_ref[0, g] + dtbt_ref[g])
        if masked:
            dt = jnp.where(c * Q + _iota(dt.shape, 0) < t_valid, dt, 0.0)
            dtt = jnp.where(c * Q + _iota(dtt.shape, 1) < t_valid, dtt, 0.0)
        cum = _dot_exact_lhs(tri_b, dt * -jnp.exp(al_ref[g]))
        cumt = _dot_exact_rhs(dtt * -jnp.exp(alt_ref[g]), upp_b)
        ecum = jnp.exp(cum)
        tail = jnp.exp(cum[Q - 1:Q, :] - cum) * dt
        boff = pl.multiple_of(SSM_DINNER + g * SSM_STATE, SSM_STATE)
        coff = pl.multiple_of(SSM_DINNER + SSM_GROUPS * SSM_STATE + g * SSM_STATE, SSM_STATE)
        roff = pl.multiple_of(g * gw, gw)
        bm = xa_ref[0, :, pl.ds(boff, SSM_STATE)]
        cm = xa_ref[0, :, pl.ds(coff, SSM_STATE)]
        cb = _dot_nt(cm, bm)
        hg = h_ref[pl.ds(roff, gw), :]
        y_state = _dot_nt(cm, hg)
        xg = xa_ref[0, :, pl.ds(roff, gw)]
        ys, xts = [], []
        for q in range(gw // LANES):
            xp = xg[:, q * LANES:(q + 1) * LANES]
            halves = []
            for e in range(2):
                hh = 2 * q + e
                seg = cum[:, hh:hh + 1] - cumt[hh:hh + 1, :]
                dec = jnp.where(tri, jnp.exp(jnp.where(tri, seg, 0.0)), 0.0)
                halves.append(_dot(cb * dec * dtt[hh:hh + 1, :], xp))
            pick = lambda t: jnp.where(left, t[:, 2 * q:2 * q + 1], t[:, 2 * q + 1:2 * q + 2])
            ys.append(jnp.where(left, halves[0], halves[1])
                      + y_state[:, q * LANES:(q + 1) * LANES] * pick(ecum))
            xts.append(xp * pick(tail))
        upd = _dot(jnp.concatenate(xts, axis=1).T, bm)
        for hh in range(per_grp):
            rows = slice(hh * P, (hh + 1) * P)
            h_ref[pl.ds(pl.multiple_of(roff + hh * P, P), P), :] = (
                hg[rows] * jnp.exp(cumt[hh:hh + 1, Q - 1:Q]) + upd[rows])
        yg = jnp.concatenate(ys, axis=1)
        yg = (yg + xg * dsk_ref[:, pl.ds(roff, gw)]) * _silu(z_ref[0, :, pl.ds(roff, gw)])
        ms = jnp.mean(yg * yg, axis=-1, keepdims=True)
        y_ref[0, :, pl.ds(roff, gw)] = (
            yg * lax.rsqrt(ms + SSM_NORM_EPS) * nw_ref[:, pl.ds(roff, gw)]).astype(y_ref.dtype)
        return 0

    lax.fori_loop(0, SSM_GROUPS, group, 0)

    @pl.when(c == pl.num_programs(1) - 1)
    def _():
        hf_ref[0] = h_ref[...]


def ssd_scan(xa, z, dt_raw, dt_bias, a_log, d_skip, norm_w, h0, t_valid):
    b, t, _ = xa.shape
    Q = SSM_CHUNK
    per_grp = SSM_HEADS // SSM_GROUPS
    assert t % Q == 0
    dt_g = dt_raw.reshape(b, t, SSM_GROUPS, per_grp).transpose(0, 2, 1, 3)
    dt_gt = dt_g.transpose(0, 1, 3, 2)
    grp = lambda p: p.reshape(SSM_GROUPS, 1, per_grp)
    grp_t = lambda p: p.reshape(SSM_GROUPS, per_grp, 1)
    full3 = lambda s: pl.BlockSpec(s, lambda bi, c: (0, 0, 0))
    lanes = lambda w: pl.BlockSpec((1, w), lambda bi, c: (0, 0))
    st_spec = pl.BlockSpec((1, SSM_DINNER, SSM_STATE), lambda bi, c: (bi, 0, 0))
    return pl.pallas_call(
        functools.partial(_ssd_body, t_valid=t_valid, t_total=t),
        grid=(b, t // Q),
        in_specs=[pl.BlockSpec((1, Q, SSM_CONV_DIM), lambda bi, c: (bi, c, 0)),
                  pl.BlockSpec((1, Q, SSM_DINNER), lambda bi, c: (bi, c, 0)),
                  pl.BlockSpec((1, SSM_GROUPS, Q, per_grp), lambda bi, c: (bi, 0, c, 0)),
                  pl.BlockSpec((1, SSM_GROUPS, per_grp, Q), lambda bi, c: (bi, 0, 0, c)),
                  full3((SSM_GROUPS, 1, per_grp)), full3((SSM_GROUPS, per_grp, 1)),
                  full3((SSM_GROUPS, 1, per_grp)), full3((SSM_GROUPS, per_grp, 1)),
                  lanes(SSM_DINNER), lanes(SSM_DINNER), st_spec],
        out_specs=[pl.BlockSpec((1, Q, SSM_DINNER), lambda bi, c: (bi, c, 0)), st_spec],
        out_shape=[jax.ShapeDtypeStruct((b, t, SSM_DINNER), BF16),
                   jax.ShapeDtypeStruct((b, SSM_DINNER, SSM_STATE), F32)],
        scratch_shapes=[pltpu.VMEM((SSM_DINNER, SSM_STATE), F32)],
        compiler_params=_params("parallel", "arbitrary"),
        name="ssd_scan",
    )(xa, z, dt_g, dt_gt, grp(dt_bias), grp_t(dt_bias), grp(a_log), grp_t(a_log),
      jnp.repeat(d_skip, SSM_HEADDIM).reshape(1, SSM_DINNER), norm_w.reshape(1, SSM_DINNER), h0)


def _pad_time(x3, mult):
    t = x3.shape[1]
    tp = -(-t // mult) * mult
    return x3 if tp == t else jnp.pad(x3, ((0, 0), (0, tp - t), (0, 0)))


def _rwkv_mixer(xf, b, t, ln, shift0, wkv0, W):
    d = D_MODEL
    xn = rmsnorm(xf, ln, out_dtype=F32).reshape(b, t, d)
    x_prev = jnp.concatenate([shift0[:, None].astype(F32), xn[:, :-1]], axis=1)
    xn_p, xp_p = _pad_time(xn, RWKV_CHUNK), _pad_time(x_prev, RWKV_CHUNK)
    tp = xn_p.shape[1]
    xr, xw, xk, xv, xa, xg = rwkv_mix(xn_p.reshape(-1, d), xp_p.reshape(-1, d), W['rwkv_mu'])
    r = linear(xr, W['rwkv_wr'])
    k = linear(xk, W['rwkv_wk'])
    v = linear(xv, W['rwkv_wv'])
    wl = linear(linear(xw, W['rwkv_w1'], act=jnp.tanh, out_dtype=BF16), W['rwkv_w2'])
    al = linear(linear(xa, W['rwkv_a1'], out_dtype=BF16), W['rwkv_a2'])
    g = linear(linear(xg, W['rwkv_g1'], act=jax.nn.sigmoid, out_dtype=BF16), W['rwkv_g2'])
    prm = jnp.stack([W['rwkv_w0'], W['rwkv_a0'], W['rwkv_kk'], W['rwkv_ka'], W['rwkv_rk'].reshape(d),
                     W['rwkv_gn_w'], W['rwkv_gn_b'], jnp.zeros((d,), F32)])
    n_pair = d // LANES
    s4 = wkv0.astype(F32).reshape(b, n_pair, 2, RWKV_HD, RWKV_HD)
    zero = jnp.zeros_like(s4[:, :, 0])
    s0 = jnp.concatenate([jnp.concatenate([s4[:, :, 0], zero], axis=-1),
                          jnp.concatenate([zero, s4[:, :, 1]], axis=-1)], axis=-2)
    to3 = lambda a: a.reshape(b, tp, d)
    y, sf = rwkv_scan(to3(r), to3(k), to3(v), to3(wl), to3(al), to3(g), prm, s0, t_valid=t)
    wkv = jnp.stack([sf[:, :, :RWKV_HD, :RWKV_HD], sf[:, :, RWKV_HD:, RWKV_HD:]], axis=2)
    y = y[:, :t].reshape(b * t, d)
    return linear(y, W['rwkv_wo'], res=xf), xn[:, -1], wkv.reshape(b, d // RWKV_HD, RWKV_HD, RWKV_HD)


def _sb_mixer(xf, b, t, ln, sb_past, W):
    d = D_MODEL
    xn = rmsnorm(xf, ln)
    qdt = BF16 if sb_past is None else F32
    q = linear(xn, W['sb_wqkv'], col0=0, n=d, gain=W['sb_gq'], hw=SB_HD, out_dtype=qdt)
    k = linear(xn, W['sb_wqkv'], col0=d, n=d, gain=W['sb_gk'], hw=SB_HD)
    v = linear(xn, W['sb_wqkv'], col0=2 * d, n=d)
    to3 = lambda a: a.reshape(b, t, d)
    if sb_past is None:
        o = sb_attention_prompt(to3(q), to3(k), to3(v), W['sb_bias'])
    else:
        pool_k, pool_v, table = sb_past
        assert t == 1
        heads = lambda a: a.reshape(b, SB_HEADS, SB_HD)
        o = sb_attention_decode(heads(q), heads(k), heads(v), pool_k, pool_v, table, W['sb_bias'])
    xf = linear(o.reshape(b * t, d), W['sb_wo'], res=xf)
    return xf, k.reshape(b, t, SB_HEADS, SB_HD), v.reshape(b, t, SB_HEADS, SB_HD)


def _dil_mixer(xf, b, t, pos0, ln, dil_bufs, W):
    n_grp = len(DIL_PATTERNS)
    gd = n_grp * DIL_HEADS * DIL_HD
    xn = rmsnorm(xf, ln)
    rows = t if t >= 8 else b * t
    rope = rope_tables(pos0 + (jnp.arange(rows, dtype=jnp.int32) % t))
    tm = min(rows, 512)
    q = linear(xn, W['dil_wqkv'], col0=0, n=gd, gain=W['dil_gq'], hw=DIL_HD, rope=rope, tm=tm)
    k = linear(xn, W['dil_wqkv'], col0=gd, n=gd, gain=W['dil_gk'], hw=DIL_HD, rope=rope, tm=tm)
    v = linear(xn, W['dil_wqkv'], col0=2 * gd, n=gd, tm=tm)
    to3 = lambda a: a.reshape(b, t, gd)
    if dil_bufs is None:
        o = dil_attention_prompt(to3(q), to3(k), to3(v))
    else:
        assert t == 1
        heads = lambda a: a.reshape(b, n_grp, DIL_HEADS, DIL_HD)
        o = dil_attention_decode(heads(q), heads(k), heads(v), dil_bufs)
    xf = linear(o.reshape(b * t, DIL_HEADS * DIL_HD), W['dil_wo'], res=xf)
    k5 = k.reshape(b, t, n_grp, DIL_HEADS, DIL_HD)
    v5 = v.reshape(b, t, n_grp, DIL_HEADS, DIL_HD)
    states = []
    for g, (win, _) in enumerate(DIL_PATTERNS):
        keep = min(win, t)
        states += [k5[:, t - keep:, g], v5[:, t - keep:, g]]
    return xf, states


def _ssd_mixer(xf, b, t, ln, conv0, h0, W):
    d = D_MODEL
    xn = _pad_time(rmsnorm(xf, ln).reshape(b, t, d), SSM_CHUNK)
    tp = xn.shape[1]
    xn = xn.reshape(b * tp, d)
    z = linear(xn, W['ssm_win'], col0=0, n=SSM_DINNER)
    xbc = linear(xn, W['ssm_win'], col0=SSM_DINNER, n=SSM_CONV_DIM).reshape(b, tp, SSM_CONV_DIM)
    dt_raw = linear(xn, W['ssm_win'][:, SSM_DINNER + SSM_CONV_DIM:])
    prev8 = jnp.pad(conv0.astype(F32), ((0, 0), (8 - (SSM_CONV - 1), 0), (0, 0)))
    xa = causal_conv_silu(xbc, prev8, W['ssm_conv_w'], W['ssm_conv_b'])
    y, hf = ssd_scan(xa, z.reshape(b, tp, SSM_DINNER), dt_raw.reshape(b, tp, SSM_HEADS),
                     W['ssm_dt_bias'], W['ssm_a_log'], W['ssm_d'], W['ssm_norm_w'],
                     h0.astype(F32).reshape(b, SSM_DINNER, SSM_STATE), t_valid=t)
    xf = linear(y[:, :t].reshape(b * t, SSM_DINNER), W['ssm_wout'], res=xf)
    conv_state = jnp.concatenate([conv0.astype(F32), xbc[:, :t]], axis=1)[:, t:]
    return xf, conv_state, hf.reshape(b, SSM_HEADS, SSM_HEADDIM, SSM_STATE)


def _mixer_and_memory(xf, i, G, W):
    b, t, d = G['shape']
    st = G['out']
    kind = i % 4
    ln = W['ln_mix'][i]
    if kind == 0:
        xf, st['rwkv_shift'], st['rwkv_wkv'] = _rwkv_mixer(xf, b, t, ln, G['shift0'], G['wkv0'], W)
    elif kind == 1:
        xf, st['sb_k'], st['sb_v'] = _sb_mixer(xf, b, t, ln, G['sb_past'], W)
    elif kind == 2:
        xf, st['dil'] = _dil_mixer(xf, b, t, G['pos0'], ln, G['dil_bufs'], W)
    else:
        xf, st['ssm_conv'], st['ssm_h'] = _ssd_mixer(xf, b, t, ln, G['conv0'], G['h0'], W)
    xn = rmsnorm(xf, W['ln_mem'][i])
    q = linear(xn, W['mem_wq'], layer=i, gain=W['mem_gq'][i], hw=MEM_HD, out_dtype=BF16)
    q = _pad_time(q.reshape(b, t, d), 8)
    o = mem_attention(q, G['mem_k'], G['mem_v'], i)[:, :t].reshape(b * t, d)
    return linear(o, W['mem_wo'], layer=i, res=xf)


def _run_groups(P, S, W):
    xp, xs = [G['x'].reshape(-1, D_MODEL) for G in (P, S)]
    for i in range(DEPTH):
        xp, xs = ffn(xp, xs, W['ln_ffn1'][i], W['ffn1_gate'], W['ffn1_up'], W['ffn1_down'], i)
        xp = _mixer_and_memory(xp, i, P, W)
        xs = _mixer_and_memory(xs, i, S, W)
        xp, xs = ffn(xp, xs, W['ln_ffn2'][i], W['ffn2_gate'], W['ffn2_up'], W['ffn2_down'], i)
    return xp.reshape(P['shape']), xs.reshape(S['shape'])


def kernel(x_prompt, x_sample, state_rwkv_shift, state_rwkv_wkv, cache_sb_k, cache_sb_v, cache_dil0_k, cache_dil0_v, cache_dil1_k, cache_dil1_v, cache_dil2_k, cache_dil2_v, state_ssm_conv, state_ssm_h, cache_mem_k, cache_mem_v, page_table, mem_prompt, ln_ffn1, ffn1_gate, ffn1_up, ffn1_down, ln_mix, ln_mem, mem_wq, mem_gq, mem_wk, mem_gk, mem_wv, mem_wo, ln_ffn2, ffn2_gate, ffn2_up, ffn2_down, rwkv_mu, rwkv_wr, rwkv_wk, rwkv_wv, rwkv_wo, rwkv_w0, rwkv_w1, rwkv_w2, rwkv_a0, rwkv_a1, rwkv_a2, rwkv_g1, rwkv_g2, rwkv_kk, rwkv_ka, rwkv_rk, rwkv_gn_w, rwkv_gn_b, sb_wqkv, sb_gq, sb_gk, sb_bias, sb_wo, dil_wqkv, dil_gq, dil_gk, dil_wo, ssm_win, ssm_conv_w, ssm_conv_b, ssm_dt_bias, ssm_a_log, ssm_d, ssm_norm_w, ssm_wout):
    W = dict(locals())
    bp, _, d = x_prompt.shape
    n_mem = mem_prompt.shape[1]
    past_len = page_table.shape[1] * cache_sb_k.shape[1]

    mem_rows = mem_prompt.reshape(bp * n_mem, d)
    p_mem_k = jnp.stack([linear(mem_rows, mem_wk, layer=i, gain=mem_gk[i], hw=MEM_HD)
                         for i in range(DEPTH)]).reshape(DEPTH, bp, n_mem, d)
    p_mem_v = jnp.stack([linear(mem_rows, mem_wv, layer=i)
                         for i in range(DEPTH)]).reshape(DEPTH, bp, n_mem, d)
    P = dict(x=x_prompt, shape=x_prompt.shape, pos0=0, mem_k=p_mem_k, mem_v=p_mem_v,
             shift0=jnp.zeros((bp, d), F32), wkv0=jnp.zeros((bp, d // RWKV_HD, RWKV_HD, RWKV_HD), F32),
             sb_past=None, dil_bufs=None,
             conv0=jnp.zeros((bp, SSM_CONV - 1, SSM_CONV_DIM), F32),
             h0=jnp.zeros((bp, SSM_HEADS, SSM_HEADDIM, SSM_STATE), F32), out={})
    S = dict(x=x_sample, shape=x_sample.shape, pos0=past_len, mem_k=cache_mem_k, mem_v=cache_mem_v,
             shift0=state_rwkv_shift, wkv0=state_rwkv_wkv, sb_past=(cache_sb_k, cache_sb_v, page_table),
             dil_bufs=(cache_dil0_k, cache_dil0_v, cache_dil1_k, cache_dil1_v, cache_dil2_k, cache_dil2_v),
             conv0=state_ssm_conv, h0=state_ssm_h, out={})
    y_p, y_s = _run_groups(P, S, W)
    sp, ss = P['out'], S['out']

    mem_shape = (DEPTH, bp, n_mem, MEM_HEADS, MEM_HD)
    dil = []
    for g in range(len(DIL_PATTERNS)):
        dil += [sp['dil'][2 * g], sp['dil'][2 * g + 1], ss['dil'][2 * g], ss['dil'][2 * g + 1]]
    return (y_p, y_s,
            sp['rwkv_shift'], ss['rwkv_shift'], sp['rwkv_wkv'], ss['rwkv_wkv'],
            sp['sb_k'], sp['sb_v'], ss['sb_k'], ss['sb_v'],
            *dil,
            sp['ssm_conv'], ss['ssm_conv'], sp['ssm_h'], ss['ssm_h'],
            p_mem_k.reshape(mem_shape), p_mem_v.reshape(mem_shape))
```

```python
import functools
import math

import jax
import jax.numpy as jnp
from jax import lax
from jax.experimental import pallas as pl
from jax.experimental.pallas import tpu as pltpu

F32 = jnp.float32
BF16 = jnp.bfloat16

D_MODEL = 2048
DEPTH = 4
NORM_EPS = 1e-6
PAGE_SIZE = 128
MEM_HEADS = 4
MEM_HD = D_MODEL // MEM_HEADS
RWKV_HD = 64
GN_EPS = 64e-5
SB_HD = 128
SB_HEADS = D_MODEL // SB_HD
DIL_PATTERNS = ((128, 1), (512, 4), (2048, 16))
DIL_HD = 128
DIL_HEADS = 8
ROPE_THETA = 500000.0
ROPE_DIM = DIL_HD // 4
SSM_DINNER = 2 * D_MODEL
SSM_HEADDIM = 64
SSM_HEADS = SSM_DINNER // SSM_HEADDIM
SSM_STATE = 128
SSM_GROUPS = 8
SSM_CONV = 4
SSM_CONV_DIM = SSM_DINNER + 2 * SSM_GROUPS * SSM_STATE
SSM_CHUNK = 128
SSM_NORM_EPS = 1e-5

LANES = 128
BF16_ROWS = 16
V7X_VMEM_BYTES = 64 << 20
VMEM_LIMIT = V7X_VMEM_BYTES - (8 << 20)
FFN_VMEM_LIMIT = V7X_VMEM_BYTES - (4 << 20)
DIL_BLOCKS_PER_ITER = 8
LINEAR_TM = 1024
LINEAR_TN = 512
LINEAR_TN_WIDE = 1024
RWKV_CHUNK = 64
RWKV_PAIRS_PER_ITER = 16
RWKV_PAIRS_PER_DOT = 2
NEG_BIG = -1e30


def _params(*sem, vmem=VMEM_LIMIT):
    return pltpu.CompilerParams(dimension_semantics=sem, vmem_limit_bytes=vmem)


def _dot(a, b):
    return jnp.dot(a.astype(BF16), b.astype(BF16), preferred_element_type=F32)


def _dot_nt(a, b):
    return lax.dot_general(a.astype(BF16), b.astype(BF16), (((1,), (1,)), ((), ())),
                           preferred_element_type=F32)


def _split3(x):
    hi = x.astype(BF16)
    r1 = x - hi.astype(F32)
    mid = r1.astype(BF16)
    lo = (r1 - mid.astype(F32)).astype(BF16)
    return hi, mid, lo


def _dot_exact_rhs(x, m_bf16):
    hi, mid, lo = _split3(x)
    d = lambda p: jnp.dot(p, m_bf16, preferred_element_type=F32)
    return d(hi) + d(mid) + d(lo)


def _dot_exact_lhs(m_bf16, x):
    hi, mid, lo = _split3(x)
    d = lambda p: jnp.dot(m_bf16, p, preferred_element_type=F32)
    return d(hi) + d(mid) + d(lo)


def _iota(shape, dim):
    return lax.broadcasted_iota(jnp.int32, shape, dim)


def _softplus(x):
    return jnp.maximum(x, 0.0) + jnp.log1p(jnp.exp(-jnp.abs(x)))


def _log_sigmoid(x):
    return jnp.minimum(x, 0.0) - jnp.log(1.0 + jnp.exp(-jnp.abs(x)))


def _silu(x):
    return x * jax.nn.sigmoid(x)


def _linear_body(*refs, has_gain, has_rope, has_res, act, hw, res_scale):
    it = iter(refs)
    x_ref, w_ref = next(it), next(it)
    gain_ref = next(it) if has_gain else None
    cos_ref, sin_ref = (next(it), next(it)) if has_rope else (None, None)
    res_ref = next(it) if has_res else None
    o_ref, wbf_ref = next(it), next(it)

    @pl.when(pl.program_id(1) == 0)
    def _():
        wbf_ref[...] = w_ref[...].astype(BF16)

    acc = jnp.dot(x_ref[...].astype(BF16), wbf_ref[...], preferred_element_type=F32)
    if act is not None:
        acc = act(acc)
    if has_res:
        acc = res_ref[...] + res_scale * acc
    if has_gain:
        for s in range(acc.shape[1] // hw):
            y = acc[:, s * hw:(s + 1) * hw]
            ms = jnp.mean(y * y, axis=-1, keepdims=True)
            y = y * lax.rsqrt(ms + NORM_EPS) * gain_ref[...]
            if has_rope:
                lane = _iota(y.shape, 1)
                half = ROPE_DIM // 2
                rot = jnp.where(lane < half, pltpu.roll(y, hw - half, 1), pltpu.roll(y, half, 1))
                y = y * cos_ref[...] + rot * sin_ref[...]
            o_ref[:, s * hw:(s + 1) * hw] = y.astype(o_ref.dtype)
    else:
        o_ref[...] = acc.astype(o_ref.dtype)


def linear(x, w, *, col0=0, n=None, out_dtype=F32, gain=None, hw=None, rope=None, res=None,
           res_scale=1.0, act=None, layer=None, tm=LINEAR_TM, tn=None, name="linear"):
    m, k = x.shape
    n = w.shape[-1] - col0 if n is None else n
    if tn is None:
        tn = LINEAR_TN_WIDE if (n % LINEAR_TN_WIDE == 0 and col0 % LINEAR_TN_WIDE == 0
                                and k * LINEAR_TN_WIDE * 10 <= VMEM_LIMIT // 3 * 2) else LINEAR_TN
    tm, tn = min(tm, m), min(tn, n)
    assert m % tm == 0 and n % tn == 0 and col0 % tn == 0 and w.shape[-2] == k
    cb = col0 // tn
    if layer is None:
        w_spec = pl.BlockSpec((k, tn), lambda j, i: (0, j + cb))
    else:
        w_spec = pl.BlockSpec((None, k, tn), lambda j, i: (layer, 0, j + cb))
    in_specs = [pl.BlockSpec((tm, k), lambda j, i: (i, 0)), w_spec]
    args = [x, w]
    if gain is not None:
        assert tn % hw == 0
        in_specs.append(pl.BlockSpec((1, hw), lambda j, i: (0, 0)))
        args.append(gain.reshape(1, hw).astype(F32))
    if rope is not None:
        cos, sin = rope
        nt = cos.shape[0] // tm
        assert hw == LANES and cos.shape[0] % tm == 0
        in_specs += [pl.BlockSpec((tm, LANES), lambda j, i: (i % nt, 0))] * 2
        args += [cos, sin]
    if res is not None:
        in_specs.append(pl.BlockSpec((tm, tn), lambda j, i: (i, j)))
        args.append(res)
    body = functools.partial(_linear_body, has_gain=gain is not None, has_rope=rope is not None,
                             has_res=res is not None, act=act, hw=hw, res_scale=res_scale)
    return pl.pallas_call(
        body,
        grid=(n // tn, m // tm),
        in_specs=in_specs,
        out_specs=pl.BlockSpec((tm, tn), lambda j, i: (i, j)),
        out_shape=jax.ShapeDtypeStruct((m, n), out_dtype),
        scratch_shapes=[pltpu.VMEM((k, tn), BF16)],
        compiler_params=_params("parallel", "arbitrary"),
        name=name,
    )(*args)


def _rmsnorm_body(x_ref, g_ref, o_ref):
    x = x_ref[...]
    ms = jnp.mean(x * x, axis=-1, keepdims=True)
    o_ref[...] = (x * lax.rsqrt(ms + NORM_EPS) * g_ref[...]).astype(o_ref.dtype)


def rmsnorm(x, g, out_dtype=BF16, tm=512):
    m, d = x.shape
    tm = min(tm, m)
    assert m % tm == 0
    return pl.pallas_call(
        _rmsnorm_body,
        grid=(m // tm,),
        in_specs=[pl.BlockSpec((tm, d), lambda i: (i, 0)), pl.BlockSpec((1, d), lambda i: (0, 0))],
        out_specs=pl.BlockSpec((tm, d), lambda i: (i, 0)),
        out_shape=jax.ShapeDtypeStruct((m, d), out_dtype),
        compiler_params=_params("parallel"),
        name="rmsnorm",
    )(x, g.reshape(1, d))


def _ffn_body(x_ref, xr_ref, g_ref, wg_ref, wu_ref, wd_ref, o_ref, or_ref, xn_ref, *, tm):
    i, f = pl.program_id(0), pl.program_id(1)

    def start(src_ref, rows, dst_ref):
        x = src_ref[...]
        ms = jnp.mean(x * x, axis=-1, keepdims=True)
        xn_ref[rows, :] = (x * lax.rsqrt(ms + NORM_EPS) * g_ref[...]).astype(BF16)
        dst_ref[...] = x

    @pl.when(f == 0)
    def _():
        start(x_ref, slice(0, tm), o_ref)

    @pl.when((f == 0) & (i == 0))
    def _():
        start(xr_ref, slice(tm, tm + xr_ref.shape[0]), or_ref)

    xn = xn_ref[...]
    gate = jnp.dot(xn, wg_ref[...].astype(BF16), preferred_element_type=F32)
    up = jnp.dot(xn, wu_ref[...].astype(BF16), preferred_element_type=F32)
    h = (0.5 * _silu(gate) * up).astype(BF16)
    y = jnp.dot(h, wd_ref[...].astype(BF16), preferred_element_type=F32)
    o_ref[...] += y[:tm]
    or_ref[...] += jnp.where(i == 0, y[tm:], 0.0)


def ffn(x, x_rider, g, w_gate, w_up, w_down, layer, tm=1024, tf=512):
    m, d = x.shape
    mr = x_rider.shape[0]
    mr_pad = -(-mr // BF16_ROWS) * BF16_ROWS
    x_rider = jnp.pad(x_rider, ((0, mr_pad - mr), (0, 0)))
    f = w_gate.shape[-1]
    tm = min(tm, m)
    assert m % tm == 0 and f % tf == 0
    out, out_rider = pl.pallas_call(
        functools.partial(_ffn_body, tm=tm),
        grid=(m // tm, f // tf),
        in_specs=[pl.BlockSpec((tm, d), lambda i, j: (i, 0), pipeline_mode=pl.Buffered(1)),
                  pl.BlockSpec((mr_pad, d), lambda i, j: (0, 0)),
                  pl.BlockSpec((1, d), lambda i, j: (0, 0)),
                  pl.BlockSpec((None, d, tf), lambda i, j: (layer, 0, j)),
                  pl.BlockSpec((None, d, tf), lambda i, j: (layer, 0, j)),
                  pl.BlockSpec((None, tf, d), lambda i, j: (layer, j, 0))],
        out_specs=[pl.BlockSpec((tm, d), lambda i, j: (i, 0)),
                   pl.BlockSpec((mr_pad, d), lambda i, j: (0, 0))],
        out_shape=[jax.ShapeDtypeStruct((m, d), F32), jax.ShapeDtypeStruct((mr_pad, d), F32)],
        scratch_shapes=[pltpu.VMEM((tm + mr_pad, d), BF16)],
        compiler_params=_params("arbitrary", "arbitrary", vmem=FFN_VMEM_LIMIT),
        name="ffn",
    )(x, x_rider, g.reshape(1, d), w_gate, w_up, w_down)
    return out, out_rider[:mr]


def _memattn_body(q_ref, *refs, scale, head_axis):
    each = lambda f, *cols: [f(*args) for args in zip(*cols)]
    lanes = [slice(h * MEM_HD, (h + 1) * MEM_HD) for h in range(MEM_HEADS)]
    k_ref, v_ref, o_ref = refs
    if head_axis:
        ks = [k_ref[0, 0, :, h, :] for h in range(MEM_HEADS)]
        vs = [v_ref[0, 0, :, h, :] for h in range(MEM_HEADS)]
    else:
        ks = [k_ref[0, 0, :, sl] for sl in lanes]
        vs = [v_ref[0, 0, :, sl] for sl in lanes]
    s = each(lambda sl, k: _dot_nt(q_ref[0, :, sl], k) * scale, lanes, ks)
    m = each(lambda t: jnp.max(t, axis=-1, keepdims=True), s)
    p = each(lambda t, mm: jnp.exp(t - mm), s, m)
    l = each(lambda t: jnp.sum(t, axis=-1, keepdims=True), p)
    o = each(lambda t, v, ll: _dot(t, v) / ll, p, vs, l)
    for sl, t in zip(lanes, o):
        o_ref[0, :, sl] = t.astype(o_ref.dtype)


def mem_attention(q, mem_k, mem_v, layer, tq=512):
    b, t, d = q.shape
    n_mem = mem_k.shape[2]
    tq = min(tq, t)
    assert t % tq == 0
    head_axis = mem_k.ndim == 5
    if head_axis:
        kv_specs = [pl.BlockSpec((1, 1, n_mem, MEM_HEADS, MEM_HD), lambda bi, ti: (layer, bi, 0, 0, 0))] * 2
        kv_args = [mem_k, mem_v]
    else:
        kv_specs = [pl.BlockSpec((1, 1, n_mem, d), lambda bi, ti: (layer, bi, 0, 0))] * 2
        kv_args = [mem_k, mem_v]
    return pl.pallas_call(
        functools.partial(_memattn_body, scale=MEM_HD ** -0.5, head_axis=head_axis),
        grid=(b, t // tq),
        in_specs=[pl.BlockSpec((1, tq, d), lambda bi, ti: (bi, ti, 0))] + kv_specs,
        out_specs=pl.BlockSpec((1, tq, d), lambda bi, ti: (bi, ti, 0)),
        out_shape=jax.ShapeDtypeStruct((b, t, d), BF16),
        compiler_params=_params("parallel", "parallel"),
        name="mem_attention",
    )(q, *kv_args)


def rope_tables(pos):
    half = ROPE_DIM // 2
    inv_freq = ROPE_THETA ** (-jnp.arange(half, dtype=F32) / half)
    ang = pos.astype(F32)[:, None] * inv_freq[None, :]
    cos, sin = jnp.cos(ang), jnp.sin(ang)
    rest = DIL_HD - ROPE_DIM
    n = pos.shape[0]
    cos_t = jnp.concatenate([cos, cos, jnp.ones((n, rest), F32)], axis=1)
    sin_t = jnp.concatenate([-sin, sin, jnp.zeros((n, rest), F32)], axis=1)
    return cos_t, sin_t


def _sb_tiles(qs, ks, vs, biases, causal, carries, accs, upper):
    each = lambda f, *cols: [f(*args) for args in zip(*cols)]
    mask = (lambda t: t) if causal is None else (lambda t: jnp.where(causal, t, 0.0))
    z = each(lambda q, k, b: _dot_nt(q, k) * (SB_HD ** -0.5) + b, qs, ks, biases)
    ls = each(_log_sigmoid, z)
    log_keep = each(lambda l, t: mask(l - t), ls, z)
    hi = each(lambda t: t.astype(BF16), log_keep)
    lo = each(lambda t, h: (t - h.astype(F32)).astype(BF16), log_keep, hi)
    local = each(lambda h, l: jnp.dot(jnp.concatenate([h, l], axis=0), upper, preferred_element_type=F32),
                 hi, lo)
    rows = qs[0].shape[0]
    att = each(lambda l, s, c: mask(jnp.exp(l + s[:rows] + s[rows:] + c)), ls, local, carries)
    accs = each(lambda a, p, v: a + _dot(p, v), accs, att, vs)
    carries = each(lambda c, t: c + jnp.sum(t, axis=-1, keepdims=True), carries, log_keep)
    return carries, accs


def _sb_tile(q, k, v, bias, causal, carry, acc, upper):
    carries, accs = _sb_tiles([q], [k], [v], [bias], causal, [carry], [acc], upper)
    return carries[0], accs[0]


def _upper_ones(n):
    return jnp.where(_iota((n, n), 0) > _iota((n, n), 1), 1.0, 0.0).astype(BF16)


def _sb_prompt_body(bias_ref, q_ref, k_ref, v_ref, o_ref, *, tq, heads):
    hb, qi = pl.program_id(1), pl.program_id(2)
    upper = _upper_ones(tq)
    lanes = [slice(u * SB_HD, (u + 1) * SB_HD) for u in range(heads)]
    qs = [q_ref[0, :, sl] for sl in lanes]
    biases = [bias_ref[hb * heads + u] for u in range(heads)]

    def tiles(j, state, causal):
        start = pl.multiple_of(j * tq, tq)
        ks = [k_ref[0, pl.ds(start, tq), sl] for sl in lanes]
        vs = [v_ref[0, pl.ds(start, tq), sl] for sl in lanes]
        return _sb_tiles(qs, ks, vs, biases, causal, state[0], state[1], upper)

    init = ([jnp.zeros((tq, 1), F32) for _ in lanes], [jnp.zeros((tq, SB_HD), F32) for _ in lanes])
    state = tiles(qi, init, _iota((tq, tq), 1) < _iota((tq, tq), 0))
    _, accs = lax.fori_loop(0, qi, lambda jj, st: tiles(qi - 1 - jj, st, None), state)
    for sl, acc in zip(lanes, accs):
        o_ref[0, :, sl] = acc.astype(o_ref.dtype)


def sb_attention_prompt(q, k, v, bias, tq=256, heads=4):
    b, t, d = q.shape
    tq = min(tq, t)
    w = heads * SB_HD
    assert t % tq == 0 and d % w == 0
    kv_spec = pl.BlockSpec((1, t, w), lambda bi, h, qi: (bi, 0, h))
    io_spec = pl.BlockSpec((1, tq, w), lambda bi, h, qi: (bi, qi, h))
    return pl.pallas_call(
        functools.partial(_sb_prompt_body, tq=tq, heads=heads),
        grid=(b, d // w, t // tq),
        in_specs=[pl.BlockSpec(memory_space=pltpu.SMEM), io_spec, kv_spec, kv_spec],
        out_specs=io_spec,
        out_shape=jax.ShapeDtypeStruct((b, t, d), BF16),
        compiler_params=_params("parallel", "parallel", "arbitrary"),
        name="sb_attention_prompt",
    )(bias, q, k, v)


def _sb_decode_body(table_ref, bias_ref, q_ref, kn_ref, vn_ref, *rest, n_pages, per_step):
    del table_ref
    page_refs = rest[:2 * per_step]
    o_ref, acc_ref, carry_ref = rest[2 * per_step:]
    j = pl.program_id(1)
    nh, cols = SB_HEADS, PAGE_SIZE * SB_HEADS
    n_sub = cols // LANES
    bias = bias_ref[...]
    past_len = n_pages * PAGE_SIZE
    scale = SB_HD ** -0.5
    q = q_ref[0]

    @pl.when(j == 0)
    def _():
        z = jnp.sum(q * kn_ref[0], axis=-1, keepdims=True) * scale + bias
        causal = jnp.full((nh, 1), past_len, jnp.int32) < past_len
        ls = _log_sigmoid(z)
        carry_ref[...] = jnp.where(causal, ls - z, 0.0)
        acc_ref[...] = jnp.where(causal, jnp.exp(ls), 0.0) * vn_ref[0]

    each = lambda f, *c: [f(*args) for args in zip(*c)]
    col = _iota((nh, cols), 1)
    own = (col & (nh - 1)) == _iota((nh, cols), 0)
    upper = _upper_ones(LANES)
    pages = [n_pages - 1 - (j * per_step + u) for u in range(per_step)]
    masks = [own & ((p * PAGE_SIZE + col // nh) < past_len) for p in pages]
    ks = [page_refs[2 * u][...] for u in range(per_step)]
    vs = [page_refs[2 * u + 1][...] for u in range(per_step)]
    z = each(lambda k: _dot_nt(q, k) * scale + bias, ks)
    ls = each(_log_sigmoid, z)
    log_keep = each(lambda l, t, m: jnp.where(m, l - t, 0.0), ls, z, masks)

    def local_suffix(t):
        x = jnp.concatenate([t[:, c * LANES:(c + 1) * LANES] for c in range(n_sub)], axis=0)
        hi = x.astype(BF16)
        lo = (x - hi.astype(F32)).astype(BF16)
        y = jnp.dot(jnp.concatenate([hi, lo], axis=0), upper, preferred_element_type=F32)
        y = y[:n_sub * nh] + y[n_sub * nh:]
        return [y[c * nh:(c + 1) * nh] for c in range(n_sub)]

    local = each(local_suffix, log_keep)
    carry = carry_ref[...]
    between = []
    for t, loc in zip(log_keep, local):
        pieces = [None] * n_sub
        for c in reversed(range(n_sub)):
            pieces[c] = loc[c] + carry
            carry = carry + jnp.sum(t[:, c * LANES:(c + 1) * LANES], axis=-1, keepdims=True)
        between.append(jnp.concatenate(pieces, axis=1))
    carry_ref[...] = carry
    att = each(lambda l, s, m: jnp.where(m, jnp.exp(l + s), 0.0), ls, between, masks)
    acc_ref[...] += sum(each(_dot, att, vs))

    @pl.when(j == pl.num_programs(1) - 1)
    def _():
        o_ref[0] = acc_ref[...]


def sb_attention_decode(q, k_new, v_new, pool_k, pool_v, table, bias, per_step=2):
    b = q.shape[0]
    n_pages = table.shape[1]
    assert n_pages % per_step == 0 and pool_k.shape[1:] == (PAGE_SIZE, SB_HEADS, SB_HD)
    assert SB_HEADS & (SB_HEADS - 1) == 0 and (PAGE_SIZE * SB_HEADS) % LANES == 0
    page_rows = PAGE_SIZE * SB_HEADS
    pool_k = pool_k.reshape(-1, SB_HD)
    pool_v = pool_v.reshape(-1, SB_HD)
    row_spec = pl.BlockSpec((1, SB_HEADS, SB_HD), lambda bi, j, tab: (bi, 0, 0))

    def page_spec(u):
        return pl.BlockSpec((page_rows, SB_HD),
                            lambda bi, j, tab: (tab[bi, n_pages - 1 - (j * per_step + u)], 0))

    pages = [page_spec(u) for u in range(per_step) for _ in range(2)]
    grid_spec = pltpu.PrefetchScalarGridSpec(
        num_scalar_prefetch=1,
        grid=(b, n_pages // per_step),
        in_specs=[pl.BlockSpec((SB_HEADS, 1), lambda bi, j, tab: (0, 0)),
                  row_spec, row_spec, row_spec] + pages,
        out_specs=row_spec,
        scratch_shapes=[pltpu.VMEM((SB_HEADS, SB_HD), F32), pltpu.VMEM((SB_HEADS, 1), F32)],
    )
    return pl.pallas_call(
        functools.partial(_sb_decode_body, n_pages=n_pages, per_step=per_step),
        grid_spec=grid_spec,
        out_shape=jax.ShapeDtypeStruct((b, SB_HEADS, SB_HD), F32),
        compiler_params=_params("parallel", "arbitrary"),
        name="sb_attention_decode",
    )(table, bias.reshape(SB_HEADS, 1), q, k_new, v_new, *([pool_k, pool_v] * per_step))


def _dil_prompt_body(*refs, t, blk):
    n_grp = len(DIL_PATTERNS)
    q_refs, k_refs, v_refs = refs[:n_grp], refs[n_grp:2 * n_grp], refs[2 * n_grp:3 * n_grp]
    o_ref, m_s, l_s, acc_s = refs[3 * n_grp:3 * n_grp + 4]
    scale = DIL_HD ** -0.5
    order = sorted(range(n_grp), key=lambda g: -DIL_PATTERNS[g][1])
    for n, g in enumerate(order):
        win, dil = DIL_PATTERNS[g]
        band, cls_len = win // dil, t // dil
        n_blk = cls_len // blk
        n_keys = min(2 * blk, cls_len)
        first, last = n == 0, n == n_grp - 1
        assert band == blk and (not last or dil == 1) and n_blk * blk == cls_len
        q_ref, k_ref, v_ref = q_refs[g], k_refs[g], v_refs[g]
        rel = _iota((blk, n_keys), 0) - _iota((blk, n_keys), 1)

        def rows(start, size, dil=dil):
            return pl.ds(start, size, stride=dil) if dil > 1 else pl.ds(start, size)

        def blocks(i, _, q_ref=q_ref, k_ref=k_ref, v_ref=v_ref, dil=dil, n_blk=n_blk, n_keys=n_keys,
                   rel=rel, rows=rows, first=first, last=last, band=band):
            each = lambda f, *cols: [f(*args) for args in zip(*cols)]
            idxs = [i * DIL_BLOCKS_PER_ITER + u for u in range(DIL_BLOCKS_PER_ITER)]
            lbs = [idx % n_blk for idx in idxs]
            k_cls = [jnp.maximum(lb - 1, 0) * blk for lb in lbs]
            if dil == 1:
                q_tok = [pl.multiple_of(lb * blk, blk) for lb in lbs]
                k_tok = [pl.multiple_of(kc, blk) for kc in k_cls]
            else:
                q_tok = [idx // n_blk + lb * blk * dil for idx, lb in zip(idxs, lbs)]
                k_tok = [idx // n_blk + kc * dil for idx, kc in zip(idxs, k_cls)]
            sel = [rows(t, blk) for t in q_tok]
            q = [q_ref[0, s, :] for s in sel]
            k = [k_ref[0, rows(t, n_keys), :] for t in k_tok]
            v = [v_ref[0, rows(t, n_keys), :] for t in k_tok]
            old = None if first else [(m_s[s, :], l_s[s, :], acc_s[s, :]) for s in sel]

            def scores(qq, kk, lb, kc):
                dist = rel + (lb * blk - kc)
                valid = jnp.where(dist >= 0, dist, band + 1) <= band
                return jnp.where(valid, _dot_nt(qq, kk) * scale, NEG_BIG)

            s = each(scores, q, k, lbs, k_cls)
            m = each(lambda t: jnp.max(t, axis=-1, keepdims=True), s)
            p = each(lambda t, mm: jnp.exp(t - mm), s, m)
            l = each(lambda t: jnp.sum(t, axis=-1, keepdims=True), p)
            acc = each(_dot, p, v)
            if not first:
                m_new = each(lambda o, mm: jnp.maximum(o[0], mm), old, m)
                e0 = each(lambda o, mn: jnp.exp(o[0] - mn), old, m_new)
                e1 = each(lambda mm, mn: jnp.exp(mm - mn), m, m_new)
                l = each(lambda o, a, b, t: a * o[1] + b * t, old, e0, e1, l)
                acc = each(lambda o, a, b, t: a * o[2] + b * t, old, e0, e1, acc)
                m = m_new
            for u, sl in enumerate(sel):
                if last:
                    o_ref[0, sl, :] = (acc[u] / l[u]).astype(o_ref.dtype)
                else:
                    m_s[sl, :], l_s[sl, :], acc_s[sl, :] = m[u], l[u], acc[u]
            return 0

        assert (dil * n_blk) % DIL_BLOCKS_PER_ITER == 0
        lax.fori_loop(0, dil * n_blk // DIL_BLOCKS_PER_ITER, blocks, 0)


def dil_attention_prompt(q, k, v, blk=128):
    b, t, _ = q.shape
    n_grp = len(DIL_PATTERNS)
    assert t % blk == 0

    def spec(g):
        return pl.BlockSpec((1, t, DIL_HD), lambda bi, h: (bi, 0, g * DIL_HEADS + h))

    return pl.pallas_call(
        functools.partial(_dil_prompt_body, t=t, blk=blk),
        grid=(b, DIL_HEADS),
        in_specs=[spec(g) for g in range(n_grp)] * 3,
        out_specs=pl.BlockSpec((1, t, DIL_HD), lambda bi, h: (bi, 0, h)),
        out_shape=jax.ShapeDtypeStruct((b, t, DIL_HEADS * DIL_HD), BF16),
        scratch_shapes=[pltpu.VMEM((t, 1), F32), pltpu.VMEM((t, 1), F32), pltpu.VMEM((t, DIL_HD), F32)],
        compiler_params=_params("parallel", "parallel"),
        name="dil_attention_prompt",
    )(*([q] * n_grp + [k] * n_grp + [v] * n_grp))


def _dil_decode_body(*refs):
    n_grp = len(DIL_PATTERNS)
    q_ref, kn_ref, vn_ref = refs[:3]
    kb_refs, vb_refs = refs[3:3 + n_grp], refs[3 + n_grp:3 + 2 * n_grp]
    o_ref = refs[3 + 2 * n_grp]
    scale = DIL_HD ** -0.5
    nh = DIL_HEADS
    scores, news, masks = [], [], []
    for g in range(n_grp):
        q = q_ref[0, g]
        band = kb_refs[g].shape[1]
        k2 = kb_refs[g][0].reshape(band * nh, DIL_HD)
        own = (_iota((nh, band * nh), 1) & (nh - 1)) == _iota((nh, band * nh), 0)
        masks.append(own)
        scores.append(jnp.where(own, _dot_nt(q, k2) * scale, NEG_BIG))
        news.append(jnp.sum(q * kn_ref[0, g], axis=-1, keepdims=True) * scale)
    m = functools.reduce(jnp.maximum, [jnp.max(s, axis=-1, keepdims=True) for s in scores] + news)
    l = jnp.zeros((nh, 1), F32)
    acc = jnp.zeros((nh, DIL_HD), F32)
    for g in range(n_grp):
        p = jnp.where(masks[g], jnp.exp(scores[g] - m), 0.0)
        pn = jnp.exp(news[g] - m)
        v2 = vb_refs[g][0].reshape(p.shape[1], DIL_HD)
        l = l + jnp.sum(p, axis=-1, keepdims=True) + pn
        acc = acc + _dot(p, v2) + pn * vn_ref[0, g]
    o_ref[0] = acc / l


def dil_attention_decode(q, k_new, v_new, bufs):
    b = q.shape[0]
    n_grp = len(DIL_PATTERNS)
    assert DIL_HEADS & (DIL_HEADS - 1) == 0
    row_spec = pl.BlockSpec((1, n_grp, DIL_HEADS, DIL_HD), lambda bi: (bi, 0, 0, 0))
    views, specs = [], []
    for which in range(2):
        for g, (win, dil) in enumerate(DIL_PATTERNS):
            buf = bufs[2 * g + which]
            assert buf.shape[1] == win and win % dil == 0
            views.append(buf.reshape(b, win // dil, dil * DIL_HEADS, DIL_HD))
            specs.append(pl.BlockSpec((1, win // dil, DIL_HEADS, DIL_HD), lambda bi: (bi, 0, 0, 0)))
    return pl.pallas_call(
        _dil_decode_body,
        grid=(b,),
        in_specs=[row_spec] * 3 + specs,
        out_specs=pl.BlockSpec((1, DIL_HEADS, DIL_HD), lambda bi: (bi, 0, 0)),
        out_shape=jax.ShapeDtypeStruct((b, DIL_HEADS, DIL_HD), F32),
        compiler_params=_params("parallel"),
        name="dil_attention_decode",
    )(q, k_new, v_new, *views)


def _rwkv_mix_body(x_ref, g_ref, shift_ref, mu_ref, *refs, tt):
    o_refs, xn_ref, buf_ref = refs[:-2], refs[-2], refs[-1]
    halo = 8

    @pl.when(pl.program_id(1) == 0)
    def _():
        buf_ref[0:halo, :] = jnp.broadcast_to(shift_ref[0], (halo, shift_ref.shape[-1]))

    x = x_ref[0]
    inv = lax.rsqrt(jnp.mean(x * x, axis=-1, keepdims=True) + NORM_EPS)
    width = 2 * LANES
    for c in range(x.shape[1] // width):
        sl = slice(c * width, (c + 1) * width)
        xn = x_ref[0, :, sl] * inv * g_ref[:, sl]
        buf_ref[halo:halo + tt, sl] = xn
        xx = buf_ref[halo - 1:halo - 1 + tt, sl] - xn
        for i, o_ref in enumerate(o_refs):
            o_ref[0, :, sl] = (xn + xx * mu_ref[i:i + 1, sl]).astype(o_ref.dtype)
        xn_ref[0, :, sl] = xn
    buf_ref[0:halo, :] = buf_ref[tt:tt + halo, :]


def rwkv_mix(x, g, shift0, mu, tt=128):
    b, t, d = x.shape
    tt = math.gcd(tt, t)
    assert tt % 8 == 0
    n_mix = mu.shape[0]
    seq = pl.BlockSpec((1, tt, d), lambda bi, ti: (bi, ti, 0))
    outs = pl.pallas_call(
        functools.partial(_rwkv_mix_body, tt=tt),
        grid=(b, t // tt),
        in_specs=[seq, pl.BlockSpec((1, d), lambda bi, ti: (0, 0)),
                  pl.BlockSpec((1, 1, d), lambda bi, ti: (bi, 0, 0)),
                  pl.BlockSpec((n_mix, d), lambda bi, ti: (0, 0))],
        out_specs=[seq] * n_mix + [pl.BlockSpec((1, tt, d), lambda bi, ti: (bi, 0, 0))],
        out_shape=[jax.ShapeDtypeStruct((b, t, d), BF16)] * n_mix + [jax.ShapeDtypeStruct((b, tt, d), F32)],
        scratch_shapes=[pltpu.VMEM((tt + 8, d), F32)],
        compiler_params=_params("parallel", "arbitrary"),
        name="rwkv_mix",
    )(x, g.reshape(1, d), shift0.astype(F32).reshape(b, 1, d), mu)
    return outs[:n_mix], outs[n_mix]


def _rwkv_scan_body(r_ref, k_ref, v_ref, wl_ref, al_ref, g_ref, prm_ref, s0_ref, y_ref, sf_ref,
                    st_ref, *, t_valid, t_total):
    c = pl.program_id(1)
    C, hd = RWKV_CHUNK, RWKV_HD
    n2 = 2 * C
    grp = RWKV_PAIRS_PER_DOT
    rows = grp * n2

    @pl.when(c == 0)
    def _():
        st_ref[...] = s0_ref[0]

    left = _iota((C, LANES), 1) < hd
    ri, ci = _iota((rows, rows), 0), _iota((rows, rows), 1)
    strict = (ci & (C - 1)) < (ri & (C - 1))
    incl = (ci & (C - 1)) <= (ri & (C - 1))
    tri = jnp.where(_iota((C, C), 1) <= _iota((C, C), 0), 1.0, 0.0).astype(BF16)
    same_head = (_iota((LANES, LANES), 0) // hd) == (_iota((LANES, LANES), 1) // hd)
    seg_ones = jnp.where(same_head, 1.0, 0.0).astype(BF16)
    masked = t_valid < t_total
    rows_valid = (c * C + _iota((C, LANES), 0)) < t_valid
    zero_slab = jnp.zeros((n2, LANES), F32)

    def seg_sum(x):
        hi = x.astype(BF16)
        lo = (x - hi.astype(F32)).astype(BF16)
        y = jnp.dot(jnp.concatenate([hi, lo], axis=0), seg_ones, preferred_element_type=F32)
        return y[:x.shape[0]] + y[x.shape[0]:]

    def cumsum_steps(x):
        w = x.shape[1]
        y = jnp.dot(tri, jnp.concatenate(_split3(x), axis=1), preferred_element_type=F32)
        return y[:, :w] + y[:, w:2 * w] + y[:, 2 * w:]

    def stack(x):
        return jnp.concatenate([jnp.where(left, x, 0.0), jnp.where(left, 0.0, x)], axis=0)

    def spread(slabs):
        return jnp.concatenate(
            [jnp.concatenate([s if q == j else zero_slab for j in range(grp)], axis=1)
             for q, s in enumerate(slabs)], axis=0)

    def core(groups):
        each = lambda f, *cols: [f(*args) for args in zip(*cols)]
        dss, stss = [g[0] for g in groups], [g[1] for g in groups]
        flat_d = [d for ds in dss for d in ds]
        flat_st = [st for sts in stss for st in sts]
        ar = each(lambda d: jnp.concatenate([stack(d['a_t']), stack(d['r_t'])], axis=0).astype(BF16), flat_d)
        bk = each(lambda d: jnp.concatenate([stack(d['b_t']), stack(d['k_t'])], axis=0).astype(BF16), flat_d)
        gram_p = each(_dot_nt, ar, bk)
        a_s_p = each(_dot_nt, ar, flat_st)
        by_group = lambda xs: [xs[q:q + grp] for q in range(0, len(xs), grp)]
        quadrant = lambda gs, r0, c0: spread([g[r0:r0 + n2, c0:c0 + n2] for g in gs])
        vv = each(lambda ds: jnp.concatenate([d['v'] for d in ds for _ in range(2)], axis=0), dss)
        y = each(lambda gs, v, ss: _dot(jnp.where(strict, quadrant(gs, 0, n2), 0.0), v)
                 + jnp.concatenate([s[:n2] for s in ss], axis=0),
                 by_group(gram_p), vv, by_group(a_s_p))
        pw = each(lambda gs: jnp.where(strict, quadrant(gs, 0, 0), 0.0), by_group(gram_p))
        n = 1
        while n < C:
            y = each(lambda p, t: t + _dot(p, t), pw, y)
            n *= 2
            if n < C:
                pw = each(lambda p: _dot(p, p), pw)
        o2 = each(lambda gs, t, v, ss: _dot(
            jnp.concatenate([jnp.where(incl, quadrant(gs, n2, 0), 0.0),
                             jnp.where(incl, quadrant(gs, n2, n2), 0.0)], axis=1),
            jnp.concatenate([t, v], axis=0)) + jnp.concatenate([s[n2:] for s in ss], axis=0),
            by_group(gram_p), y, vv, by_group(a_s_p))
        pick = lambda t, q: jnp.where(left, t[q * n2:q * n2 + C], t[q * n2 + C:(q + 1) * n2])
        upd = each(lambda ds, t: _dot(
            jnp.concatenate([jnp.concatenate([pick(t, q), d['v']], axis=0)
                             for q, d in enumerate(ds)], axis=1).T,
            jnp.concatenate([jnp.concatenate([d['b_end'], d['k_end']], axis=0) for d in ds], axis=1)),
            dss, y)
        os, s_new = [], []
        for ds, sts, o, up in zip(dss, stss, o2, upd):
            os += [pick(o, q) for q in range(grp)]
            s_new += [jnp.where(same_head, st * d['decay_end']
                                + up[q * LANES:(q + 1) * LANES, q * LANES:(q + 1) * LANES], 0.0)
                      for q, (d, st) in enumerate(zip(ds, sts))]
        return os, s_new

    def pairs(i, _):
        ps = [i * RWKV_PAIRS_PER_ITER + u for u in range(RWKV_PAIRS_PER_ITER)]
        offs = [pl.multiple_of(p * LANES, LANES) for p in ps]
        sts = [st_ref[p] for p in ps]
        ds = []
        for off in offs:
            r, k, v, wl, al, g = [ref[0, :, pl.ds(off, LANES)]
                                  for ref in (r_ref, k_ref, v_ref, wl_ref, al_ref, g_ref)]
            prm = prm_ref[:, pl.ds(off, LANES)]
            w0, a0, k_k, k_a, r_k, gn_w, gn_b = [prm[j:j + 1] for j in range(7)]
            a = jax.nn.sigmoid(a0 + al)
            ds.append(dict(r=r, v=v, g=g, a=a, gn_w=gn_w, gn_b=gn_b, r_k=r_k, kkr=k * k_k,
                           lw=-jnp.exp(-_softplus(-(w0 + wl)) - 0.5),
                           kmod=k * (1.0 + (a - 1.0) * k_a)))
        sums = seg_sum(jnp.concatenate([d['kkr'] * d['kkr'] for d in ds]
                                       + [d['r'] * d['kmod'] * d['r_k'] for d in ds], axis=0))
        n_p = len(ds)
        for u, d in enumerate(ds):
            d['kk'] = d['kkr'] * lax.rsqrt(jnp.maximum(sums[u * C:(u + 1) * C], 1e-24))
            d['bonus'] = sums[(n_p + u) * C:(n_p + u + 1) * C] * d['v']
            if masked:
                for name in ('lw', 'kk', 'kmod', 'v'):
                    d[name] = jnp.where(rows_valid, d[name], 0.0)
        lc_all = cumsum_steps(jnp.concatenate([d['lw'] for d in ds], axis=1))
        for u, d in enumerate(ds):
            lc = lc_all[:, u * LANES:(u + 1) * LANES]
            lend = lc[C - 1:C, :]
            kka = d['kk'] * d['a']
            e_inv, e_end = jnp.exp(-lc), jnp.exp(lend - lc)
            d.update(a_t=-d['kk'] * jnp.exp(lc - d['lw']), r_t=d['r'] * jnp.exp(lc),
                     b_t=kka * e_inv, k_t=d['kmod'] * e_inv, b_end=kka * e_end,
                     k_end=d['kmod'] * e_end, decay_end=jnp.exp(lend))
        os, s_news = core([(ds[q:q + grp], sts[q:q + grp]) for q in range(0, n_p, grp)])
        stats = jnp.dot(jnp.concatenate(os + [o * o for o in os], axis=0).astype(BF16), seg_ones,
                        preferred_element_type=F32)
        for u, (p, off, d, o) in enumerate(zip(ps, offs, ds, os)):
            mean = stats[u * C:(u + 1) * C] * (1.0 / hd)
            var = stats[(n_p + u) * C:(n_p + u + 1) * C] * (1.0 / hd) - mean * mean
            on = (o - mean) * lax.rsqrt(var + GN_EPS) * d['gn_w'] + d['gn_b']
            st_ref[p] = s_news[u]
            y_ref[0, :, pl.ds(off, LANES)] = ((on + d['bonus']) * d['g'])[:y_ref.shape[1]].astype(y_ref.dtype)
        return 0

    lax.fori_loop(0, D_MODEL // LANES // RWKV_PAIRS_PER_ITER, pairs, 0)

    @pl.when(c == pl.num_programs(1) - 1)
    def _():
        sf_ref[0] = st_ref[...]


def rwkv_scan(r, k, v, wl, al, g, prm, s0, t_valid):
    b, t, d = r.shape
    C = RWKV_CHUNK
    assert t % C == 0 and d == D_MODEL
    seq = pl.BlockSpec((1, C, d), lambda bi, c: (bi, c, 0))
    st_spec = pl.BlockSpec((1, d // LANES, LANES, LANES), lambda bi, c: (bi, 0, 0, 0))
    y_rows = BF16_ROWS if (t == C and t_valid <= BF16_ROWS) else C
    return pl.pallas_call(
        functools.partial(_rwkv_scan_body, t_valid=t_valid, t_total=t),
        grid=(b, t // C),
        in_specs=[seq] * 6 + [pl.BlockSpec((8, d), lambda bi, c: (0, 0)), st_spec],
        out_specs=[pl.BlockSpec((1, y_rows, d), lambda bi, c: (bi, c, 0)), st_spec],
        out_shape=[jax.ShapeDtypeStruct((b, t // C * y_rows, d), BF16),
                   jax.ShapeDtypeStruct((b, d // LANES, LANES, LANES), F32)],
        scratch_shapes=[pltpu.VMEM((d // LANES, LANES, LANES), F32)],
        compiler_params=_params("parallel", "arbitrary"),
        name="rwkv_scan",
    )(r, k, v, wl, al, g, prm, s0)


def _conv_body(x_ref, prev_ref, w_ref, b_ref, o_ref, buf_ref, *, tt):
    halo = 8

    @pl.when(pl.program_id(2) == 0)
    def _():
        buf_ref[0:halo, :] = prev_ref[0]

    buf_ref[halo:halo + tt, :] = x_ref[0]
    for c in range(x_ref.shape[2] // LANES):
        sl = slice(c * LANES, (c + 1) * LANES)
        acc = b_ref[:, sl] + x_ref[0, :, sl] * w_ref[SSM_CONV - 1:SSM_CONV, sl]
        for back in range(1, SSM_CONV):
            tap = SSM_CONV - 1 - back
            acc = acc + buf_ref[halo - back:halo - back + tt, sl] * w_ref[tap:tap + 1, sl]
        o_ref[0, :, sl] = _silu(acc)
    buf_ref[0:halo, :] = buf_ref[tt:tt + halo, :]


def causal_conv_silu(x, prev8, w, bias, tt=256, tc=1024):
    b, t, ch = x.shape
    tt = min(tt, t)
    assert t % tt == 0 and ch % tc == 0
    return pl.pallas_call(
        functools.partial(_conv_body, tt=tt),
        grid=(b, ch // tc, t // tt),
        in_specs=[pl.BlockSpec((1, tt, tc), lambda bi, ci, ti: (bi, ti, ci)),
                  pl.BlockSpec((1, 8, tc), lambda bi, ci, ti: (bi, 0, ci)),
                  pl.BlockSpec((SSM_CONV, tc), lambda bi, ci, ti: (0, ci)),
                  pl.BlockSpec((1, tc), lambda bi, ci, ti: (0, ci))],
        out_specs=pl.BlockSpec((1, tt, tc), lambda bi, ci, ti: (bi, ti, ci)),
        out_shape=jax.ShapeDtypeStruct((b, t, ch), F32),
        scratch_shapes=[pltpu.VMEM((tt + 8, tc), F32)],
        compiler_params=_params("parallel", "parallel", "arbitrary"),
        name="causal_conv_silu",
    )(x, prev8, w, bias.reshape(1, ch))


def _ssd_body(xa_ref, z_ref, dt_ref, dtt_ref, dtb_ref, dtbt_ref, al_ref, alt_ref, dsk_ref, nw_ref,
              h0_ref, y_ref, hf_ref, h_ref, *, t_valid, t_total):
    c = pl.program_id(1)
    Q, P = SSM_CHUNK, SSM_HEADDIM
    per_grp = SSM_HEADS // SSM_GROUPS
    gw = per_grp * P

    @pl.when(c == 0)
    def _():
        h_ref[...] = h0_ref[0]

    tri = _iota((Q, Q), 1) <= _iota((Q, Q), 0)
    tri_b = jnp.where(tri, 1.0, 0.0).astype(BF16)
    upp_b = jnp.where(_iota((Q, Q), 0) <= _iota((Q, Q), 1), 1.0, 0.0).astype(BF16)
    left = _iota((Q, LANES), 1) < P
    masked = t_valid < t_total

    def one_hot_rows(width, per):
        head = _iota((3 * per_grp, width), 0) % per_grp
        return jnp.where(_iota((3 * per_grp, width), 1) // per == head, 1.0, 0.0).astype(BF16)

    to_heads = one_hot_rows(gw, P)
    to_slabs = one_hot_rows(per_grp * Q, Q)

    def group(g, _):
        dt = _softplus(dt_ref[0, g] + dtb_ref[g])
        dtt = _softplus(dtt_ref[0, g] + dtbt_ref[g])
        if masked:
            dt = jnp.where(c * Q + _iota(dt.shape, 0) < t_valid, dt, 0.0)
            dtt = jnp.where(c * Q + _iota(dtt.shape, 1) < t_valid, dtt, 0.0)
        cum = _dot_exact_lhs(tri_b, dt * -jnp.exp(al_ref[g]))
        cumt = _dot_exact_rhs(dtt * -jnp.exp(alt_ref[g]), upp_b)
        spread_cols = lambda t, e3: jnp.dot(jnp.concatenate(_split3(t), axis=1), e3,
                                            preferred_element_type=F32)
        ecum_w = spread_cols(jnp.exp(cum), to_heads)
        tail_w = spread_cols(jnp.exp(cum[Q - 1:Q, :] - cum) * dt, to_heads)
        cum_w = spread_cols(cum, to_slabs)
        boff = pl.multiple_of(SSM_DINNER + g * SSM_STATE, SSM_STATE)
        coff = pl.multiple_of(SSM_DINNER + SSM_GROUPS * SSM_STATE + g * SSM_STATE, SSM_STATE)
        roff = pl.multiple_of(g * gw, gw)
        bm = xa_ref[0, :, pl.ds(boff, SSM_STATE)]
        cm = xa_ref[0, :, pl.ds(coff, SSM_STATE)]
        cb = _dot_nt(cm, bm)
        hg = h_ref[pl.ds(roff, gw), :]
        y_state = _dot_nt(cm, hg)
        xg = xa_ref[0, :, pl.ds(roff, gw)]
        xps = [xg[:, q * LANES:(q + 1) * LANES] for q in range(gw // LANES)]

        def head_matrix(hh):
            seg = cum_w[:, hh * Q:(hh + 1) * Q] - cumt[hh:hh + 1, :]
            dec = jnp.where(tri, jnp.exp(jnp.where(tri, seg, 0.0)), 0.0)
            return (cb * dec * dtt[hh:hh + 1, :]).astype(BF16)

        mats = [head_matrix(hh) for hh in range(per_grp)]
        prods = [_dot(mats[hh], xps[hh // 2]) for hh in range(per_grp)]
        ys, xts = [], []
        for q, xp in enumerate(xps):
            sl = slice(q * LANES, (q + 1) * LANES)
            ys.append(jnp.where(left, prods[2 * q], prods[2 * q + 1]) + y_state[:, sl] * ecum_w[:, sl])
            xts.append(xp * tail_w[:, sl])
        upd = _dot(jnp.concatenate(xts, axis=1).T, bm)
        for hh in range(per_grp):
            rows = slice(hh * P, (hh + 1) * P)
            h_ref[pl.ds(pl.multiple_of(roff + hh * P, P), P), :] = (
                hg[rows] * jnp.exp(cumt[hh:hh + 1, Q - 1:Q]) + upd[rows])
        yg = jnp.concatenate(ys, axis=1)
        yg = (yg + xg * dsk_ref[:, pl.ds(roff, gw)]) * _silu(z_ref[0, :, pl.ds(roff, gw)])
        ms = jnp.mean(yg * yg, axis=-1, keepdims=True)
        y_ref[0, :, pl.ds(roff, gw)] = (
            yg * lax.rsqrt(ms + SSM_NORM_EPS) * nw_ref[:, pl.ds(roff, gw)]).astype(y_ref.dtype)
        return 0

    lax.fori_loop(0, SSM_GROUPS, group, 0)

    @pl.when(c == pl.num_programs(1) - 1)
    def _():
        hf_ref[0] = h_ref[...]


def ssd_scan(xa, z, dt_raw, dt_bias, a_log, d_skip, norm_w, h0, t_valid):
    b, t, _ = xa.shape
    Q = SSM_CHUNK
    per_grp = SSM_HEADS // SSM_GROUPS
    assert t % Q == 0
    dt_g = dt_raw.reshape(b, t, SSM_GROUPS, per_grp).transpose(0, 2, 1, 3)
    dt_gt = dt_g.transpose(0, 1, 3, 2)
    grp = lambda p: p.reshape(SSM_GROUPS, 1, per_grp)
    grp_t = lambda p: p.reshape(SSM_GROUPS, per_grp, 1)
    full3 = lambda s: pl.BlockSpec(s, lambda bi, c: (0, 0, 0))
    lanes = lambda w: pl.BlockSpec((1, w), lambda bi, c: (0, 0))
    st_spec = pl.BlockSpec((1, SSM_DINNER, SSM_STATE), lambda bi, c: (bi, 0, 0))
    return pl.pallas_call(
        functools.partial(_ssd_body, t_valid=t_valid, t_total=t),
        grid=(b, t // Q),
        in_specs=[pl.BlockSpec((1, Q, SSM_CONV_DIM), lambda bi, c: (bi, c, 0)),
                  pl.BlockSpec((1, Q, SSM_DINNER), lambda bi, c: (bi, c, 0)),
                  pl.BlockSpec((1, SSM_GROUPS, Q, per_grp), lambda bi, c: (bi, 0, c, 0)),
                  pl.BlockSpec((1, SSM_GROUPS, per_grp, Q), lambda bi, c: (bi, 0, 0, c)),
                  full3((SSM_GROUPS, 1, per_grp)), full3((SSM_GROUPS, per_grp, 1)),
                  full3((SSM_GROUPS, 1, per_grp)), full3((SSM_GROUPS, per_grp, 1)),
                  lanes(SSM_DINNER), lanes(SSM_DINNER), st_spec],
        out_specs=[pl.BlockSpec((1, Q, SSM_DINNER), lambda bi, c: (bi, c, 0)), st_spec],
        out_shape=[jax.ShapeDtypeStruct((b, t, SSM_DINNER), BF16),
                   jax.ShapeDtypeStruct((b, SSM_DINNER, SSM_STATE), F32)],
        scratch_shapes=[pltpu.VMEM((SSM_DINNER, SSM_STATE), F32)],
        compiler_params=_params("parallel", "arbitrary"),
        name="ssd_scan",
    )(xa, z, dt_g, dt_gt, grp(dt_bias), grp_t(dt_bias), grp(a_log), grp_t(a_log),
      jnp.repeat(d_skip, SSM_HEADDIM).reshape(1, SSM_DINNER), norm_w.reshape(1, SSM_DINNER), h0)


def _pad_time(x3, mult):
    t = x3.shape[1]
    tp = -(-t // mult) * mult
    return x3 if tp == t else jnp.pad(x3, ((0, 0), (0, tp - t), (0, 0)))


def _rwkv_mixer(xf, b, t, ln, shift0, wkv0, W):
    d = D_MODEL
    x_p = _pad_time(xf.reshape(b, t, d), RWKV_CHUNK)
    tp = x_p.shape[1]
    mixes, xn_tail = rwkv_mix(x_p, ln, shift0, W['rwkv_mu'])
    xr, xw, xk, xv, xa, xg = [m.reshape(b * tp, d) for m in mixes]
    shift_out = xn_tail[:, (t - 1) % xn_tail.shape[1]]
    r = linear(xr, W['rwkv_wr'])
    k = linear(xk, W['rwkv_wk'])
    v = linear(xv, W['rwkv_wv'])
    wl = linear(linear(xw, W['rwkv_w1'], act=jnp.tanh, out_dtype=BF16), W['rwkv_w2'])
    al = linear(linear(xa, W['rwkv_a1'], out_dtype=BF16), W['rwkv_a2'])
    g = linear(linear(xg, W['rwkv_g1'], act=jax.nn.sigmoid, out_dtype=BF16), W['rwkv_g2'])
    prm = jnp.stack([W['rwkv_w0'], W['rwkv_a0'], W['rwkv_kk'], W['rwkv_ka'], W['rwkv_rk'].reshape(d),
                     W['rwkv_gn_w'], W['rwkv_gn_b'], jnp.zeros((d,), F32)])
    n_pair = d // LANES
    s4 = wkv0.astype(F32).reshape(b, n_pair, 2, RWKV_HD, RWKV_HD)
    zero = jnp.zeros_like(s4[:, :, 0])
    s0 = jnp.concatenate([jnp.concatenate([s4[:, :, 0], zero], axis=-1),
                          jnp.concatenate([zero, s4[:, :, 1]], axis=-1)], axis=-2)
    to3 = lambda a: a.reshape(b, tp, d)
    y, sf = rwkv_scan(to3(r), to3(k), to3(v), to3(wl), to3(al), to3(g), prm, s0, t_valid=t)
    wkv = jnp.stack([sf[:, :, :RWKV_HD, :RWKV_HD], sf[:, :, RWKV_HD:, RWKV_HD:]], axis=2)
    y = y[:, :t].reshape(b * t, d)
    return linear(y, W['rwkv_wo'], res=xf), shift_out, wkv.reshape(b, d // RWKV_HD, RWKV_HD, RWKV_HD)


def _sb_mixer(xf, b, t, ln, sb_past, W):
    d = D_MODEL
    xn = rmsnorm(xf, ln)
    qdt = BF16 if sb_past is None else F32
    q = linear(xn, W['sb_wqkv'], col0=0, n=d, gain=W['sb_gq'], hw=SB_HD, out_dtype=qdt)
    k = linear(xn, W['sb_wqkv'], col0=d, n=d, gain=W['sb_gk'], hw=SB_HD)
    v = linear(xn, W['sb_wqkv'], col0=2 * d, n=d)
    to3 = lambda a: a.reshape(b, t, d)
    if sb_past is None:
        o = sb_attention_prompt(to3(q), to3(k), to3(v), W['sb_bias'])
    else:
        pool_k, pool_v, table = sb_past
        assert t == 1
        heads = lambda a: a.reshape(b, SB_HEADS, SB_HD)
        o = sb_attention_decode(heads(q), heads(k), heads(v), pool_k, pool_v, table, W['sb_bias'])
    xf = linear(o.reshape(b * t, d), W['sb_wo'], res=xf)
    return xf, k.reshape(b, t, SB_HEADS, SB_HD), v.reshape(b, t, SB_HEADS, SB_HD)


def _dil_mixer(xf, b, t, pos0, ln, dil_bufs, W):
    n_grp = len(DIL_PATTERNS)
    gd = n_grp * DIL_HEADS * DIL_HD
    xn = rmsnorm(xf, ln)
    rows = t if t >= 8 else b * t
    rope = rope_tables(pos0 + (jnp.arange(rows, dtype=jnp.int32) % t))
    tm = min(rows, LINEAR_TM)
    q = linear(xn, W['dil_wqkv'], col0=0, n=gd, gain=W['dil_gq'], hw=DIL_HD, rope=rope, tm=tm)
    k = linear(xn, W['dil_wqkv'], col0=gd, n=gd, gain=W['dil_gk'], hw=DIL_HD, rope=rope, tm=tm)
    v = linear(xn, W['dil_wqkv'], col0=2 * gd, n=gd, tm=tm)
    to3 = lambda a: a.reshape(b, t, gd)
    if dil_bufs is None:
        o = dil_attention_prompt(to3(q), to3(k), to3(v))
    else:
        assert t == 1
        heads = lambda a: a.reshape(b, n_grp, DIL_HEADS, DIL_HD)
        o = dil_attention_decode(heads(q), heads(k), heads(v), dil_bufs)
    xf = linear(o.reshape(b * t, DIL_HEADS * DIL_HD), W['dil_wo'], res=xf)
    k5 = k.reshape(b, t, n_grp, DIL_HEADS, DIL_HD)
    v5 = v.reshape(b, t, n_grp, DIL_HEADS, DIL_HD)
    states = []
    for g, (win, _) in enumerate(DIL_PATTERNS):
        keep = min(win, t)
        states += [k5[:, t - keep:, g], v5[:, t - keep:, g]]
    return xf, states


def _ssd_mixer(xf, b, t, ln, conv0, h0, W):
    d = D_MODEL
    xn = _pad_time(rmsnorm(xf, ln).reshape(b, t, d), SSM_CHUNK)
    tp = xn.shape[1]
    xn = xn.reshape(b * tp, d)
    z = linear(xn, W['ssm_win'], col0=0, n=SSM_DINNER)
    xbc = linear(xn, W['ssm_win'], col0=SSM_DINNER, n=SSM_CONV_DIM).reshape(b, tp, SSM_CONV_DIM)
    dt_raw = linear(xn, W['ssm_win'][:, SSM_DINNER + SSM_CONV_DIM:])
    prev8 = jnp.pad(conv0.astype(F32), ((0, 0), (8 - (SSM_CONV - 1), 0), (0, 0)))
    xa = causal_conv_silu(xbc, prev8, W['ssm_conv_w'], W['ssm_conv_b'])
    y, hf = ssd_scan(xa, z.reshape(b, tp, SSM_DINNER), dt_raw.reshape(b, tp, SSM_HEADS),
                     W['ssm_dt_bias'], W['ssm_a_log'], W['ssm_d'], W['ssm_norm_w'],
                     h0.astype(F32).reshape(b, SSM_DINNER, SSM_STATE), t_valid=t)
    xf = linear(y[:, :t].reshape(b * t, SSM_DINNER), W['ssm_wout'], res=xf)
    conv_state = jnp.concatenate([conv0.astype(F32), xbc[:, :t]], axis=1)[:, t:]
    return xf, conv_state, hf.reshape(b, SSM_HEADS, SSM_HEADDIM, SSM_STATE)


def _mixer_and_memory(xf, i, G, W):
    b, t, d = G['shape']
    st = G['out']
    kind = i % 4
    ln = W['ln_mix'][i]
    if kind == 0:
        xf, st['rwkv_shift'], st['rwkv_wkv'] = _rwkv_mixer(xf, b, t, ln, G['shift0'], G['wkv0'], W)
    elif kind == 1:
        xf, st['sb_k'], st['sb_v'] = _sb_mixer(xf, b, t, ln, G['sb_past'], W)
    elif kind == 2:
        xf, st['dil'] = _dil_mixer(xf, b, t, G['pos0'], ln, G['dil_bufs'], W)
    else:
        xf, st['ssm_conv'], st['ssm_h'] = _ssd_mixer(xf, b, t, ln, G['conv0'], G['h0'], W)
    xn = rmsnorm(xf, W['ln_mem'][i])
    q = linear(xn, W['mem_wq'], layer=i, gain=W['mem_gq'][i], hw=MEM_HD, out_dtype=BF16)
    q = _pad_time(q.reshape(b, t, d), 8)
    o = mem_attention(q, G['mem_k'], G['mem_v'], i)[:, :t].reshape(b * t, d)
    return linear(o, W['mem_wo'], layer=i, res=xf)


def _run_groups(P, S, W):
    xp, xs = [G['x'].reshape(-1, D_MODEL) for G in (P, S)]
    for i in range(DEPTH):
        xp, xs = ffn(xp, xs, W['ln_ffn1'][i], W['ffn1_gate'], W['ffn1_up'], W['ffn1_down'], i)
        xp = _mixer_and_memory(xp, i, P, W)
        xs = _mixer_and_memory(xs, i, S, W)
        xp, xs = ffn(xp, xs, W['ln_ffn2'][i], W['ffn2_gate'], W['ffn2_up'], W['ffn2_down'], i)
    return xp.reshape(P['shape']), xs.reshape(S['shape'])


def kernel(x_prompt, x_sample, state_rwkv_shift, state_rwkv_wkv, cache_sb_k, cache_sb_v, cache_dil0_k, cache_dil0_v, cache_dil1_k, cache_dil1_v, cache_dil2_k, cache_dil2_v, state_ssm_conv, state_ssm_h, cache_mem_k, cache_mem_v, page_table, mem_prompt, ln_ffn1, ffn1_gate, ffn1_up, ffn1_down, ln_mix, ln_mem, mem_wq, mem_gq, mem_wk, mem_gk, mem_wv, mem_wo, ln_ffn2, ffn2_gate, ffn2_up, ffn2_down, rwkv_mu, rwkv_wr, rwkv_wk, rwkv_wv, rwkv_wo, rwkv_w0, rwkv_w1, rwkv_w2, rwkv_a0, rwkv_a1, rwkv_a2, rwkv_g1, rwkv_g2, rwkv_kk, rwkv_ka, rwkv_rk, rwkv_gn_w, rwkv_gn_b, sb_wqkv, sb_gq, sb_gk, sb_bias, sb_wo, dil_wqkv, dil_gq, dil_gk, dil_wo, ssm_win, ssm_conv_w, ssm_conv_b, ssm_dt_bias, ssm_a_log, ssm_d, ssm_norm_w, ssm_wout):
    W = dict(locals())
    bp, _, d = x_prompt.shape
    n_mem = mem_prompt.shape[1]
    past_len = page_table.shape[1] * cache_sb_k.shape[1]

    mem_rows = mem_prompt.reshape(bp * n_mem, d)
    p_mem_k = jnp.stack([linear(mem_rows, mem_wk, layer=i, gain=mem_gk[i], hw=MEM_HD)
                         for i in range(DEPTH)]).reshape(DEPTH, bp, n_mem, d)
    p_mem_v = jnp.stack([linear(mem_rows, mem_wv, layer=i)
                         for i in range(DEPTH)]).reshape(DEPTH, bp, n_mem, d)
    P = dict(x=x_prompt, shape=x_prompt.shape, pos0=0, mem_k=p_mem_k, mem_v=p_mem_v,
             shift0=jnp.zeros((bp, d), F32), wkv0=jnp.zeros((bp, d // RWKV_HD, RWKV_HD, RWKV_HD), F32),
             sb_past=None, dil_bufs=None,
             conv0=jnp.zeros((bp, SSM_CONV - 1, SSM_CONV_DIM), F32),
             h0=jnp.zeros((bp, SSM_HEADS, SSM_HEADDIM, SSM_STATE), F32), out={})
    S = dict(x=x_sample, shape=x_sample.shape, pos0=past_len, mem_k=cache_mem_k, mem_v=cache_mem_v,
             shift0=state_rwkv_shift, wkv0=state_rwkv_wkv, sb_past=(cache_sb_k, cache_sb_v, page_table),
             dil_bufs=(cache_dil0_k, cache_dil0_v, cache_dil1_k, cache_dil1_v, cache_dil2_k, cache_dil2_v),
             conv0=state_ssm_conv, h0=state_ssm_h, out={})
    y_p, y_s = _run_groups(P, S, W)
    sp, ss = P['out'], S['out']

    mem_shape = (DEPTH, bp, n_mem, MEM_HEADS, MEM_HD)
    dil = []
    for g in range(len(DIL_PATTERNS)):
        dil += [sp['dil'][2 * g], sp['dil'][2 * g + 1], ss['dil'][2 * g], ss['dil'][2 * g + 1]]
    return (y_p, y_s,
            sp['rwkv_shift'], ss['rwkv_shift'], sp['rwkv_wkv'], ss['rwkv_wkv'],
            sp['sb_k'], sp['sb_v'], ss['sb_k'], ss['sb_v'],
            *dil,
            sp['ssm_conv'], ss['ssm_conv'], sp['ssm_h'], ss['ssm_h'],
            p_mem_k.reshape(mem_shape), p_mem_v.reshape(mem_shape))
```

```python
import functools
import math

import jax
import jax.numpy as jnp
from jax import lax
from jax.experimental import pallas as pl
from jax.experimental.pallas import tpu as pltpu

F32 = jnp.float32
BF16 = jnp.bfloat16

D_MODEL = 2048
DEPTH = 4
NORM_EPS = 1e-6
PAGE_SIZE = 128
MEM_HEADS = 4
MEM_HD = D_MODEL // MEM_HEADS
RWKV_HD = 64
GN_EPS = 64e-5
SB_HD = 128
SB_HEADS = D_MODEL // SB_HD
DIL_PATTERNS = ((128, 1), (512, 4), (2048, 16))
DIL_HD = 128
DIL_HEADS = 8
ROPE_THETA = 500000.0
ROPE_DIM = DIL_HD // 4
SSM_DINNER = 2 * D_MODEL
SSM_HEADDIM = 64
SSM_HEADS = SSM_DINNER // SSM_HEADDIM
SSM_STATE = 128
SSM_GROUPS = 8
SSM_CONV = 4
SSM_CONV_DIM = SSM_DINNER + 2 * SSM_GROUPS * SSM_STATE
SSM_CHUNK = 128
SSM_NORM_EPS = 1e-5

LANES = 128
BF16_ROWS = 16
V7X_VMEM_BYTES = 64 << 20
VMEM_LIMIT = V7X_VMEM_BYTES - (8 << 20)
FFN_VMEM_LIMIT = V7X_VMEM_BYTES - (4 << 20)
DIL_BLOCKS_PER_ITER = 8
LINEAR_TM = 1024
LINEAR_TN = 512
LINEAR_TN_WIDE = 1024
RWKV_CHUNK = 64
RWKV_PAIRS_PER_ITER = 16
RWKV_PAIRS_PER_DOT = 2
NEG_BIG = -1e30


def _params(*sem, vmem=VMEM_LIMIT):
    return pltpu.CompilerParams(dimension_semantics=sem, vmem_limit_bytes=vmem)


def _dot(a, b):
    return jnp.dot(a.astype(BF16), b.astype(BF16), preferred_element_type=F32)


def _dot_nt(a, b):
    return lax.dot_general(a.astype(BF16), b.astype(BF16), (((1,), (1,)), ((), ())),
                           preferred_element_type=F32)


def _split3(x):
    hi = x.astype(BF16)
    r1 = x - hi.astype(F32)
    mid = r1.astype(BF16)
    lo = (r1 - mid.astype(F32)).astype(BF16)
    return hi, mid, lo


def _dot_exact_rhs(x, m_bf16):
    hi, mid, lo = _split3(x)
    d = lambda p: jnp.dot(p, m_bf16, preferred_element_type=F32)
    return d(hi) + d(mid) + d(lo)


def _dot_exact_lhs(m_bf16, x):
    hi, mid, lo = _split3(x)
    d = lambda p: jnp.dot(m_bf16, p, preferred_element_type=F32)
    return d(hi) + d(mid) + d(lo)


def _iota(shape, dim):
    return lax.broadcasted_iota(jnp.int32, shape, dim)


def _softplus(x):
    return jnp.maximum(x, 0.0) + jnp.log1p(jnp.exp(-jnp.abs(x)))


def _log_sigmoid(x):
    return jnp.minimum(x, 0.0) - jnp.log(1.0 + jnp.exp(-jnp.abs(x)))


def _silu(x):
    return x * jax.nn.sigmoid(x)


def _linear_body(*refs, has_gain, has_rope, has_res, act, hw, res_scale):
    it = iter(refs)
    x_ref, w_ref = next(it), next(it)
    gain_ref = next(it) if has_gain else None
    cos_ref, sin_ref = (next(it), next(it)) if has_rope else (None, None)
    res_ref = next(it) if has_res else None
    o_ref, wbf_ref = next(it), next(it)

    @pl.when(pl.program_id(1) == 0)
    def _():
        wbf_ref[...] = w_ref[...].astype(BF16)

    acc = jnp.dot(x_ref[...].astype(BF16), wbf_ref[...], preferred_element_type=F32)
    if act is not None:
        acc = act(acc)
    if has_res:
        acc = res_ref[...] + res_scale * acc
    if has_gain:
        for s in range(acc.shape[1] // hw):
            y = acc[:, s * hw:(s + 1) * hw]
            ms = jnp.mean(y * y, axis=-1, keepdims=True)
            y = y * lax.rsqrt(ms + NORM_EPS) * gain_ref[...]
            if has_rope:
                lane = _iota(y.shape, 1)
                half = ROPE_DIM // 2
                rot = jnp.where(lane < half, pltpu.roll(y, hw - half, 1), pltpu.roll(y, half, 1))
                y = y * cos_ref[...] + rot * sin_ref[...]
            o_ref[:, s * hw:(s + 1) * hw] = y.astype(o_ref.dtype)
    else:
        o_ref[...] = acc.astype(o_ref.dtype)


def linear(x, w, *, col0=0, n=None, out_dtype=F32, gain=None, hw=None, rope=None, res=None,
           res_scale=1.0, act=None, layer=None, tm=LINEAR_TM, tn=None, name="linear"):
    m, k = x.shape
    n = w.shape[-1] - col0 if n is None else n
    if tn is None:
        tn = LINEAR_TN_WIDE if (n % LINEAR_TN_WIDE == 0 and col0 % LINEAR_TN_WIDE == 0
                                and k * LINEAR_TN_WIDE * 10 <= VMEM_LIMIT // 3 * 2) else LINEAR_TN
    tm, tn = min(tm, m), min(tn, n)
    assert m % tm == 0 and n % tn == 0 and col0 % tn == 0 and w.shape[-2] == k
    cb = col0 // tn
    if layer is None:
        w_spec = pl.BlockSpec((k, tn), lambda j, i: (0, j + cb))
    else:
        w_spec = pl.BlockSpec((None, k, tn), lambda j, i: (layer, 0, j + cb))
    in_specs = [pl.BlockSpec((tm, k), lambda j, i: (i, 0)), w_spec]
    args = [x, w]
    if gain is not None:
        assert tn % hw == 0
        in_specs.append(pl.BlockSpec((1, hw), lambda j, i: (0, 0)))
        args.append(gain.reshape(1, hw).astype(F32))
    if rope is not None:
        cos, sin = rope
        nt = cos.shape[0] // tm
        assert hw == LANES and cos.shape[0] % tm == 0
        in_specs += [pl.BlockSpec((tm, LANES), lambda j, i: (i % nt, 0))] * 2
        args += [cos, sin]
    if res is not None:
        in_specs.append(pl.BlockSpec((tm, tn), lambda j, i: (i, j)))
        args.append(res)
    body = functools.partial(_linear_body, has_gain=gain is not None, has_rope=rope is not None,
                             has_res=res is not None, act=act, hw=hw, res_scale=res_scale)
    return pl.pallas_call(
        body,
        grid=(n // tn, m // tm),
        in_specs=in_specs,
        out_specs=pl.BlockSpec((tm, tn), lambda j, i: (i, j)),
        out_shape=jax.ShapeDtypeStruct((m, n), out_dtype),
        scratch_shapes=[pltpu.VMEM((k, tn), BF16)],
        compiler_params=_params("parallel", "arbitrary"),
        name=name,
    )(*args)


def _rmsnorm_body(x_ref, g_ref, o_ref):
    x = x_ref[...]
    ms = jnp.mean(x * x, axis=-1, keepdims=True)
    o_ref[...] = (x * lax.rsqrt(ms + NORM_EPS) * g_ref[...]).astype(o_ref.dtype)


def rmsnorm(x, g, out_dtype=BF16, tm=512):
    m, d = x.shape
    tm = min(tm, m)
    assert m % tm == 0
    return pl.pallas_call(
        _rmsnorm_body,
        grid=(m // tm,),
        in_specs=[pl.BlockSpec((tm, d), lambda i: (i, 0)), pl.BlockSpec((1, d), lambda i: (0, 0))],
        out_specs=pl.BlockSpec((tm, d), lambda i: (i, 0)),
        out_shape=jax.ShapeDtypeStruct((m, d), out_dtype),
        compiler_params=_params("parallel"),
        name="rmsnorm",
    )(x, g.reshape(1, d))


def _ffn_body(x_ref, xr_ref, g_ref, wg_ref, wu_ref, wd_ref, o_ref, or_ref, xn_ref, *, tm):
    i, f = pl.program_id(0), pl.program_id(1)

    def start(src_ref, rows, dst_ref):
        x = src_ref[...]
        ms = jnp.mean(x * x, axis=-1, keepdims=True)
        xn_ref[rows, :] = (x * lax.rsqrt(ms + NORM_EPS) * g_ref[...]).astype(BF16)
        dst_ref[...] = x

    @pl.when(f == 0)
    def _():
        start(x_ref, slice(0, tm), o_ref)

    @pl.when((f == 0) & (i == 0))
    def _():
        start(xr_ref, slice(tm, tm + xr_ref.shape[0]), or_ref)

    xn = xn_ref[...]
    gate = jnp.dot(xn, wg_ref[...].astype(BF16), preferred_element_type=F32)
    up = jnp.dot(xn, wu_ref[...].astype(BF16), preferred_element_type=F32)
    h = (0.5 * _silu(gate) * up).astype(BF16)
    y = jnp.dot(h, wd_ref[...].astype(BF16), preferred_element_type=F32)
    o_ref[...] += y[:tm]
    or_ref[...] += jnp.where(i == 0, y[tm:], 0.0)


def ffn(x, x_rider, g, w_gate, w_up, w_down, layer, tm=1024, tf=512):
    m, d = x.shape
    mr = x_rider.shape[0]
    mr_pad = -(-mr // BF16_ROWS) * BF16_ROWS
    x_rider = jnp.pad(x_rider, ((0, mr_pad - mr), (0, 0)))
    f = w_gate.shape[-1]
    tm = min(tm, m)
    assert m % tm == 0 and f % tf == 0
    out, out_rider = pl.pallas_call(
        functools.partial(_ffn_body, tm=tm),
        grid=(m // tm, f // tf),
        in_specs=[pl.BlockSpec((tm, d), lambda i, j: (i, 0), pipeline_mode=pl.Buffered(1)),
                  pl.BlockSpec((mr_pad, d), lambda i, j: (0, 0)),
                  pl.BlockSpec((1, d), lambda i, j: (0, 0)),
                  pl.BlockSpec((None, d, tf), lambda i, j: (layer, 0, j)),
                  pl.BlockSpec((None, d, tf), lambda i, j: (layer, 0, j)),
                  pl.BlockSpec((None, tf, d), lambda i, j: (layer, j, 0))],
        out_specs=[pl.BlockSpec((tm, d), lambda i, j: (i, 0)),
                   pl.BlockSpec((mr_pad, d), lambda i, j: (0, 0))],
        out_shape=[jax.ShapeDtypeStruct((m, d), F32), jax.ShapeDtypeStruct((mr_pad, d), F32)],
        scratch_shapes=[pltpu.VMEM((tm + mr_pad, d), BF16)],
        compiler_params=_params("arbitrary", "arbitrary", vmem=FFN_VMEM_LIMIT),
        name="ffn",
    )(x, x_rider, g.reshape(1, d), w_gate, w_up, w_down)
    return out, out_rider[:mr]


def _memattn_body(q_ref, *refs, scale, head_axis):
    each = lambda f, *cols: [f(*args) for args in zip(*cols)]
    lanes = [slice(h * MEM_HD, (h + 1) * MEM_HD) for h in range(MEM_HEADS)]
    k_ref, v_ref, o_ref = refs
    if head_axis:
        ks = [k_ref[0, 0, :, h, :] for h in range(MEM_HEADS)]
        vs = [v_ref[0, 0, :, h, :] for h in range(MEM_HEADS)]
    else:
        ks = [k_ref[0, 0, :, sl] for sl in lanes]
        vs = [v_ref[0, 0, :, sl] for sl in lanes]
    s = each(lambda sl, k: _dot_nt(q_ref[0, :, sl], k) * scale, lanes, ks)
    m = each(lambda t: jnp.max(t, axis=-1, keepdims=True), s)
    p = each(lambda t, mm: jnp.exp(t - mm), s, m)
    l = each(lambda t: jnp.sum(t, axis=-1, keepdims=True), p)
    o = each(lambda t, v, ll: _dot(t, v) / ll, p, vs, l)
    for sl, t in zip(lanes, o):
        o_ref[0, :, sl] = t.astype(o_ref.dtype)


def mem_attention(q, mem_k, mem_v, layer, tq=512):
    b, t, d = q.shape
    n_mem = mem_k.shape[2]
    tq = min(tq, t)
    assert t % tq == 0
    head_axis = mem_k.ndim == 5
    if head_axis:
        kv_specs = [pl.BlockSpec((1, 1, n_mem, MEM_HEADS, MEM_HD), lambda bi, ti: (layer, bi, 0, 0, 0))] * 2
        kv_args = [mem_k, mem_v]
    else:
        kv_specs = [pl.BlockSpec((1, 1, n_mem, d), lambda bi, ti: (layer, bi, 0, 0))] * 2
        kv_args = [mem_k, mem_v]
    return pl.pallas_call(
        functools.partial(_memattn_body, scale=MEM_HD ** -0.5, head_axis=head_axis),
        grid=(b, t // tq),
        in_specs=[pl.BlockSpec((1, tq, d), lambda bi, ti: (bi, ti, 0))] + kv_specs,
        out_specs=pl.BlockSpec((1, tq, d), lambda bi, ti: (bi, ti, 0)),
        out_shape=jax.ShapeDtypeStruct((b, t, d), BF16),
        compiler_params=_params("parallel", "parallel"),
        name="mem_attention",
    )(q, *kv_args)


def rope_tables(pos):
    half = ROPE_DIM // 2
    inv_freq = ROPE_THETA ** (-jnp.arange(half, dtype=F32) / half)
    ang = pos.astype(F32)[:, None] * inv_freq[None, :]
    cos, sin = jnp.cos(ang), jnp.sin(ang)
    rest = DIL_HD - ROPE_DIM
    n = pos.shape[0]
    cos_t = jnp.concatenate([cos, cos, jnp.ones((n, rest), F32)], axis=1)
    sin_t = jnp.concatenate([-sin, sin, jnp.zeros((n, rest), F32)], axis=1)
    return cos_t, sin_t


def _sb_tiles(qs, ks, vs, biases, causal, carries, accs, upper):
    each = lambda f, *cols: [f(*args) for args in zip(*cols)]
    mask = (lambda t: t) if causal is None else (lambda t: jnp.where(causal, t, 0.0))
    z = each(lambda q, k, b: _dot_nt(q, k) * (SB_HD ** -0.5) + b, qs, ks, biases)
    ls = each(_log_sigmoid, z)
    log_keep = each(lambda l, t: mask(l - t), ls, z)
    hi = each(lambda t: t.astype(BF16), log_keep)
    lo = each(lambda t, h: (t - h.astype(F32)).astype(BF16), log_keep, hi)
    local = each(lambda h, l: jnp.dot(jnp.concatenate([h, l], axis=0), upper, preferred_element_type=F32),
                 hi, lo)
    rows = qs[0].shape[0]
    att = each(lambda l, s, c: mask(jnp.exp(l + s[:rows] + s[rows:] + c)), ls, local, carries)
    accs = each(lambda a, p, v: a + _dot(p, v), accs, att, vs)
    carries = each(lambda c, t: c + jnp.sum(t, axis=-1, keepdims=True), carries, log_keep)
    return carries, accs


def _sb_tile(q, k, v, bias, causal, carry, acc, upper):
    carries, accs = _sb_tiles([q], [k], [v], [bias], causal, [carry], [acc], upper)
    return carries[0], accs[0]


def _upper_ones(n):
    return jnp.where(_iota((n, n), 0) > _iota((n, n), 1), 1.0, 0.0).astype(BF16)


def _sb_prompt_body(bias_ref, q_ref, k_ref, v_ref, o_ref, *, tq, heads):
    hb, qi = pl.program_id(1), pl.program_id(2)
    upper = _upper_ones(tq)
    lanes = [slice(u * SB_HD, (u + 1) * SB_HD) for u in range(heads)]
    qs = [q_ref[0, :, sl] for sl in lanes]
    biases = [bias_ref[hb * heads + u] for u in range(heads)]

    def tiles(j, state, causal):
        start = pl.multiple_of(j * tq, tq)
        ks = [k_ref[0, pl.ds(start, tq), sl] for sl in lanes]
        vs = [v_ref[0, pl.ds(start, tq), sl] for sl in lanes]
        return _sb_tiles(qs, ks, vs, biases, causal, state[0], state[1], upper)

    init = ([jnp.zeros((tq, 1), F32) for _ in lanes], [jnp.zeros((tq, SB_HD), F32) for _ in lanes])
    state = tiles(qi, init, _iota((tq, tq), 1) < _iota((tq, tq), 0))
    _, accs = lax.fori_loop(0, qi, lambda jj, st: tiles(qi - 1 - jj, st, None), state)
    for sl, acc in zip(lanes, accs):
        o_ref[0, :, sl] = acc.astype(o_ref.dtype)


def sb_attention_prompt(q, k, v, bias, tq=256, heads=4):
    b, t, d = q.shape
    tq = min(tq, t)
    w = heads * SB_HD
    assert t % tq == 0 and d % w == 0
    kv_spec = pl.BlockSpec((1, t, w), lambda bi, h, qi: (bi, 0, h))
    io_spec = pl.BlockSpec((1, tq, w), lambda bi, h, qi: (bi, qi, h))
    return pl.pallas_call(
        functools.partial(_sb_prompt_body, tq=tq, heads=heads),
        grid=(b, d // w, t // tq),
        in_specs=[pl.BlockSpec(memory_space=pltpu.SMEM), io_spec, kv_spec, kv_spec],
        out_specs=io_spec,
        out_shape=jax.ShapeDtypeStruct((b, t, d), BF16),
        compiler_params=_params("parallel", "parallel", "arbitrary"),
        name="sb_attention_prompt",
    )(bias, q, k, v)


def _sb_decode_body(table_ref, bias_ref, q_ref, kn_ref, vn_ref, *rest, n_pages, per_step):
    del table_ref
    page_refs = rest[:2 * per_step]
    o_ref, acc_ref, carry_ref = rest[2 * per_step:]
    j = pl.program_id(1)
    nh, cols = SB_HEADS, PAGE_SIZE * SB_HEADS
    n_sub = cols // LANES
    bias = bias_ref[...]
    past_len = n_pages * PAGE_SIZE
    scale = SB_HD ** -0.5
    q = q_ref[0]

    @pl.when(j == 0)
    def _():
        z = jnp.sum(q * kn_ref[0], axis=-1, keepdims=True) * scale + bias
        causal = jnp.full((nh, 1), past_len, jnp.int32) < past_len
        ls = _log_sigmoid(z)
        carry_ref[...] = jnp.where(causal, ls - z, 0.0)
        acc_ref[...] = jnp.where(causal, jnp.exp(ls), 0.0) * vn_ref[0]

    each = lambda f, *c: [f(*args) for args in zip(*c)]
    col = _iota((nh, cols), 1)
    own = (col & (nh - 1)) == _iota((nh, cols), 0)
    upper = _upper_ones(LANES)
    pages = [n_pages - 1 - (j * per_step + u) for u in range(per_step)]
    masks = [own & ((p * PAGE_SIZE + col // nh) < past_len) for p in pages]
    ks = [page_refs[2 * u][...] for u in range(per_step)]
    vs = [page_refs[2 * u + 1][...] for u in range(per_step)]
    z = each(lambda k: _dot_nt(q, k) * scale + bias, ks)
    ls = each(_log_sigmoid, z)
    log_keep = each(lambda l, t, m: jnp.where(m, l - t, 0.0), ls, z, masks)

    def local_suffix(t):
        x = jnp.concatenate([t[:, c * LANES:(c + 1) * LANES] for c in range(n_sub)], axis=0)
        hi = x.astype(BF16)
        lo = (x - hi.astype(F32)).astype(BF16)
        y = jnp.dot(jnp.concatenate([hi, lo], axis=0), upper, preferred_element_type=F32)
        y = y[:n_sub * nh] + y[n_sub * nh:]
        return [y[c * nh:(c + 1) * nh] for c in range(n_sub)]

    local = each(local_suffix, log_keep)
    carry = carry_ref[...]
    between = []
    for t, loc in zip(log_keep, local):
        pieces = [None] * n_sub
        for c in reversed(range(n_sub)):
            pieces[c] = loc[c] + carry
            carry = carry + jnp.sum(t[:, c * LANES:(c + 1) * LANES], axis=-1, keepdims=True)
        between.append(jnp.concatenate(pieces, axis=1))
    carry_ref[...] = carry
    att = each(lambda l, s, m: jnp.where(m, jnp.exp(l + s), 0.0), ls, between, masks)
    acc_ref[...] += sum(each(_dot, att, vs))

    @pl.when(j == pl.num_programs(1) - 1)
    def _():
        o_ref[0] = acc_ref[...]


def sb_attention_decode(q, k_new, v_new, pool_k, pool_v, table, bias, per_step=4):
    b = q.shape[0]
    n_pages = table.shape[1]
    assert n_pages % per_step == 0 and pool_k.shape[1:] == (PAGE_SIZE, SB_HEADS, SB_HD)
    assert SB_HEADS & (SB_HEADS - 1) == 0 and (PAGE_SIZE * SB_HEADS) % LANES == 0
    page_rows = PAGE_SIZE * SB_HEADS
    pool_k = pool_k.reshape(-1, SB_HD)
    pool_v = pool_v.reshape(-1, SB_HD)
    row_spec = pl.BlockSpec((1, SB_HEADS, SB_HD), lambda bi, j, tab: (bi, 0, 0))

    def page_spec(u):
        return pl.BlockSpec((page_rows, SB_HD),
                            lambda bi, j, tab: (tab[bi, n_pages - 1 - (j * per_step + u)], 0))

    pages = [page_spec(u) for u in range(per_step) for _ in range(2)]
    grid_spec = pltpu.PrefetchScalarGridSpec(
        num_scalar_prefetch=1,
        grid=(b, n_pages // per_step),
        in_specs=[pl.BlockSpec((SB_HEADS, 1), lambda bi, j, tab: (0, 0)),
                  row_spec, row_spec, row_spec] + pages,
        out_specs=row_spec,
        scratch_shapes=[pltpu.VMEM((SB_HEADS, SB_HD), F32), pltpu.VMEM((SB_HEADS, 1), F32)],
    )
    return pl.pallas_call(
        functools.partial(_sb_decode_body, n_pages=n_pages, per_step=per_step),
        grid_spec=grid_spec,
        out_shape=jax.ShapeDtypeStruct((b, SB_HEADS, SB_HD), F32),
        compiler_params=_params("parallel", "arbitrary"),
        name="sb_attention_decode",
    )(table, bias.reshape(SB_HEADS, 1), q, k_new, v_new, *([pool_k, pool_v] * per_step))


def _dil_prompt_body(*refs, t, blk):
    n_grp = len(DIL_PATTERNS)
    q_refs, k_refs, v_refs = refs[:n_grp], refs[n_grp:2 * n_grp], refs[2 * n_grp:3 * n_grp]
    o_ref, m_s, l_s, acc_s = refs[3 * n_grp:3 * n_grp + 4]
    scale = DIL_HD ** -0.5
    order = sorted(range(n_grp), key=lambda g: -DIL_PATTERNS[g][1])
    for n, g in enumerate(order):
        win, dil = DIL_PATTERNS[g]
        band, cls_len = win // dil, t // dil
        n_blk = cls_len // blk
        n_keys = min(2 * blk, cls_len)
        first, last = n == 0, n == n_grp - 1
        assert band == blk and (not last or dil == 1) and n_blk * blk == cls_len
        q_ref, k_ref, v_ref = q_refs[g], k_refs[g], v_refs[g]
        rel = _iota((blk, n_keys), 0) - _iota((blk, n_keys), 1)

        def rows(start, size, dil=dil):
            return pl.ds(start, size, stride=dil) if dil > 1 else pl.ds(start, size)

        def blocks(i, _, q_ref=q_ref, k_ref=k_ref, v_ref=v_ref, dil=dil, n_blk=n_blk, n_keys=n_keys,
                   rel=rel, rows=rows, first=first, last=last, band=band):
            each = lambda f, *cols: [f(*args) for args in zip(*cols)]
            idxs = [i * DIL_BLOCKS_PER_ITER + u for u in range(DIL_BLOCKS_PER_ITER)]
            lbs = [idx % n_blk for idx in idxs]
            k_cls = [jnp.maximum(lb - 1, 0) * blk for lb in lbs]
            if dil == 1:
                q_tok = [pl.multiple_of(lb * blk, blk) for lb in lbs]
                k_tok = [pl.multiple_of(kc, blk) for kc in k_cls]
            else:
                q_tok = [idx // n_blk + lb * blk * dil for idx, lb in zip(idxs, lbs)]
                k_tok = [idx // n_blk + kc * dil for idx, kc in zip(idxs, k_cls)]
            sel = [rows(t, blk) for t in q_tok]
            q = [q_ref[0, s, :] for s in sel]
            k = [k_ref[0, rows(t, n_keys), :] for t in k_tok]
            v = [v_ref[0, rows(t, n_keys), :] for t in k_tok]
            old = None if first else [(m_s[s, :], l_s[s, :], acc_s[s, :]) for s in sel]

            def scores(qq, kk, lb, kc):
                dist = rel + (lb * blk - kc)
                valid = jnp.where(dist >= 0, dist, band + 1) <= band
                return jnp.where(valid, _dot_nt(qq, kk) * scale, NEG_BIG)

            s = each(scores, q, k, lbs, k_cls)
            m = each(lambda t: jnp.max(t, axis=-1, keepdims=True), s)
            p = each(lambda t, mm: jnp.exp(t - mm), s, m)
            l = each(lambda t: jnp.sum(t, axis=-1, keepdims=True), p)
            acc = each(_dot, p, v)
            if not first:
                m_new = each(lambda o, mm: jnp.maximum(o[0], mm), old, m)
                e0 = each(lambda o, mn: jnp.exp(o[0] - mn), old, m_new)
                e1 = each(lambda mm, mn: jnp.exp(mm - mn), m, m_new)
                l = each(lambda o, a, b, t: a * o[1] + b * t, old, e0, e1, l)
                acc = each(lambda o, a, b, t: a * o[2] + b * t, old, e0, e1, acc)
                m = m_new
            for u, sl in enumerate(sel):
                if last:
                    o_ref[0, sl, :] = (acc[u] / l[u]).astype(o_ref.dtype)
                else:
                    m_s[sl, :], l_s[sl, :], acc_s[sl, :] = m[u], l[u], acc[u]
            return 0

        assert (dil * n_blk) % DIL_BLOCKS_PER_ITER == 0
        lax.fori_loop(0, dil * n_blk // DIL_BLOCKS_PER_ITER, blocks, 0)


def dil_attention_prompt(q, k, v, blk=128):
    b, t, _ = q.shape
    n_grp = len(DIL_PATTERNS)
    assert t % blk == 0

    def spec(g):
        return pl.BlockSpec((1, t, DIL_HD), lambda bi, h: (bi, 0, g * DIL_HEADS + h))

    return pl.pallas_call(
        functools.partial(_dil_prompt_body, t=t, blk=blk),
        grid=(b, DIL_HEADS),
        in_specs=[spec(g) for g in range(n_grp)] * 3,
        out_specs=pl.BlockSpec((1, t, DIL_HD), lambda bi, h: (bi, 0, h)),
        out_shape=jax.ShapeDtypeStruct((b, t, DIL_HEADS * DIL_HD), BF16),
        scratch_shapes=[pltpu.VMEM((t, 1), F32), pltpu.VMEM((t, 1), F32), pltpu.VMEM((t, DIL_HD), F32)],
        compiler_params=_params("parallel", "parallel"),
        name="dil_attention_prompt",
    )(*([q] * n_grp + [k] * n_grp + [v] * n_grp))


def _dil_decode_body(*refs):
    n_grp = len(DIL_PATTERNS)
    q_ref, kn_ref, vn_ref = refs[:3]
    kb_refs, vb_refs = refs[3:3 + n_grp], refs[3 + n_grp:3 + 2 * n_grp]
    o_ref = refs[3 + 2 * n_grp]
    scale = DIL_HD ** -0.5
    nh = DIL_HEADS
    scores, news, masks = [], [], []
    for g in range(n_grp):
        q = q_ref[0, g]
        band = kb_refs[g].shape[1]
        k2 = kb_refs[g][0].reshape(band * nh, DIL_HD)
        own = (_iota((nh, band * nh), 1) & (nh - 1)) == _iota((nh, band * nh), 0)
        masks.append(own)
        scores.append(jnp.where(own, _dot_nt(q, k2) * scale, NEG_BIG))
        news.append(jnp.sum(q * kn_ref[0, g], axis=-1, keepdims=True) * scale)
    m = functools.reduce(jnp.maximum, [jnp.max(s, axis=-1, keepdims=True) for s in scores] + news)
    l = jnp.zeros((nh, 1), F32)
    acc = jnp.zeros((nh, DIL_HD), F32)
    for g in range(n_grp):
        p = jnp.where(masks[g], jnp.exp(scores[g] - m), 0.0)
        pn = jnp.exp(news[g] - m)
        v2 = vb_refs[g][0].reshape(p.shape[1], DIL_HD)
        l = l + jnp.sum(p, axis=-1, keepdims=True) + pn
        acc = acc + _dot(p, v2) + pn * vn_ref[0, g]
    o_ref[0] = acc / l


def dil_attention_decode(q, k_new, v_new, bufs):
    b = q.shape[0]
    n_grp = len(DIL_PATTERNS)
    assert DIL_HEADS & (DIL_HEADS - 1) == 0
    row_spec = pl.BlockSpec((1, n_grp, DIL_HEADS, DIL_HD), lambda bi: (bi, 0, 0, 0))
    views, specs = [], []
    for which in range(2):
        for g, (win, dil) in enumerate(DIL_PATTERNS):
            buf = bufs[2 * g + which]
            assert buf.shape[1] == win and win % dil == 0
            views.append(buf.reshape(b, win // dil, dil * DIL_HEADS, DIL_HD))
            specs.append(pl.BlockSpec((1, win // dil, DIL_HEADS, DIL_HD), lambda bi: (bi, 0, 0, 0)))
    return pl.pallas_call(
        _dil_decode_body,
        grid=(b,),
        in_specs=[row_spec] * 3 + specs,
        out_specs=pl.BlockSpec((1, DIL_HEADS, DIL_HD), lambda bi: (bi, 0, 0)),
        out_shape=jax.ShapeDtypeStruct((b, DIL_HEADS, DIL_HD), F32),
        compiler_params=_params("parallel"),
        name="dil_attention_decode",
    )(q, k_new, v_new, *views)


def _rwkv_mix_body(x_ref, g_ref, shift_ref, mu_ref, *refs, tt):
    o_refs, xn_ref, buf_ref = refs[:-2], refs[-2], refs[-1]
    halo = 8

    @pl.when(pl.program_id(1) == 0)
    def _():
        buf_ref[0:halo, :] = jnp.broadcast_to(shift_ref[0], (halo, shift_ref.shape[-1]))

    x = x_ref[0]
    inv = lax.rsqrt(jnp.mean(x * x, axis=-1, keepdims=True) + NORM_EPS)
    width = 2 * LANES
    for c in range(x.shape[1] // width):
        sl = slice(c * width, (c + 1) * width)
        xn = x_ref[0, :, sl] * inv * g_ref[:, sl]
        buf_ref[halo:halo + tt, sl] = xn
        xx = buf_ref[halo - 1:halo - 1 + tt, sl] - xn
        for i, o_ref in enumerate(o_refs):
            o_ref[0, :, sl] = (xn + xx * mu_ref[i:i + 1, sl]).astype(o_ref.dtype)
        xn_ref[0, :, sl] = xn
    buf_ref[0:halo, :] = buf_ref[tt:tt + halo, :]


def rwkv_mix(x, g, shift0, mu, tt=128):
    b, t, d = x.shape
    tt = math.gcd(tt, t)
    assert tt % 8 == 0
    n_mix = mu.shape[0]
    seq = pl.BlockSpec((1, tt, d), lambda bi, ti: (bi, ti, 0))
    outs = pl.pallas_call(
        functools.partial(_rwkv_mix_body, tt=tt),
        grid=(b, t // tt),
        in_specs=[seq, pl.BlockSpec((1, d), lambda bi, ti: (0, 0)),
                  pl.BlockSpec((1, 1, d), lambda bi, ti: (bi, 0, 0)),
                  pl.BlockSpec((n_mix, d), lambda bi, ti: (0, 0))],
        out_specs=[seq] * n_mix + [pl.BlockSpec((1, tt, d), lambda bi, ti: (bi, 0, 0))],
        out_shape=[jax.ShapeDtypeStruct((b, t, d), BF16)] * n_mix + [jax.ShapeDtypeStruct((b, tt, d), F32)],
        scratch_shapes=[pltpu.VMEM((tt + 8, d), F32)],
        compiler_params=_params("parallel", "arbitrary"),
        name="rwkv_mix",
    )(x, g.reshape(1, d), shift0.astype(F32).reshape(b, 1, d), mu)
    return outs[:n_mix], outs[n_mix]


def _rwkv_scan_body(r_ref, k_ref, v_ref, wl_ref, al_ref, g_ref, prm_ref, s0_ref, y_ref, sf_ref,
                    st_ref, *, t_valid, t_total):
    c = pl.program_id(1)
    C, hd = RWKV_CHUNK, RWKV_HD
    n2 = 2 * C
    grp = RWKV_PAIRS_PER_DOT
    rows = grp * n2

    @pl.when(c == 0)
    def _():
        st_ref[...] = s0_ref[0]

    left = _iota((C, LANES), 1) < hd
    ri, ci = _iota((rows, rows), 0), _iota((rows, rows), 1)
    strict = (ci & (C - 1)) < (ri & (C - 1))
    incl = (ci & (C - 1)) <= (ri & (C - 1))
    tri = jnp.where(_iota((C, C), 1) <= _iota((C, C), 0), 1.0, 0.0).astype(BF16)
    same_head = (_iota((LANES, LANES), 0) // hd) == (_iota((LANES, LANES), 1) // hd)
    seg_ones = jnp.where(same_head, 1.0, 0.0).astype(BF16)
    masked = t_valid < t_total
    rows_valid = (c * C + _iota((C, LANES), 0)) < t_valid
    zero_slab = jnp.zeros((n2, LANES), F32)

    def seg_sum(x):
        hi = x.astype(BF16)
        lo = (x - hi.astype(F32)).astype(BF16)
        y = jnp.dot(jnp.concatenate([hi, lo], axis=0), seg_ones, preferred_element_type=F32)
        return y[:x.shape[0]] + y[x.shape[0]:]

    def cumsum_steps(x):
        w = x.shape[1]
        y = jnp.dot(tri, jnp.concatenate(_split3(x), axis=1), preferred_element_type=F32)
        return y[:, :w] + y[:, w:2 * w] + y[:, 2 * w:]

    def stack(x):
        return jnp.concatenate([jnp.where(left, x, 0.0), jnp.where(left, 0.0, x)], axis=0)

    def spread(slabs):
        return jnp.concatenate(
            [jnp.concatenate([s if q == j else zero_slab for j in range(grp)], axis=1)
             for q, s in enumerate(slabs)], axis=0)

    def core(groups):
        each = lambda f, *cols: [f(*args) for args in zip(*cols)]
        dss, stss = [g[0] for g in groups], [g[1] for g in groups]
        flat_d = [d for ds in dss for d in ds]
        flat_st = [st for sts in stss for st in sts]
        ar = each(lambda d: jnp.concatenate([stack(d['a_t']), stack(d['r_t'])], axis=0).astype(BF16), flat_d)
        bk = each(lambda d: jnp.concatenate([stack(d['b_t']), stack(d['k_t'])], axis=0).astype(BF16), flat_d)
        gram_p = each(_dot_nt, ar, bk)
        a_s_p = each(_dot_nt, ar, flat_st)
        by_group = lambda xs: [xs[q:q + grp] for q in range(0, len(xs), grp)]
        quadrant = lambda gs, r0, c0: spread([g[r0:r0 + n2, c0:c0 + n2] for g in gs])
        vv = each(lambda ds: jnp.concatenate([d['v'] for d in ds for _ in range(2)], axis=0), dss)
        y = each(lambda gs, v, ss: _dot(jnp.where(strict, quadrant(gs, 0, n2), 0.0), v)
                 + jnp.concatenate([s[:n2] for s in ss], axis=0),
                 by_group(gram_p), vv, by_group(a_s_p))
        pw = each(lambda gs: jnp.where(strict, quadrant(gs, 0, 0), 0.0), by_group(gram_p))
        n = 1
        while n < C:
            y = each(lambda p, t: t + _dot(p, t), pw, y)
            n *= 2
            if n < C:
                pw = each(lambda p: _dot(p, p), pw)
        o2 = each(lambda gs, t, v, ss: _dot(
            jnp.concatenate([jnp.where(incl, quadrant(gs, n2, 0), 0.0),
                             jnp.where(incl, quadrant(gs, n2, n2), 0.0)], axis=1),
            jnp.concatenate([t, v], axis=0)) + jnp.concatenate([s[n2:] for s in ss], axis=0),
            by_group(gram_p), y, vv, by_group(a_s_p))
        pick = lambda t, q: jnp.where(left, t[q * n2:q * n2 + C], t[q * n2 + C:(q + 1) * n2])
        upd = each(lambda ds, t: _dot(
            jnp.concatenate([jnp.concatenate([pick(t, q), d['v']], axis=0)
                             for q, d in enumerate(ds)], axis=1).T,
            jnp.concatenate([jnp.concatenate([d['b_end'], d['k_end']], axis=0) for d in ds], axis=1)),
            dss, y)
        os, s_new = [], []
        for ds, sts, o, up in zip(dss, stss, o2, upd):
            os += [pick(o, q) for q in range(grp)]
            s_new += [jnp.where(same_head, st * d['decay_end']
                                + up[q * LANES:(q + 1) * LANES, q * LANES:(q + 1) * LANES], 0.0)
                      for q, (d, st) in enumerate(zip(ds, sts))]
        return os, s_new

    def pairs(i, _):
        ps = [i * RWKV_PAIRS_PER_ITER + u for u in range(RWKV_PAIRS_PER_ITER)]
        offs = [pl.multiple_of(p * LANES, LANES) for p in ps]
        sts = [st_ref[p] for p in ps]
        ds = []
        for off in offs:
            r, k, v, wl, al, g = [ref[0, :, pl.ds(off, LANES)]
                                  for ref in (r_ref, k_ref, v_ref, wl_ref, al_ref, g_ref)]
            prm = prm_ref[:, pl.ds(off, LANES)]
            w0, a0, k_k, k_a, r_k, gn_w, gn_b = [prm[j:j + 1] for j in range(7)]
            a = jax.nn.sigmoid(a0 + al)
            ds.append(dict(r=r, v=v, g=g, a=a, gn_w=gn_w, gn_b=gn_b, r_k=r_k, kkr=k * k_k,
                           lw=-jnp.exp(-_softplus(-(w0 + wl)) - 0.5),
                           kmod=k * (1.0 + (a - 1.0) * k_a)))
        sums = seg_sum(jnp.concatenate([d['kkr'] * d['kkr'] for d in ds]
                                       + [d['r'] * d['kmod'] * d['r_k'] for d in ds], axis=0))
        n_p = len(ds)
        for u, d in enumerate(ds):
            d['kk'] = d['kkr'] * lax.rsqrt(jnp.maximum(sums[u * C:(u + 1) * C], 1e-24))
            d['bonus'] = sums[(n_p + u) * C:(n_p + u + 1) * C] * d['v']
            if masked:
                for name in ('lw', 'kk', 'kmod', 'v'):
                    d[name] = jnp.where(rows_valid, d[name], 0.0)
        lc_all = cumsum_steps(jnp.concatenate([d['lw'] for d in ds], axis=1))
        for u, d in enumerate(ds):
            lc = lc_all[:, u * LANES:(u + 1) * LANES]
            lend = lc[C - 1:C, :]
            kka = d['kk'] * d['a']
            e_inv, e_end = jnp.exp(-lc), jnp.exp(lend - lc)
            d.update(a_t=-d['kk'] * jnp.exp(lc - d['lw']), r_t=d['r'] * jnp.exp(lc),
                     b_t=kka * e_inv, k_t=d['kmod'] * e_inv, b_end=kka * e_end,
                     k_end=d['kmod'] * e_end, decay_end=jnp.exp(lend))
        os, s_news = core([(ds[q:q + grp], sts[q:q + grp]) for q in range(0, n_p, grp)])
        stats = jnp.dot(jnp.concatenate(os + [o * o for o in os], axis=0).astype(BF16), seg_ones,
                        preferred_element_type=F32)
        for u, (p, off, d, o) in enumerate(zip(ps, offs, ds, os)):
            mean = stats[u * C:(u + 1) * C] * (1.0 / hd)
            var = stats[(n_p + u) * C:(n_p + u + 1) * C] * (1.0 / hd) - mean * mean
            on = (o - mean) * lax.rsqrt(var + GN_EPS) * d['gn_w'] + d['gn_b']
            st_ref[p] = s_news[u]
            y_ref[0, :, pl.ds(off, LANES)] = ((on + d['bonus']) * d['g'])[:y_ref.shape[1]].astype(y_ref.dtype)
        return 0

    lax.fori_loop(0, D_MODEL // LANES // RWKV_PAIRS_PER_ITER, pairs, 0)

    @pl.when(c == pl.num_programs(1) - 1)
    def _():
        sf_ref[0] = st_ref[...]


def rwkv_scan(r, k, v, wl, al, g, prm, s0, t_valid):
    b, t, d = r.shape
    C = RWKV_CHUNK
    assert t % C == 0 and d == D_MODEL
    seq = pl.BlockSpec((1, C, d), lambda bi, c: (bi, c, 0))
    st_spec = pl.BlockSpec((1, d // LANES, LANES, LANES), lambda bi, c: (bi, 0, 0, 0))
    y_rows = BF16_ROWS if (t == C and t_valid <= BF16_ROWS) else C
    return pl.pallas_call(
        functools.partial(_rwkv_scan_body, t_valid=t_valid, t_total=t),
        grid=(b, t // C),
        in_specs=[seq] * 6 + [pl.BlockSpec((8, d), lambda bi, c: (0, 0)), st_spec],
        out_specs=[pl.BlockSpec((1, y_rows, d), lambda bi, c: (bi, c, 0)), st_spec],
        out_shape=[jax.ShapeDtypeStruct((b, t // C * y_rows, d), BF16),
                   jax.ShapeDtypeStruct((b, d // LANES, LANES, LANES), F32)],
        scratch_shapes=[pltpu.VMEM((d // LANES, LANES, LANES), F32)],
        compiler_params=_params("parallel", "arbitrary"),
        name="rwkv_scan",
    )(r, k, v, wl, al, g, prm, s0)


def _conv_body(x_ref, prev_ref, w_ref, b_ref, o_ref, buf_ref, *, tt):
    halo = 8

    @pl.when(pl.program_id(2) == 0)
    def _():
        buf_ref[0:halo, :] = prev_ref[0]

    buf_ref[halo:halo + tt, :] = x_ref[0]
    for c in range(x_ref.shape[2] // LANES):
        sl = slice(c * LANES, (c + 1) * LANES)
        acc = b_ref[:, sl] + x_ref[0, :, sl] * w_ref[SSM_CONV - 1:SSM_CONV, sl]
        for back in range(1, SSM_CONV):
            tap = SSM_CONV - 1 - back
            acc = acc + buf_ref[halo - back:halo - back + tt, sl] * w_ref[tap:tap + 1, sl]
        o_ref[0, :, sl] = _silu(acc)
    buf_ref[0:halo, :] = buf_ref[tt:tt + halo, :]


def causal_conv_silu(x, prev8, w, bias, tt=256, tc=1024):
    b, t, ch = x.shape
    tt = min(tt, t)
    assert t % tt == 0 and ch % tc == 0
    return pl.pallas_call(
        functools.partial(_conv_body, tt=tt),
        grid=(b, ch // tc, t // tt),
        in_specs=[pl.BlockSpec((1, tt, tc), lambda bi, ci, ti: (bi, ti, ci)),
                  pl.BlockSpec((1, 8, tc), lambda bi, ci, ti: (bi, 0, ci)),
                  pl.BlockSpec((SSM_CONV, tc), lambda bi, ci, ti: (0, ci)),
                  pl.BlockSpec((1, tc), lambda bi, ci, ti: (0, ci))],
        out_specs=pl.BlockSpec((1, tt, tc), lambda bi, ci, ti: (bi, ti, ci)),
        out_shape=jax.ShapeDtypeStruct((b, t, ch), F32),
        scratch_shapes=[pltpu.VMEM((tt + 8, tc), F32)],
        compiler_params=_params("parallel", "parallel", "arbitrary"),
        name="causal_conv_silu",
    )(x, prev8, w, bias.reshape(1, ch))


def _ssd_body(xa_ref, z_ref, dt_ref, dtt_ref, dtb_ref, dtbt_ref, al_ref, alt_ref, dsk_ref, nw_ref,
              h0_ref, y_ref, hf_ref, h_ref, *, t_valid, t_total):
    c = pl.program_id(1)
    Q, P = SSM_CHUNK, SSM_HEADDIM
    per_grp = SSM_HEADS // SSM_GROUPS
    gw = per_grp * P

    @pl.when(c == 0)
    def _():
        h_ref[...] = h0_ref[0]

    tri = _iota((Q, Q), 1) <= _iota((Q, Q), 0)
    tri_b = jnp.where(tri, 1.0, 0.0).astype(BF16)
    upp_b = jnp.where(_iota((Q, Q), 0) <= _iota((Q, Q), 1), 1.0, 0.0).astype(BF16)
    left = _iota((Q, LANES), 1) < P
    masked = t_valid < t_total

    def one_hot_rows(width, per):
        head = _iota((3 * per_grp, width), 0) % per_grp
        return jnp.where(_iota((3 * per_grp, width), 1) // per == head, 1.0, 0.0).astype(BF16)

    to_heads = one_hot_rows(gw, P)
    to_slabs = one_hot_rows(per_grp * Q, Q)

    def group(g, _):
        dt = _softplus(dt_ref[0, g] + dtb_ref[g])
        dtt = _softplus(dtt_ref[0, g] + dtbt_ref[g])
        if masked:
            dt = jnp.where(c * Q + _iota(dt.shape, 0) < t_valid, dt, 0.0)
            dtt = jnp.where(c * Q + _iota(dtt.shape, 1) < t_valid, dtt, 0.0)
        cum = _dot_exact_lhs(tri_b, dt * -jnp.exp(al_ref[g]))
        cumt = _dot_exact_rhs(dtt * -jnp.exp(alt_ref[g]), upp_b)
        spread_cols = lambda t, e3: jnp.dot(jnp.concatenate(_split3(t), axis=1), e3,
                                            preferred_element_type=F32)
        ecum_w = spread_cols(jnp.exp(cum), to_heads)
        tail_w = spread_cols(jnp.exp(cum[Q - 1:Q, :] - cum) * dt, to_heads)
        cum_w = spread_cols(cum, to_slabs)
        boff = pl.multiple_of(SSM_DINNER + g * SSM_STATE, SSM_STATE)
        coff = pl.multiple_of(SSM_DINNER + SSM_GROUPS * SSM_STATE + g * SSM_STATE, SSM_STATE)
        roff = pl.multiple_of(g * gw, gw)
        bm = xa_ref[0, :, pl.ds(boff, SSM_STATE)]
        cm = xa_ref[0, :, pl.ds(coff, SSM_STATE)]
        cb = _dot_nt(cm, bm)
        hg = h_ref[pl.ds(roff, gw), :]
        y_state = _dot_nt(cm, hg)
        xg = xa_ref[0, :, pl.ds(roff, gw)]
        xps = [xg[:, q * LANES:(q + 1) * LANES] for q in range(gw // LANES)]

        def head_matrix(hh):
            seg = cum_w[:, hh * Q:(hh + 1) * Q] - cumt[hh:hh + 1, :]
            dec = jnp.where(tri, jnp.exp(jnp.where(tri, seg, 0.0)), 0.0)
            return (cb * dec * dtt[hh:hh + 1, :]).astype(BF16)

        mats = [head_matrix(hh) for hh in range(per_grp)]
        prods = [_dot(mats[hh], xps[hh // 2]) for hh in range(per_grp)]
        ys, xts = [], []
        for q, xp in enumerate(xps):
            sl = slice(q * LANES, (q + 1) * LANES)
            ys.append(jnp.where(left, prods[2 * q], prods[2 * q + 1]) + y_state[:, sl] * ecum_w[:, sl])
            xts.append(xp * tail_w[:, sl])
        upd = _dot(jnp.concatenate(xts, axis=1).T, bm)
        for hh in range(per_grp):
            rows = slice(hh * P, (hh + 1) * P)
            h_ref[pl.ds(pl.multiple_of(roff + hh * P, P), P), :] = (
                hg[rows] * jnp.exp(cumt[hh:hh + 1, Q - 1:Q]) + upd[rows])
        yg = jnp.concatenate(ys, axis=1)
        yg = (yg + xg * dsk_ref[:, pl.ds(roff, gw)]) * _silu(z_ref[0, :, pl.ds(roff, gw)])
        ms = jnp.mean(yg * yg, axis=-1, keepdims=True)
        y_ref[0, :, pl.ds(roff, gw)] = (
            yg * lax.rsqrt(ms + SSM_NORM_EPS) * nw_ref[:, pl.ds(roff, gw)]).astype(y_ref.dtype)
        return 0

    lax.fori_loop(0, SSM_GROUPS, group, 0)

    @pl.when(c == pl.num_programs(1) - 1)
    def _():
        hf_ref[0] = h_ref[...]


def ssd_scan(xa, z, dt_raw, dt_bias, a_log, d_skip, norm_w, h0, t_valid):
    b, t, _ = xa.shape
    Q = SSM_CHUNK
    per_grp = SSM_HEADS // SSM_GROUPS
    assert t % Q == 0
    dt_g = dt_raw.reshape(b, t, SSM_GROUPS, per_grp).transpose(0, 2, 1, 3)
    dt_gt = dt_g.transpose(0, 1, 3, 2)
    grp = lambda p: p.reshape(SSM_GROUPS, 1, per_grp)
    grp_t = lambda p: p.reshape(SSM_GROUPS, per_grp, 1)
    full3 = lambda s: pl.BlockSpec(s, lambda bi, c: (0, 0, 0))
    lanes = lambda w: pl.BlockSpec((1, w), lambda bi, c: (0, 0))
    st_spec = pl.BlockSpec((1, SSM_DINNER, SSM_STATE), lambda bi, c: (bi, 0, 0))
    return pl.pallas_call(
        functools.partial(_ssd_body, t_valid=t_valid, t_total=t),
        grid=(b, t // Q),
        in_specs=[pl.BlockSpec((1, Q, SSM_CONV_DIM), lambda bi, c: (bi, c, 0)),
                  pl.BlockSpec((1, Q, SSM_DINNER), lambda bi, c: (bi, c, 0)),
                  pl.BlockSpec((1, SSM_GROUPS, Q, per_grp), lambda bi, c: (bi, 0, c, 0)),
                  pl.BlockSpec((1, SSM_GROUPS, per_grp, Q), lambda bi, c: (bi, 0, 0, c)),
                  full3((SSM_GROUPS, 1, per_grp)), full3((SSM_GROUPS, per_grp, 1)),
                  full3((SSM_GROUPS, 1, per_grp)), full3((SSM_GROUPS, per_grp, 1)),
                  lanes(SSM_DINNER), lanes(SSM_DINNER), st_spec],
        out_specs=[pl.BlockSpec((1, Q, SSM_DINNER), lambda bi, c: (bi, c, 0)), st_spec],
        out_shape=[jax.ShapeDtypeStruct((b, t, SSM_DINNER), BF16),
                   jax.ShapeDtypeStruct((b, SSM_DINNER, SSM_STATE), F32)],
        scratch_shapes=[pltpu.VMEM((SSM_DINNER, SSM_STATE), F32)],
        compiler_params=_params("parallel", "arbitrary"),
        name="ssd_scan",
    )(xa, z, dt_g, dt_gt, grp(dt_bias), grp_t(dt_bias), grp(a_log), grp_t(a_log),
      jnp.repeat(d_skip, SSM_HEADDIM).reshape(1, SSM_DINNER), norm_w.reshape(1, SSM_DINNER), h0)


def _pad_time(x3, mult):
    t = x3.shape[1]
    tp = -(-t // mult) * mult
    return x3 if tp == t else jnp.pad(x3, ((0, 0), (0, tp - t), (0, 0)))


def _rwkv_mixer(xf, b, t, ln, shift0, wkv0, W):
    d = D_MODEL
    x_p = _pad_time(xf.reshape(b, t, d), RWKV_CHUNK)
    tp = x_p.shape[1]
    mixes, xn_tail = rwkv_mix(x_p, ln, shift0, W['rwkv_mu'])
    xr, xw, xk, xv, xa, xg = [m.reshape(b * tp, d) for m in mixes]
    shift_out = xn_tail[:, (t - 1) % xn_tail.shape[1]]
    r = linear(xr, W['rwkv_wr'])
    k = linear(xk, W['rwkv_wk'])
    v = linear(xv, W['rwkv_wv'])
    wl = linear(linear(xw, W['rwkv_w1'], act=jnp.tanh, out_dtype=BF16), W['rwkv_w2'])
    al = linear(linear(xa, W['rwkv_a1'], out_dtype=BF16), W['rwkv_a2'])
    g = linear(linear(xg, W['rwkv_g1'], act=jax.nn.sigmoid, out_dtype=BF16), W['rwkv_g2'])
    prm = jnp.stack([W['rwkv_w0'], W['rwkv_a0'], W['rwkv_kk'], W['rwkv_ka'], W['rwkv_rk'].reshape(d),
                     W['rwkv_gn_w'], W['rwkv_gn_b'], jnp.zeros((d,), F32)])
    n_pair = d // LANES
    s4 = wkv0.astype(F32).reshape(b, n_pair, 2, RWKV_HD, RWKV_HD)
    zero = jnp.zeros_like(s4[:, :, 0])
    s0 = jnp.concatenate([jnp.concatenate([s4[:, :, 0], zero], axis=-1),
                          jnp.concatenate([zero, s4[:, :, 1]], axis=-1)], axis=-2)
    to3 = lambda a: a.reshape(b, tp, d)
    y, sf = rwkv_scan(to3(r), to3(k), to3(v), to3(wl), to3(al), to3(g), prm, s0, t_valid=t)
    wkv = jnp.stack([sf[:, :, :RWKV_HD, :RWKV_HD], sf[:, :, RWKV_HD:, RWKV_HD:]], axis=2)
    y = y[:, :t].reshape(b * t, d)
    return linear(y, W['rwkv_wo'], res=xf), shift_out, wkv.reshape(b, d // RWKV_HD, RWKV_HD, RWKV_HD)


def _sb_mixer(xf, b, t, ln, sb_past, W):
    d = D_MODEL
    xn = rmsnorm(xf, ln)
    qdt = BF16 if sb_past is None else F32
    q = linear(xn, W['sb_wqkv'], col0=0, n=d, gain=W['sb_gq'], hw=SB_HD, out_dtype=qdt)
    k = linear(xn, W['sb_wqkv'], col0=d, n=d, gain=W['sb_gk'], hw=SB_HD)
    v = linear(xn, W['sb_wqkv'], col0=2 * d, n=d)
    to3 = lambda a: a.reshape(b, t, d)
    if sb_past is None:
        o = sb_attention_prompt(to3(q), to3(k), to3(v), W['sb_bias'])
    else:
        pool_k, pool_v, table = sb_past
        assert t == 1
        heads = lambda a: a.reshape(b, SB_HEADS, SB_HD)
        o = sb_attention_decode(heads(q), heads(k), heads(v), pool_k, pool_v, table, W['sb_bias'])
    xf = linear(o.reshape(b * t, d), W['sb_wo'], res=xf)
    return xf, k.reshape(b, t, SB_HEADS, SB_HD), v.reshape(b, t, SB_HEADS, SB_HD)


def _dil_mixer(xf, b, t, pos0, ln, dil_bufs, W):
    n_grp = len(DIL_PATTERNS)
    gd = n_grp * DIL_HEADS * DIL_HD
    xn = rmsnorm(xf, ln)
    rows = t if t >= 8 else b * t
    rope = rope_tables(pos0 + (jnp.arange(rows, dtype=jnp.int32) % t))
    tm = min(rows, LINEAR_TM)
    q = linear(xn, W['dil_wqkv'], col0=0, n=gd, gain=W['dil_gq'], hw=DIL_HD, rope=rope, tm=tm)
    k = linear(xn, W['dil_wqkv'], col0=gd, n=gd, gain=W['dil_gk'], hw=DIL_HD, rope=rope, tm=tm)
    v = linear(xn, W['dil_wqkv'], col0=2 * gd, n=gd, tm=tm)
    to3 = lambda a: a.reshape(b, t, gd)
    if dil_bufs is None:
        o = dil_attention_prompt(to3(q), to3(k), to3(v))
    else:
        assert t == 1
        heads = lambda a: a.reshape(b, n_grp, DIL_HEADS, DIL_HD)
        o = dil_attention_decode(heads(q), heads(k), heads(v), dil_bufs)
    xf = linear(o.reshape(b * t, DIL_HEADS * DIL_HD), W['dil_wo'], res=xf)
    k5 = k.reshape(b, t, n_grp, DIL_HEADS, DIL_HD)
    v5 = v.reshape(b, t, n_grp, DIL_HEADS, DIL_HD)
    states = []
    for g, (win, _) in enumerate(DIL_PATTERNS):
        keep = min(win, t)
        states += [k5[:, t - keep:, g], v5[:, t - keep:, g]]
    return xf, states


def _ssd_mixer(xf, b, t, ln, conv0, h0, W):
    d = D_MODEL
    xn = _pad_time(rmsnorm(xf, ln).reshape(b, t, d), SSM_CHUNK)
    tp = xn.shape[1]
    xn = xn.reshape(b * tp, d)
    z = linear(xn, W['ssm_win'], col0=0, n=SSM_DINNER)
    xbc = linear(xn, W['ssm_win'], col0=SSM_DINNER, n=SSM_CONV_DIM).reshape(b, tp, SSM_CONV_DIM)
    dt_raw = linear(xn, W['ssm_win'][:, SSM_DINNER + SSM_CONV_DIM:])
    prev8 = jnp.pad(conv0.astype(F32), ((0, 0), (8 - (SSM_CONV - 1), 0), (0, 0)))
    xa = causal_conv_silu(xbc, prev8, W['ssm_conv_w'], W['ssm_conv_b'])
    y, hf = ssd_scan(xa, z.reshape(b, tp, SSM_DINNER), dt_raw.reshape(b, tp, SSM_HEADS),
                     W['ssm_dt_bias'], W['ssm_a_log'], W['ssm_d'], W['ssm_norm_w'],
                     h0.astype(F32).reshape(b, SSM_DINNER, SSM_STATE), t_valid=t)
    xf = linear(y[:, :t].reshape(b * t, SSM_DINNER), W['ssm_wout'], res=xf)
    conv_state = jnp.concatenate([conv0.astype(F32), xbc[:, :t]], axis=1)[:, t:]
    return xf, conv_state, hf.reshape(b, SSM_HEADS, SSM_HEADDIM, SSM_STATE)


def _mixer_and_memory(xf, i, G, W):
    b, t, d = G['shape']
    st = G['out']
    kind = i % 4
    ln = W['ln_mix'][i]
    if kind == 0:
        xf, st['rwkv_shift'], st['rwkv_wkv'] = _rwkv_mixer(xf, b, t, ln, G['shift0'], G['wkv0'], W)
    elif kind == 1:
        xf, st['sb_k'], st['sb_v'] = _sb_mixer(xf, b, t, ln, G['sb_past'], W)
    elif kind == 2:
        xf, st['dil'] = _dil_mixer(xf, b, t, G['pos0'], ln, G['dil_bufs'], W)
    else:
        xf, st['ssm_conv'], st['ssm_h'] = _ssd_mixer(xf, b, t, ln, G['conv0'], G['h0'], W)
    xn = rmsnorm(xf, W['ln_mem'][i])
    q = linear(xn, W['mem_wq'], layer=i, gain=W['mem_gq'][i], hw=MEM_HD, out_dtype=BF16)
    q = _pad_time(q.reshape(b, t, d), 8)
    o = mem_attention(q, G['mem_k'], G['mem_v'], i)[:, :t].reshape(b * t, d)
    return linear(o, W['mem_wo'], layer=i, res=xf)


def _run_groups(P, S, W):
    xp, xs = [G['x'].reshape(-1, D_MODEL) for G in (P, S)]
    for i in range(DEPTH):
        xp, xs = ffn(xp, xs, W['ln_ffn1'][i], W['ffn1_gate'], W['ffn1_up'], W['ffn1_down'], i)
        xp = _mixer_and_memory(xp, i, P, W)
        xs = _mixer_and_memory(xs, i, S, W)
        xp, xs = ffn(xp, xs, W['ln_ffn2'][i], W['ffn2_gate'], W['ffn2_up'], W['ffn2_down'], i)
    return xp.reshape(P['shape']), xs.reshape(S['shape'])


def kernel(x_prompt, x_sample, state_rwkv_shift, state_rwkv_wkv, cache_sb_k, cache_sb_v, cache_dil0_k, cache_dil0_v, cache_dil1_k, cache_dil1_v, cache_dil2_k, cache_dil2_v, state_ssm_conv, state_ssm_h, cache_mem_k, cache_mem_v, page_table, mem_prompt, ln_ffn1, ffn1_gate, ffn1_up, ffn1_down, ln_mix, ln_mem, mem_wq, mem_gq, mem_wk, mem_gk, mem_wv, mem_wo, ln_ffn2, ffn2_gate, ffn2_up, ffn2_down, rwkv_mu, rwkv_wr, rwkv_wk, rwkv_wv, rwkv_wo, rwkv_w0, rwkv_w1, rwkv_w2, rwkv_a0, rwkv_a1, rwkv_a2, rwkv_g1, rwkv_g2, rwkv_kk, rwkv_ka, rwkv_rk, rwkv_gn_w, rwkv_gn_b, sb_wqkv, sb_gq, sb_gk, sb_bias, sb_wo, dil_wqkv, dil_gq, dil_gk, dil_wo, ssm_win, ssm_conv_w, ssm_conv_b, ssm_dt_bias, ssm_a_log, ssm_d, ssm_norm_w, ssm_wout):
    W = dict(locals())
    bp, _, d = x_prompt.shape
    n_mem = mem_prompt.shape[1]
    past_len = page_table.shape[1] * cache_sb_k.shape[1]

    mem_rows = mem_prompt.reshape(bp * n_mem, d)
    p_mem_k = jnp.stack([linear(mem_rows, mem_wk, layer=i, gain=mem_gk[i], hw=MEM_HD)
                         for i in range(DEPTH)]).reshape(DEPTH, bp, n_mem, d)
    p_mem_v = jnp.stack([linear(mem_rows, mem_wv, layer=i)
                         for i in range(DEPTH)]).reshape(DEPTH, bp, n_mem, d)
    P = dict(x=x_prompt, shape=x_prompt.shape, pos0=0, mem_k=p_mem_k, mem_v=p_mem_v,
             shift0=jnp.zeros((bp, d), F32), wkv0=jnp.zeros((bp, d // RWKV_HD, RWKV_HD, RWKV_HD), F32),
             sb_past=None, dil_bufs=None,
             conv0=jnp.zeros((bp, SSM_CONV - 1, SSM_CONV_DIM), F32),
             h0=jnp.zeros((bp, SSM_HEADS, SSM_HEADDIM, SSM_STATE), F32), out={})
    merged = lambda a: a.reshape(a.shape[:3] + (MEM_HEADS * MEM_HD,))
    S = dict(x=x_sample, shape=x_sample.shape, pos0=past_len,
             mem_k=merged(cache_mem_k), mem_v=merged(cache_mem_v),
             shift0=state_rwkv_shift, wkv0=state_rwkv_wkv, sb_past=(cache_sb_k, cache_sb_v, page_table),
             dil_bufs=(cache_dil0_k, cache_dil0_v, cache_dil1_k, cache_dil1_v, cache_dil2_k, cache_dil2_v),
             conv0=state_ssm_conv, h0=state_ssm_h, out={})
    y_p, y_s = _run_groups(P, S, W)
    sp, ss = P['out'], S['out']

    mem_shape = (DEPTH, bp, n_mem, MEM_HEADS, MEM_HD)
    dil = []
    for g in range(len(DIL_PATTERNS)):
        dil += [sp['dil'][2 * g], sp['dil'][2 * g + 1], ss['dil'][2 * g], ss['dil'][2 * g + 1]]
    return (y_p, y_s,
            sp['rwkv_shift'], ss['rwkv_shift'], sp['rwkv_wkv'], ss['rwkv_wkv'],
            sp['sb_k'], sp['sb_v'], ss['sb_k'], ss['sb_v'],
            *dil,
            sp['ssm_conv'], ss['ssm_conv'], sp['ssm_h'], ss['ssm_h'],
            p_mem_k.reshape(mem_shape), p_mem_v.reshape(mem_shape))
```

```python
import functools
import itertools
import math

import jax
import jax.numpy as jnp
from jax import lax
from jax.experimental import pallas as pl
from jax.experimental.pallas import tpu as pltpu

F32 = jnp.float32
BF16 = jnp.bfloat16

D_MODEL = 2048
DEPTH = 4
NORM_EPS = 1e-6
PAGE_SIZE = 128
MEM_HEADS = 4
MEM_HD = D_MODEL // MEM_HEADS
RWKV_HD = 64
GN_EPS = 64e-5
SB_HD = 128
SB_HEADS = D_MODEL // SB_HD
DIL_PATTERNS = ((128, 1), (512, 4), (2048, 16))
DIL_HD = 128
DIL_HEADS = 8
ROPE_THETA = 500000.0
ROPE_DIM = DIL_HD // 4
SSM_DINNER = 2 * D_MODEL
SSM_HEADDIM = 64
SSM_HEADS = SSM_DINNER // SSM_HEADDIM
SSM_STATE = 128
SSM_GROUPS = 8
SSM_CONV = 4
SSM_CONV_DIM = SSM_DINNER + 2 * SSM_GROUPS * SSM_STATE
SSM_CHUNK = 128
SSM_NORM_EPS = 1e-5

LANES = 128
BF16_ROWS = 16
V7X_VMEM_BYTES = 64 << 20
VMEM_LIMIT = V7X_VMEM_BYTES - (8 << 20)
FFN_VMEM_LIMIT = V7X_VMEM_BYTES - (4 << 20)
DIL_BLOCKS_PER_ITER = 8
LINEAR_TM = 1024
LINEAR_TN = 512
LINEAR_TN_WIDE = 1024
RWKV_CHUNK = 64
SSD_GROUPS_PER_ITER = 4
RWKV_PAIRS_PER_ITER = 16
RWKV_PAIRS_PER_DOT = 2
NEG_BIG = -1e30


def _params(*sem, vmem=VMEM_LIMIT):
    return pltpu.CompilerParams(dimension_semantics=sem, vmem_limit_bytes=vmem)


def _dot(a, b):
    return jnp.dot(a.astype(BF16), b.astype(BF16), preferred_element_type=F32)


def _dot_nt(a, b):
    return lax.dot_general(a.astype(BF16), b.astype(BF16), (((1,), (1,)), ((), ())),
                           preferred_element_type=F32)


def _split3(x):
    hi = x.astype(BF16)
    r1 = x - hi.astype(F32)
    mid = r1.astype(BF16)
    lo = (r1 - mid.astype(F32)).astype(BF16)
    return hi, mid, lo


def _dot_exact_rhs(x, m_bf16):
    hi, mid, lo = _split3(x)
    d = lambda p: jnp.dot(p, m_bf16, preferred_element_type=F32)
    return d(hi) + d(mid) + d(lo)


def _dot_exact_lhs(m_bf16, x):
    hi, mid, lo = _split3(x)
    d = lambda p: jnp.dot(m_bf16, p, preferred_element_type=F32)
    return d(hi) + d(mid) + d(lo)


def _iota(shape, dim):
    return lax.broadcasted_iota(jnp.int32, shape, dim)


def _softplus(x):
    return jnp.maximum(x, 0.0) + jnp.log1p(jnp.exp(-jnp.abs(x)))


def _log_sigmoid(x):
    return jnp.minimum(x, 0.0) - jnp.log(1.0 + jnp.exp(-jnp.abs(x)))


def _silu(x):
    return x * jax.nn.sigmoid(x)


def _linear_body(*refs, has_gain, has_rope, has_res, act, hw, res_scale):
    it = iter(refs)
    x_ref, w_ref = next(it), next(it)
    gain_ref = next(it) if has_gain else None
    cos_ref, sin_ref = (next(it), next(it)) if has_rope else (None, None)
    res_ref = next(it) if has_res else None
    o_ref, wbf_ref = next(it), next(it)

    @pl.when(pl.program_id(1) == 0)
    def _():
        wbf_ref[...] = w_ref[...].astype(BF16)

    acc = jnp.dot(x_ref[...].astype(BF16), wbf_ref[...], preferred_element_type=F32)
    if act is not None:
        acc = act(acc)
    if has_res:
        acc = res_ref[...] + res_scale * acc
    if has_gain:
        for s in range(acc.shape[1] // hw):
            y = acc[:, s * hw:(s + 1) * hw]
            ms = jnp.mean(y * y, axis=-1, keepdims=True)
            y = y * lax.rsqrt(ms + NORM_EPS) * gain_ref[...]
            if has_rope:
                lane = _iota(y.shape, 1)
                half = ROPE_DIM // 2
                rot = jnp.where(lane < half, pltpu.roll(y, hw - half, 1), pltpu.roll(y, half, 1))
                y = y * cos_ref[...] + rot * sin_ref[...]
            o_ref[:, s * hw:(s + 1) * hw] = y.astype(o_ref.dtype)
    else:
        o_ref[...] = acc.astype(o_ref.dtype)


def linear(x, w, *, col0=0, n=None, out_dtype=F32, gain=None, hw=None, rope=None, res=None,
           res_scale=1.0, act=None, layer=None, tm=LINEAR_TM, tn=None, name="linear"):
    m, k = x.shape
    n = w.shape[-1] - col0 if n is None else n
    if tn is None:
        tn = LINEAR_TN_WIDE if (n % LINEAR_TN_WIDE == 0 and col0 % LINEAR_TN_WIDE == 0
                                and k * LINEAR_TN_WIDE * 10 <= VMEM_LIMIT // 3 * 2) else LINEAR_TN
    tm, tn = min(tm, m), min(tn, n)
    assert m % tm == 0 and n % tn == 0 and col0 % tn == 0 and w.shape[-2] == k
    cb = col0 // tn
    if layer is None:
        w_spec = pl.BlockSpec((k, tn), lambda j, i: (0, j + cb))
    else:
        w_spec = pl.BlockSpec((None, k, tn), lambda j, i: (layer, 0, j + cb))
    in_specs = [pl.BlockSpec((tm, k), lambda j, i: (i, 0)), w_spec]
    args = [x, w]
    if gain is not None:
        assert tn % hw == 0
        in_specs.append(pl.BlockSpec((1, hw), lambda j, i: (0, 0)))
        args.append(gain.reshape(1, hw).astype(F32))
    if rope is not None:
        cos, sin = rope
        nt = cos.shape[0] // tm
        assert hw == LANES and cos.shape[0] % tm == 0
        in_specs += [pl.BlockSpec((tm, LANES), lambda j, i: (i % nt, 0))] * 2
        args += [cos, sin]
    if res is not None:
        in_specs.append(pl.BlockSpec((tm, tn), lambda j, i: (i, j)))
        args.append(res)
    body = functools.partial(_linear_body, has_gain=gain is not None, has_rope=rope is not None,
                             has_res=res is not None, act=act, hw=hw, res_scale=res_scale)
    return pl.pallas_call(
        body,
        grid=(n // tn, m // tm),
        in_specs=in_specs,
        out_specs=pl.BlockSpec((tm, tn), lambda j, i: (i, j)),
        out_shape=jax.ShapeDtypeStruct((m, n), out_dtype),
        scratch_shapes=[pltpu.VMEM((k, tn), BF16)],
        compiler_params=_params("parallel", "arbitrary"),
        name=name,
    )(*args)


def _rmsnorm_body(x_ref, g_ref, o_ref):
    x = x_ref[...]
    ms = jnp.mean(x * x, axis=-1, keepdims=True)
    o_ref[...] = (x * lax.rsqrt(ms + NORM_EPS) * g_ref[...]).astype(o_ref.dtype)


def rmsnorm(x, g, out_dtype=BF16, tm=512):
    m, d = x.shape
    tm = min(tm, m)
    assert m % tm == 0
    return pl.pallas_call(
        _rmsnorm_body,
        grid=(m // tm,),
        in_specs=[pl.BlockSpec((tm, d), lambda i: (i, 0)), pl.BlockSpec((1, d), lambda i: (0, 0))],
        out_specs=pl.BlockSpec((tm, d), lambda i: (i, 0)),
        out_shape=jax.ShapeDtypeStruct((m, d), out_dtype),
        compiler_params=_params("parallel"),
        name="rmsnorm",
    )(x, g.reshape(1, d))


def _ffn_body(x_ref, xr_ref, g_ref, wg_ref, wu_ref, wd_ref, o_ref, or_ref, xn_ref, *, tm):
    i, f = pl.program_id(0), pl.program_id(1)

    def start(src_ref, rows, dst_ref):
        x = src_ref[...]
        ms = jnp.mean(x * x, axis=-1, keepdims=True)
        xn_ref[rows, :] = (x * lax.rsqrt(ms + NORM_EPS) * g_ref[...]).astype(BF16)
        dst_ref[...] = x

    @pl.when(f == 0)
    def _():
        start(x_ref, slice(0, tm), o_ref)

    @pl.when((f == 0) & (i == 0))
    def _():
        start(xr_ref, slice(tm, tm + xr_ref.shape[0]), or_ref)

    xn = xn_ref[...]
    gate = jnp.dot(xn, wg_ref[...].astype(BF16), preferred_element_type=F32)
    up = jnp.dot(xn, wu_ref[...].astype(BF16), preferred_element_type=F32)
    h = (0.5 * _silu(gate) * up).astype(BF16)
    y = jnp.dot(h, wd_ref[...].astype(BF16), preferred_element_type=F32)
    o_ref[...] += y[:tm]
    or_ref[...] += jnp.where(i == 0, y[tm:], 0.0)


def ffn(x, x_rider, g, w_gate, w_up, w_down, layer, tm=1024, tf=512):
    m, d = x.shape
    mr = x_rider.shape[0]
    mr_pad = -(-mr // BF16_ROWS) * BF16_ROWS
    x_rider = jnp.pad(x_rider, ((0, mr_pad - mr), (0, 0)))
    f = w_gate.shape[-1]
    tm = min(tm, m)
    assert m % tm == 0 and f % tf == 0
    out, out_rider = pl.pallas_call(
        functools.partial(_ffn_body, tm=tm),
        grid=(m // tm, f // tf),
        in_specs=[pl.BlockSpec((tm, d), lambda i, j: (i, 0), pipeline_mode=pl.Buffered(1)),
                  pl.BlockSpec((mr_pad, d), lambda i, j: (0, 0)),
                  pl.BlockSpec((1, d), lambda i, j: (0, 0)),
                  pl.BlockSpec((None, d, tf), lambda i, j: (layer, 0, j)),
                  pl.BlockSpec((None, d, tf), lambda i, j: (layer, 0, j)),
                  pl.BlockSpec((None, tf, d), lambda i, j: (layer, j, 0))],
        out_specs=[pl.BlockSpec((tm, d), lambda i, j: (i, 0)),
                   pl.BlockSpec((mr_pad, d), lambda i, j: (0, 0))],
        out_shape=[jax.ShapeDtypeStruct((m, d), F32), jax.ShapeDtypeStruct((mr_pad, d), F32)],
        scratch_shapes=[pltpu.VMEM((tm + mr_pad, d), BF16)],
        compiler_params=_params("arbitrary", "arbitrary", vmem=FFN_VMEM_LIMIT),
        name="ffn",
    )(x, x_rider, g.reshape(1, d), w_gate, w_up, w_down)
    return out, out_rider[:mr]


def _memattn_body(q_ref, *refs, scale, head_axis):
    each = lambda f, *cols: [f(*args) for args in zip(*cols)]
    lanes = [slice(h * MEM_HD, (h + 1) * MEM_HD) for h in range(MEM_HEADS)]
    k_ref, v_ref, o_ref = refs
    if head_axis:
        ks = [k_ref[0, 0, :, h, :] for h in range(MEM_HEADS)]
        vs = [v_ref[0, 0, :, h, :] for h in range(MEM_HEADS)]
    else:
        ks = [k_ref[0, 0, :, sl] for sl in lanes]
        vs = [v_ref[0, 0, :, sl] for sl in lanes]
    s = each(lambda sl, k: _dot_nt(q_ref[0, :, sl], k) * scale, lanes, ks)
    m = each(lambda t: jnp.max(t, axis=-1, keepdims=True), s)
    p = each(lambda t, mm: jnp.exp(t - mm), s, m)
    l = each(lambda t: jnp.sum(t, axis=-1, keepdims=True), p)
    o = each(lambda t, v, ll: _dot(t, v) / ll, p, vs, l)
    for sl, t in zip(lanes, o):
        o_ref[0, :, sl] = t.astype(o_ref.dtype)


def mem_attention(q, mem_k, mem_v, layer, tq=512):
    b, t, d = q.shape
    n_mem = mem_k.shape[2]
    tq = min(tq, t)
    assert t % tq == 0
    head_axis = mem_k.ndim == 5
    if head_axis:
        kv_specs = [pl.BlockSpec((1, 1, n_mem, MEM_HEADS, MEM_HD), lambda bi, ti: (layer, bi, 0, 0, 0))] * 2
        kv_args = [mem_k, mem_v]
    else:
        kv_specs = [pl.BlockSpec((1, 1, n_mem, d), lambda bi, ti: (layer, bi, 0, 0))] * 2
        kv_args = [mem_k, mem_v]
    return pl.pallas_call(
        functools.partial(_memattn_body, scale=MEM_HD ** -0.5, head_axis=head_axis),
        grid=(b, t // tq),
        in_specs=[pl.BlockSpec((1, tq, d), lambda bi, ti: (bi, ti, 0))] + kv_specs,
        out_specs=pl.BlockSpec((1, tq, d), lambda bi, ti: (bi, ti, 0)),
        out_shape=jax.ShapeDtypeStruct((b, t, d), BF16),
        compiler_params=_params("parallel", "parallel"),
        name="mem_attention",
    )(q, *kv_args)


def rope_tables(pos):
    half = ROPE_DIM // 2
    inv_freq = ROPE_THETA ** (-jnp.arange(half, dtype=F32) / half)
    ang = pos.astype(F32)[:, None] * inv_freq[None, :]
    cos, sin = jnp.cos(ang), jnp.sin(ang)
    rest = DIL_HD - ROPE_DIM
    n = pos.shape[0]
    cos_t = jnp.concatenate([cos, cos, jnp.ones((n, rest), F32)], axis=1)
    sin_t = jnp.concatenate([-sin, sin, jnp.zeros((n, rest), F32)], axis=1)
    return cos_t, sin_t


def _sb_tiles(qs, ks, vs, biases, causal, carries, accs, upper):
    each = lambda f, *cols: [f(*args) for args in zip(*cols)]
    mask = (lambda t: t) if causal is None else (lambda t: jnp.where(causal, t, 0.0))
    z = each(lambda q, k, b: _dot_nt(q, k) * (SB_HD ** -0.5) + b, qs, ks, biases)
    ls = each(_log_sigmoid, z)
    log_keep = each(lambda l, t: mask(l - t), ls, z)
    hi = each(lambda t: t.astype(BF16), log_keep)
    lo = each(lambda t, h: (t - h.astype(F32)).astype(BF16), log_keep, hi)
    local = each(lambda h, l: jnp.dot(jnp.concatenate([h, l], axis=0), upper, preferred_element_type=F32),
                 hi, lo)
    rows = qs[0].shape[0]
    att = each(lambda l, s, c: mask(jnp.exp(l + s[:rows] + s[rows:] + c)), ls, local, carries)
    accs = each(lambda a, p, v: a + _dot(p, v), accs, att, vs)
    carries = each(lambda c, t: c + jnp.sum(t, axis=-1, keepdims=True), carries, log_keep)
    return carries, accs


def _upper_ones(n):
    return jnp.where(_iota((n, n), 0) > _iota((n, n), 1), 1.0, 0.0).astype(BF16)


def _sb_prompt_body(bias_ref, q_ref, k_ref, v_ref, o_ref, *, tq, heads):
    hb, qi = pl.program_id(1), pl.program_id(2)
    upper = _upper_ones(tq)
    lanes = [slice(u * SB_HD, (u + 1) * SB_HD) for u in range(heads)]
    qs = [q_ref[0, :, sl] for sl in lanes]
    biases = [bias_ref[hb * heads + u] for u in range(heads)]

    def tiles(j, state, causal):
        start = pl.multiple_of(j * tq, tq)
        ks = [k_ref[0, pl.ds(start, tq), sl] for sl in lanes]
        vs = [v_ref[0, pl.ds(start, tq), sl] for sl in lanes]
        return _sb_tiles(qs, ks, vs, biases, causal, state[0], state[1], upper)

    init = ([jnp.zeros((tq, 1), F32) for _ in lanes], [jnp.zeros((tq, SB_HD), F32) for _ in lanes])
    state = tiles(qi, init, _iota((tq, tq), 1) < _iota((tq, tq), 0))
    _, accs = lax.fori_loop(0, qi, lambda jj, st: tiles(qi - 1 - jj, st, None), state)
    for sl, acc in zip(lanes, accs):
        o_ref[0, :, sl] = acc.astype(o_ref.dtype)


def sb_attention_prompt(q, k, v, bias, tq=256, heads=4):
    b, t, d = q.shape
    tq = min(tq, t)
    w = heads * SB_HD
    assert t % tq == 0 and d % w == 0
    kv_spec = pl.BlockSpec((1, t, w), lambda bi, h, qi: (bi, 0, h))
    io_spec = pl.BlockSpec((1, tq, w), lambda bi, h, qi: (bi, qi, h))
    return pl.pallas_call(
        functools.partial(_sb_prompt_body, tq=tq, heads=heads),
        grid=(b, d // w, t // tq),
        in_specs=[pl.BlockSpec(memory_space=pltpu.SMEM), io_spec, kv_spec, kv_spec],
        out_specs=io_spec,
        out_shape=jax.ShapeDtypeStruct((b, t, d), BF16),
        compiler_params=_params("parallel", "parallel", "arbitrary"),
        name="sb_attention_prompt",
    )(bias, q, k, v)


def _sb_decode_body(table_ref, bias_ref, q_ref, kn_ref, vn_ref, *rest, n_pages, per_step):
    del table_ref
    page_refs = rest[:2 * per_step]
    o_ref, acc_ref, carry_ref = rest[2 * per_step:]
    j = pl.program_id(1)
    nh, cols = SB_HEADS, PAGE_SIZE * SB_HEADS
    n_sub = cols // LANES
    bias = bias_ref[...]
    past_len = n_pages * PAGE_SIZE
    scale = SB_HD ** -0.5
    q = q_ref[0]

    @pl.when(j == 0)
    def _():
        z = jnp.sum(q * kn_ref[0], axis=-1, keepdims=True) * scale + bias
        causal = jnp.full((nh, 1), past_len, jnp.int32) < past_len
        ls = _log_sigmoid(z)
        carry_ref[...] = jnp.where(causal, ls - z, 0.0)
        acc_ref[...] = jnp.where(causal, jnp.exp(ls), 0.0) * vn_ref[0]

    each = lambda f, *c: [f(*args) for args in zip(*c)]
    col = _iota((nh, cols), 1)
    own = (col & (nh - 1)) == _iota((nh, cols), 0)
    upper = _upper_ones(LANES)
    pages = [n_pages - 1 - (j * per_step + u) for u in range(per_step)]
    masks = [own & ((p * PAGE_SIZE + col // nh) < past_len) for p in pages]
    ks = [page_refs[2 * u][...] for u in range(per_step)]
    vs = [page_refs[2 * u + 1][...] for u in range(per_step)]
    z = each(lambda k: _dot_nt(q, k) * scale + bias, ks)
    ls = each(_log_sigmoid, z)
    log_keep = each(lambda l, t, m: jnp.where(m, l - t, 0.0), ls, z, masks)

    def local_suffix(t):
        x = jnp.concatenate([t[:, c * LANES:(c + 1) * LANES] for c in range(n_sub)], axis=0)
        hi = x.astype(BF16)
        lo = (x - hi.astype(F32)).astype(BF16)
        y = jnp.dot(jnp.concatenate([hi, lo], axis=0), upper, preferred_element_type=F32)
        y = y[:n_sub * nh] + y[n_sub * nh:]
        return [y[c * nh:(c + 1) * nh] for c in range(n_sub)]

    local = each(local_suffix, log_keep)
    carry = carry_ref[...]
    between = []
    for t, loc in zip(log_keep, local):
        pieces = [None] * n_sub
        for c in reversed(range(n_sub)):
            pieces[c] = loc[c] + carry
            carry = carry + jnp.sum(t[:, c * LANES:(c + 1) * LANES], axis=-1, keepdims=True)
        between.append(jnp.concatenate(pieces, axis=1))
    carry_ref[...] = carry
    att = each(lambda l, s, m: jnp.where(m, jnp.exp(l + s), 0.0), ls, between, masks)
    acc_ref[...] += sum(each(_dot, att, vs))

    @pl.when(j == pl.num_programs(1) - 1)
    def _():
        o_ref[0] = acc_ref[...]


def sb_attention_decode(q, k_new, v_new, pool_k, pool_v, table, bias, per_step=4):
    b = q.shape[0]
    n_pages = table.shape[1]
    assert n_pages % per_step == 0 and pool_k.shape[1:] == (PAGE_SIZE, SB_HEADS, SB_HD)
    assert SB_HEADS & (SB_HEADS - 1) == 0 and (PAGE_SIZE * SB_HEADS) % LANES == 0
    page_rows = PAGE_SIZE * SB_HEADS
    pool_k = pool_k.reshape(-1, SB_HD)
    pool_v = pool_v.reshape(-1, SB_HD)
    row_spec = pl.BlockSpec((1, SB_HEADS, SB_HD), lambda bi, j, tab: (bi, 0, 0))

    def page_spec(u):
        return pl.BlockSpec((page_rows, SB_HD),
                            lambda bi, j, tab: (tab[bi, n_pages - 1 - (j * per_step + u)], 0))

    pages = [page_spec(u) for u in range(per_step) for _ in range(2)]
    grid_spec = pltpu.PrefetchScalarGridSpec(
        num_scalar_prefetch=1,
        grid=(b, n_pages // per_step),
        in_specs=[pl.BlockSpec((SB_HEADS, 1), lambda bi, j, tab: (0, 0)),
                  row_spec, row_spec, row_spec] + pages,
        out_specs=row_spec,
        scratch_shapes=[pltpu.VMEM((SB_HEADS, SB_HD), F32), pltpu.VMEM((SB_HEADS, 1), F32)],
    )
    return pl.pallas_call(
        functools.partial(_sb_decode_body, n_pages=n_pages, per_step=per_step),
        grid_spec=grid_spec,
        out_shape=jax.ShapeDtypeStruct((b, SB_HEADS, SB_HD), F32),
        compiler_params=_params("parallel", "arbitrary"),
        name="sb_attention_decode",
    )(table, bias.reshape(SB_HEADS, 1), q, k_new, v_new, *([pool_k, pool_v] * per_step))


def _dil_prompt_body(*refs, t, blk):
    n_grp = len(DIL_PATTERNS)
    q_refs, k_refs, v_refs = refs[:n_grp], refs[n_grp:2 * n_grp], refs[2 * n_grp:3 * n_grp]
    o_ref, m_s, l_s, acc_s = refs[3 * n_grp:3 * n_grp + 4]
    scale = DIL_HD ** -0.5
    order = sorted(range(n_grp), key=lambda g: -DIL_PATTERNS[g][1])
    for n, g in enumerate(order):
        win, dil = DIL_PATTERNS[g]
        band, cls_len = win // dil, t // dil
        n_blk = cls_len // blk
        n_keys = min(2 * blk, cls_len)
        first, last = n == 0, n == n_grp - 1
        assert band == blk and (not last or dil == 1) and n_blk * blk == cls_len
        q_ref, k_ref, v_ref = q_refs[g], k_refs[g], v_refs[g]
        rel = _iota((blk, n_keys), 0) - _iota((blk, n_keys), 1)

        def rows(start, size, dil=dil):
            return pl.ds(start, size, stride=dil) if dil > 1 else pl.ds(start, size)

        def blocks(i, _, q_ref=q_ref, k_ref=k_ref, v_ref=v_ref, dil=dil, n_blk=n_blk, n_keys=n_keys,
                   rel=rel, rows=rows, first=first, last=last, band=band):
            each = lambda f, *cols: [f(*args) for args in zip(*cols)]
            idxs = [i * DIL_BLOCKS_PER_ITER + u for u in range(DIL_BLOCKS_PER_ITER)]
            lbs = [idx % n_blk for idx in idxs]
            k_cls = [jnp.maximum(lb - 1, 0) * blk for lb in lbs]
            if dil == 1:
                q_tok = [pl.multiple_of(lb * blk, blk) for lb in lbs]
                k_tok = [pl.multiple_of(kc, blk) for kc in k_cls]
            else:
                q_tok = [idx // n_blk + lb * blk * dil for idx, lb in zip(idxs, lbs)]
                k_tok = [idx // n_blk + kc * dil for idx, kc in zip(idxs, k_cls)]
            sel = [rows(t, blk) for t in q_tok]
            q = [q_ref[0, s, :] for s in sel]
            k = [k_ref[0, rows(t, n_keys), :] for t in k_tok]
            v = [v_ref[0, rows(t, n_keys), :] for t in k_tok]
            old = None if first else [(m_s[s, :], l_s[s, :], acc_s[s, :]) for s in sel]

            def scores(qq, kk, lb, kc):
                dist = rel + (lb * blk - kc)
                valid = jnp.where(dist >= 0, dist, band + 1) <= band
                return jnp.where(valid, _dot_nt(qq, kk) * scale, NEG_BIG)

            s = each(scores, q, k, lbs, k_cls)
            m = each(lambda t: jnp.max(t, axis=-1, keepdims=True), s)
            p = each(lambda t, mm: jnp.exp(t - mm), s, m)
            l = each(lambda t: jnp.sum(t, axis=-1, keepdims=True), p)
            acc = each(_dot, p, v)
            if not first:
                m_new = each(lambda o, mm: jnp.maximum(o[0], mm), old, m)
                e0 = each(lambda o, mn: jnp.exp(o[0] - mn), old, m_new)
                e1 = each(lambda mm, mn: jnp.exp(mm - mn), m, m_new)
                l = each(lambda o, a, b, t: a * o[1] + b * t, old, e0, e1, l)
                acc = each(lambda o, a, b, t: a * o[2] + b * t, old, e0, e1, acc)
                m = m_new
            for u, sl in enumerate(sel):
                if last:
                    o_ref[0, sl, :] = (acc[u] / l[u]).astype(o_ref.dtype)
                else:
                    m_s[sl, :], l_s[sl, :], acc_s[sl, :] = m[u], l[u], acc[u]
            return 0

        assert (dil * n_blk) % DIL_BLOCKS_PER_ITER == 0
        lax.fori_loop(0, dil * n_blk // DIL_BLOCKS_PER_ITER, blocks, 0)


def dil_attention_prompt(q, k, v, blk=128):
    b, t, _ = q.shape
    n_grp = len(DIL_PATTERNS)
    assert t % blk == 0

    def spec(g):
        return pl.BlockSpec((1, t, DIL_HD), lambda bi, h: (bi, 0, g * DIL_HEADS + h))

    return pl.pallas_call(
        functools.partial(_dil_prompt_body, t=t, blk=blk),
        grid=(b, DIL_HEADS),
        in_specs=[spec(g) for g in range(n_grp)] * 3,
        out_specs=pl.BlockSpec((1, t, DIL_HD), lambda bi, h: (bi, 0, h)),
        out_shape=jax.ShapeDtypeStruct((b, t, DIL_HEADS * DIL_HD), BF16),
        scratch_shapes=[pltpu.VMEM((t, 1), F32), pltpu.VMEM((t, 1), F32), pltpu.VMEM((t, DIL_HD), F32)],
        compiler_params=_params("parallel", "parallel"),
        name="dil_attention_prompt",
    )(*([q] * n_grp + [k] * n_grp + [v] * n_grp))


def _dil_decode_body(*refs):
    n_grp = len(DIL_PATTERNS)
    q_ref, kn_ref, vn_ref = refs[:3]
    kb_refs, vb_refs = refs[3:3 + n_grp], refs[3 + n_grp:3 + 2 * n_grp]
    o_ref = refs[3 + 2 * n_grp]
    scale = DIL_HD ** -0.5
    nh = DIL_HEADS
    scores, news, masks = [], [], []
    for g in range(n_grp):
        q = q_ref[0, g]
        band = kb_refs[g].shape[1]
        k2 = kb_refs[g][0].reshape(band * nh, DIL_HD)
        own = (_iota((nh, band * nh), 1) & (nh - 1)) == _iota((nh, band * nh), 0)
        masks.append(own)
        scores.append(jnp.where(own, _dot_nt(q, k2) * scale, NEG_BIG))
        news.append(jnp.sum(q * kn_ref[0, g], axis=-1, keepdims=True) * scale)
    m = functools.reduce(jnp.maximum, [jnp.max(s, axis=-1, keepdims=True) for s in scores] + news)
    l = jnp.zeros((nh, 1), F32)
    acc = jnp.zeros((nh, DIL_HD), F32)
    for g in range(n_grp):
        p = jnp.where(masks[g], jnp.exp(scores[g] - m), 0.0)
        pn = jnp.exp(news[g] - m)
        v2 = vb_refs[g][0].reshape(p.shape[1], DIL_HD)
        l = l + jnp.sum(p, axis=-1, keepdims=True) + pn
        acc = acc + _dot(p, v2) + pn * vn_ref[0, g]
    o_ref[0] = acc / l


def dil_attention_decode(q, k_new, v_new, bufs):
    b = q.shape[0]
    n_grp = len(DIL_PATTERNS)
    assert DIL_HEADS & (DIL_HEADS - 1) == 0
    row_spec = pl.BlockSpec((1, n_grp, DIL_HEADS, DIL_HD), lambda bi: (bi, 0, 0, 0))
    views, specs = [], []
    for which in range(2):
        for g, (win, dil) in enumerate(DIL_PATTERNS):
            buf = bufs[2 * g + which]
            assert buf.shape[1] == win and win % dil == 0
            views.append(buf.reshape(b, win // dil, dil * DIL_HEADS, DIL_HD))
            specs.append(pl.BlockSpec((1, win // dil, DIL_HEADS, DIL_HD), lambda bi: (bi, 0, 0, 0)))
    return pl.pallas_call(
        _dil_decode_body,
        grid=(b,),
        in_specs=[row_spec] * 3 + specs,
        out_specs=pl.BlockSpec((1, DIL_HEADS, DIL_HD), lambda bi: (bi, 0, 0)),
        out_shape=jax.ShapeDtypeStruct((b, DIL_HEADS, DIL_HD), F32),
        compiler_params=_params("parallel"),
        name="dil_attention_decode",
    )(q, k_new, v_new, *views)


def _rwkv_mix_body(x_ref, g_ref, shift_ref, mu_ref, *refs, tt):
    o_refs, xn_ref, buf_ref = refs[:-2], refs[-2], refs[-1]
    halo = 8

    @pl.when(pl.program_id(1) == 0)
    def _():
        buf_ref[0:halo, :] = jnp.broadcast_to(shift_ref[0], (halo, shift_ref.shape[-1]))

    x = x_ref[0]
    inv = lax.rsqrt(jnp.mean(x * x, axis=-1, keepdims=True) + NORM_EPS)
    width = 2 * LANES
    for c in range(x.shape[1] // width):
        sl = slice(c * width, (c + 1) * width)
        xn = x_ref[0, :, sl] * inv * g_ref[:, sl]
        buf_ref[halo:halo + tt, sl] = xn
        xx = buf_ref[halo - 1:halo - 1 + tt, sl] - xn
        for i, o_ref in enumerate(o_refs):
            o_ref[0, :, sl] = (xn + xx * mu_ref[i:i + 1, sl]).astype(o_ref.dtype)
        xn_ref[0, :, sl] = xn
    buf_ref[0:halo, :] = buf_ref[tt:tt + halo, :]


def rwkv_mix(x, g, shift0, mu, tt=128):
    b, t, d = x.shape
    tt = math.gcd(tt, t)
    assert tt % 8 == 0
    n_mix = mu.shape[0]
    seq = pl.BlockSpec((1, tt, d), lambda bi, ti: (bi, ti, 0))
    outs = pl.pallas_call(
        functools.partial(_rwkv_mix_body, tt=tt),
        grid=(b, t // tt),
        in_specs=[seq, pl.BlockSpec((1, d), lambda bi, ti: (0, 0)),
                  pl.BlockSpec((1, 1, d), lambda bi, ti: (bi, 0, 0)),
                  pl.BlockSpec((n_mix, d), lambda bi, ti: (0, 0))],
        out_specs=[seq] * n_mix + [pl.BlockSpec((1, tt, d), lambda bi, ti: (bi, 0, 0))],
        out_shape=[jax.ShapeDtypeStruct((b, t, d), BF16)] * n_mix + [jax.ShapeDtypeStruct((b, tt, d), F32)],
        scratch_shapes=[pltpu.VMEM((tt + 8, d), F32)],
        compiler_params=_params("parallel", "arbitrary"),
        name="rwkv_mix",
    )(x, g.reshape(1, d), shift0.astype(F32).reshape(b, 1, d), mu)
    return outs[:n_mix], outs[n_mix]


def _rwkv_scan_body(r_ref, k_ref, v_ref, wl_ref, al_ref, g_ref, prm_ref, s0_ref, y_ref, sf_ref,
                    st_ref, *, t_valid, t_total):
    c = pl.program_id(1)
    C, hd = RWKV_CHUNK, RWKV_HD
    n2 = 2 * C
    grp = RWKV_PAIRS_PER_DOT
    rows = grp * n2

    @pl.when(c == 0)
    def _():
        st_ref[...] = s0_ref[0]

    left = _iota((C, LANES), 1) < hd
    ri, ci = _iota((rows, rows), 0), _iota((rows, rows), 1)
    strict = (ci & (C - 1)) < (ri & (C - 1))
    incl = (ci & (C - 1)) <= (ri & (C - 1))
    tri = jnp.where(_iota((C, C), 1) <= _iota((C, C), 0), 1.0, 0.0).astype(BF16)
    same_head = (_iota((LANES, LANES), 0) // hd) == (_iota((LANES, LANES), 1) // hd)
    seg_ones = jnp.where(same_head, 1.0, 0.0).astype(BF16)
    masked = t_valid < t_total
    rows_valid = (c * C + _iota((C, LANES), 0)) < t_valid
    zero_slab = jnp.zeros((n2, LANES), F32)

    def seg_sum(x):
        hi = x.astype(BF16)
        lo = (x - hi.astype(F32)).astype(BF16)
        y = jnp.dot(jnp.concatenate([hi, lo], axis=0), seg_ones, preferred_element_type=F32)
        return y[:x.shape[0]] + y[x.shape[0]:]

    def cumsum_steps(x):
        w = x.shape[1]
        y = jnp.dot(tri, jnp.concatenate(_split3(x), axis=1), preferred_element_type=F32)
        return y[:, :w] + y[:, w:2 * w] + y[:, 2 * w:]

    def stack(x):
        return jnp.concatenate([jnp.where(left, x, 0.0), jnp.where(left, 0.0, x)], axis=0)

    def spread(slabs):
        return jnp.concatenate(
            [jnp.concatenate([s if q == j else zero_slab for j in range(grp)], axis=1)
             for q, s in enumerate(slabs)], axis=0)

    def core(groups):
        each = lambda f, *cols: [f(*args) for args in zip(*cols)]
        dss, stss = [g[0] for g in groups], [g[1] for g in groups]
        flat_d = [d for ds in dss for d in ds]
        flat_st = [st for sts in stss for st in sts]
        ar = each(lambda d: jnp.concatenate([stack(d['a_t']), stack(d['r_t'])], axis=0).astype(BF16), flat_d)
        bk = each(lambda d: jnp.concatenate([stack(d['b_t']), stack(d['k_t'])], axis=0).astype(BF16), flat_d)
        gram_p = each(_dot_nt, ar, bk)
        a_s_p = each(_dot_nt, ar, flat_st)
        by_group = lambda xs: [xs[q:q + grp] for q in range(0, len(xs), grp)]
        quadrant = lambda gs, r0, c0: spread([g[r0:r0 + n2, c0:c0 + n2] for g in gs])
        vv = each(lambda ds: jnp.concatenate([d['v'] for d in ds for _ in range(2)], axis=0), dss)
        y = each(lambda gs, v, ss: _dot(jnp.where(strict, quadrant(gs, 0, n2), 0.0), v)
                 + jnp.concatenate([s[:n2] for s in ss], axis=0),
                 by_group(gram_p), vv, by_group(a_s_p))
        pw = each(lambda gs: jnp.where(strict, quadrant(gs, 0, 0), 0.0), by_group(gram_p))
        n = 1
        while n < C:
            y = each(lambda p, t: t + _dot(p, t), pw, y)
            n *= 2
            if n < C:
                pw = each(lambda p: _dot(p, p), pw)
        o2 = each(lambda gs, t, v, ss: _dot(
            jnp.concatenate([jnp.where(incl, quadrant(gs, n2, 0), 0.0),
                             jnp.where(incl, quadrant(gs, n2, n2), 0.0)], axis=1),
            jnp.concatenate([t, v], axis=0)) + jnp.concatenate([s[n2:] for s in ss], axis=0),
            by_group(gram_p), y, vv, by_group(a_s_p))
        pick = lambda t, q: jnp.where(left, t[q * n2:q * n2 + C], t[q * n2 + C:(q + 1) * n2])
        upd = each(lambda ds, t: _dot(
            jnp.concatenate([jnp.concatenate([pick(t, q), d['v']], axis=0)
                             for q, d in enumerate(ds)], axis=1).T,
            jnp.concatenate([jnp.concatenate([d['b_end'], d['k_end']], axis=0) for d in ds], axis=1)),
            dss, y)
        os, s_new = [], []
        for ds, sts, o, up in zip(dss, stss, o2, upd):
            os += [pick(o, q) for q in range(grp)]
            s_new += [jnp.where(same_head, st * d['decay_end']
                                + up[q * LANES:(q + 1) * LANES, q * LANES:(q + 1) * LANES], 0.0)
                      for q, (d, st) in enumerate(zip(ds, sts))]
        return os, s_new

    def pairs(i, _):
        ps = [i * RWKV_PAIRS_PER_ITER + u for u in range(RWKV_PAIRS_PER_ITER)]
        offs = [pl.multiple_of(p * LANES, LANES) for p in ps]
        sts = [st_ref[p] for p in ps]
        ds = []
        for off in offs:
            r, k, v, wl, al, g = [ref[0, :, pl.ds(off, LANES)]
                                  for ref in (r_ref, k_ref, v_ref, wl_ref, al_ref, g_ref)]
            prm = prm_ref[:, pl.ds(off, LANES)]
            w0, a0, k_k, k_a, r_k, gn_w, gn_b = [prm[j:j + 1] for j in range(7)]
            a = jax.nn.sigmoid(a0 + al)
            ds.append(dict(r=r, v=v, g=g, a=a, gn_w=gn_w, gn_b=gn_b, r_k=r_k, kkr=k * k_k,
                           lw=-jnp.exp(-_softplus(-(w0 + wl)) - 0.5),
                           kmod=k * (1.0 + (a - 1.0) * k_a)))
        sums = seg_sum(jnp.concatenate([d['kkr'] * d['kkr'] for d in ds]
                                       + [d['r'] * d['kmod'] * d['r_k'] for d in ds], axis=0))
        n_p = len(ds)
        for u, d in enumerate(ds):
            d['kk'] = d['kkr'] * lax.rsqrt(jnp.maximum(sums[u * C:(u + 1) * C], 1e-24))
            d['bonus'] = sums[(n_p + u) * C:(n_p + u + 1) * C] * d['v']
            if masked:
                for name in ('lw', 'kk', 'kmod', 'v'):
                    d[name] = jnp.where(rows_valid, d[name], 0.0)
        lc_all = cumsum_steps(jnp.concatenate([d['lw'] for d in ds], axis=1))
        for u, d in enumerate(ds):
            lc = lc_all[:, u * LANES:(u + 1) * LANES]
            lend = lc[C - 1:C, :]
            kka = d['kk'] * d['a']
            e_inv, e_end = jnp.exp(-lc), jnp.exp(lend - lc)
            d.update(a_t=-d['kk'] * jnp.exp(lc - d['lw']), r_t=d['r'] * jnp.exp(lc),
                     b_t=kka * e_inv, k_t=d['kmod'] * e_inv, b_end=kka * e_end,
                     k_end=d['kmod'] * e_end, decay_end=jnp.exp(lend))
        os, s_news = core([(ds[q:q + grp], sts[q:q + grp]) for q in range(0, n_p, grp)])
        stats = jnp.dot(jnp.concatenate(os + [o * o for o in os], axis=0).astype(BF16), seg_ones,
                        preferred_element_type=F32)
        for u, (p, off, d, o) in enumerate(zip(ps, offs, ds, os)):
            mean = stats[u * C:(u + 1) * C] * (1.0 / hd)
            var = stats[(n_p + u) * C:(n_p + u + 1) * C] * (1.0 / hd) - mean * mean
            on = (o - mean) * lax.rsqrt(var + GN_EPS) * d['gn_w'] + d['gn_b']
            st_ref[p] = s_news[u]
            y_ref[0, :, pl.ds(off, LANES)] = ((on + d['bonus']) * d['g'])[:y_ref.shape[1]].astype(y_ref.dtype)
        return 0

    lax.fori_loop(0, D_MODEL // LANES // RWKV_PAIRS_PER_ITER, pairs, 0)

    @pl.when(c == pl.num_programs(1) - 1)
    def _():
        sf_ref[0] = st_ref[...]


def rwkv_scan(r, k, v, wl, al, g, prm, s0, t_valid):
    b, t, d = r.shape
    C = RWKV_CHUNK
    assert t % C == 0 and d == D_MODEL
    seq = pl.BlockSpec((1, C, d), lambda bi, c: (bi, c, 0))
    st_spec = pl.BlockSpec((1, d // LANES, LANES, LANES), lambda bi, c: (bi, 0, 0, 0))
    y_rows = BF16_ROWS if (t == C and t_valid <= BF16_ROWS) else C
    return pl.pallas_call(
        functools.partial(_rwkv_scan_body, t_valid=t_valid, t_total=t),
        grid=(b, t // C),
        in_specs=[seq] * 6 + [pl.BlockSpec((8, d), lambda bi, c: (0, 0)), st_spec],
        out_specs=[pl.BlockSpec((1, y_rows, d), lambda bi, c: (bi, c, 0)), st_spec],
        out_shape=[jax.ShapeDtypeStruct((b, t // C * y_rows, d), BF16),
                   jax.ShapeDtypeStruct((b, d // LANES, LANES, LANES), F32)],
        scratch_shapes=[pltpu.VMEM((d // LANES, LANES, LANES), F32)],
        compiler_params=_params("parallel", "arbitrary"),
        name="rwkv_scan",
    )(r, k, v, wl, al, g, prm, s0)


def _conv_body(x_ref, prev_ref, w_ref, b_ref, o_ref, buf_ref, *, tt):
    halo = 8

    @pl.when(pl.program_id(2) == 0)
    def _():
        buf_ref[0:halo, :] = prev_ref[0]

    buf_ref[halo:halo + tt, :] = x_ref[0]
    for c in range(x_ref.shape[2] // LANES):
        sl = slice(c * LANES, (c + 1) * LANES)
        acc = b_ref[:, sl] + x_ref[0, :, sl] * w_ref[SSM_CONV - 1:SSM_CONV, sl]
        for back in range(1, SSM_CONV):
            tap = SSM_CONV - 1 - back
            acc = acc + buf_ref[halo - back:halo - back + tt, sl] * w_ref[tap:tap + 1, sl]
        o_ref[0, :, sl] = _silu(acc)
    buf_ref[0:halo, :] = buf_ref[tt:tt + halo, :]


def causal_conv_silu(x, prev8, w, bias, tt=256, tc=1024):
    b, t, ch = x.shape
    tt = min(tt, t)
    assert t % tt == 0 and ch % tc == 0
    return pl.pallas_call(
        functools.partial(_conv_body, tt=tt),
        grid=(b, ch // tc, t // tt),
        in_specs=[pl.BlockSpec((1, tt, tc), lambda bi, ci, ti: (bi, ti, ci)),
                  pl.BlockSpec((1, 8, tc), lambda bi, ci, ti: (bi, 0, ci)),
                  pl.BlockSpec((SSM_CONV, tc), lambda bi, ci, ti: (0, ci)),
                  pl.BlockSpec((1, tc), lambda bi, ci, ti: (0, ci))],
        out_specs=pl.BlockSpec((1, tt, tc), lambda bi, ci, ti: (bi, ti, ci)),
        out_shape=jax.ShapeDtypeStruct((b, t, ch), F32),
        scratch_shapes=[pltpu.VMEM((tt + 8, tc), F32)],
        compiler_params=_params("parallel", "parallel", "arbitrary"),
        name="causal_conv_silu",
    )(x, prev8, w, bias.reshape(1, ch))


def _ssd_body(xa_ref, z_ref, dt_ref, dtt_ref, dtb_ref, dtbt_ref, al_ref, alt_ref, dsk_ref, nw_ref,
              h0_ref, y_ref, hf_ref, h_ref, *, t_valid, t_total):
    c = pl.program_id(1)
    Q, P = SSM_CHUNK, SSM_HEADDIM
    per_grp = SSM_HEADS // SSM_GROUPS
    gw = per_grp * P

    @pl.when(c == 0)
    def _():
        h_ref[...] = h0_ref[0]

    tri = _iota((Q, Q), 1) <= _iota((Q, Q), 0)
    tri_b = jnp.where(tri, 1.0, 0.0).astype(BF16)
    upp_b = jnp.where(_iota((Q, Q), 0) <= _iota((Q, Q), 1), 1.0, 0.0).astype(BF16)
    left = _iota((Q, LANES), 1) < P
    masked = t_valid < t_total

    def one_hot_rows(width, per):
        head = _iota((3 * per_grp, width), 0) % per_grp
        return jnp.where(_iota((3 * per_grp, width), 1) // per == head, 1.0, 0.0).astype(BF16)

    to_heads = one_hot_rows(gw, P)
    to_slabs = one_hot_rows(per_grp * Q, Q)

    def group(g):
        dt = _softplus(dt_ref[0, g] + dtb_ref[g])
        dtt = _softplus(dtt_ref[0, g] + dtbt_ref[g])
        if masked:
            dt = jnp.where(c * Q + _iota(dt.shape, 0) < t_valid, dt, 0.0)
            dtt = jnp.where(c * Q + _iota(dtt.shape, 1) < t_valid, dtt, 0.0)
        cum = _dot_exact_lhs(tri_b, dt * -jnp.exp(al_ref[g]))
        cumt = _dot_exact_rhs(dtt * -jnp.exp(alt_ref[g]), upp_b)
        yield
        spread_cols = lambda t, e3: jnp.dot(jnp.concatenate(_split3(t), axis=1), e3,
                                            preferred_element_type=F32)
        ecum_w = spread_cols(jnp.exp(cum), to_heads)
        tail_w = spread_cols(jnp.exp(cum[Q - 1:Q, :] - cum) * dt, to_heads)
        cum_w = spread_cols(cum, to_slabs)
        yield
        boff = pl.multiple_of(SSM_DINNER + g * SSM_STATE, SSM_STATE)
        coff = pl.multiple_of(SSM_DINNER + SSM_GROUPS * SSM_STATE + g * SSM_STATE, SSM_STATE)
        roff = pl.multiple_of(g * gw, gw)
        bm = xa_ref[0, :, pl.ds(boff, SSM_STATE)]
        cm = xa_ref[0, :, pl.ds(coff, SSM_STATE)]
        cb = _dot_nt(cm, bm)
        hg = h_ref[pl.ds(roff, gw), :]
        y_state = _dot_nt(cm, hg)
        xg = xa_ref[0, :, pl.ds(roff, gw)]
        yield
        xps = [xg[:, q * LANES:(q + 1) * LANES] for q in range(gw // LANES)]

        def head_matrix(hh):
            seg = cum_w[:, hh * Q:(hh + 1) * Q] - cumt[hh:hh + 1, :]
            dec = jnp.where(tri, jnp.exp(jnp.where(tri, seg, 0.0)), 0.0)
            return (cb * dec * dtt[hh:hh + 1, :]).astype(BF16)

        mats = [head_matrix(hh) for hh in range(per_grp)]
        yield
        prods = [_dot(mats[hh], xps[hh // 2]) for hh in range(per_grp)]
        yield
        ys, xts = [], []
        for q, xp in enumerate(xps):
            sl = slice(q * LANES, (q + 1) * LANES)
            ys.append(jnp.where(left, prods[2 * q], prods[2 * q + 1]) + y_state[:, sl] * ecum_w[:, sl])
            xts.append(xp * tail_w[:, sl])
        upd = _dot(jnp.concatenate(xts, axis=1).T, bm)
        yield
        for hh in range(per_grp):
            rows = slice(hh * P, (hh + 1) * P)
            h_ref[pl.ds(pl.multiple_of(roff + hh * P, P), P), :] = (
                hg[rows] * jnp.exp(cumt[hh:hh + 1, Q - 1:Q]) + upd[rows])
        yg = jnp.concatenate(ys, axis=1)
        yg = (yg + xg * dsk_ref[:, pl.ds(roff, gw)]) * _silu(z_ref[0, :, pl.ds(roff, gw)])
        ms = jnp.mean(yg * yg, axis=-1, keepdims=True)
        y_ref[0, :, pl.ds(roff, gw)] = (
            yg * lax.rsqrt(ms + SSM_NORM_EPS) * nw_ref[:, pl.ds(roff, gw)]).astype(y_ref.dtype)

    def groups(i, _):
        stages = [group(i * SSD_GROUPS_PER_ITER + u) for u in range(SSD_GROUPS_PER_ITER)]
        for _ in itertools.zip_longest(*stages):
            pass
        return 0

    lax.fori_loop(0, SSM_GROUPS // SSD_GROUPS_PER_ITER, groups, 0)

    @pl.when(c == pl.num_programs(1) - 1)
    def _():
        hf_ref[0] = h_ref[...]


def ssd_scan(xa, z, dt_raw, dt_bias, a_log, d_skip, norm_w, h0, t_valid):
    b, t, _ = xa.shape
    Q = SSM_CHUNK
    per_grp = SSM_HEADS // SSM_GROUPS
    assert t % Q == 0
    dt_g = dt_raw.reshape(b, t, SSM_GROUPS, per_grp).transpose(0, 2, 1, 3)
    dt_gt = dt_g.transpose(0, 1, 3, 2)
    grp = lambda p: p.reshape(SSM_GROUPS, 1, per_grp)
    grp_t = lambda p: p.reshape(SSM_GROUPS, per_grp, 1)
    full3 = lambda s: pl.BlockSpec(s, lambda bi, c: (0, 0, 0))
    lanes = lambda w: pl.BlockSpec((1, w), lambda bi, c: (0, 0))
    st_spec = pl.BlockSpec((1, SSM_DINNER, SSM_STATE), lambda bi, c: (bi, 0, 0))
    return pl.pallas_call(
        functools.partial(_ssd_body, t_valid=t_valid, t_total=t),
        grid=(b, t // Q),
        in_specs=[pl.BlockSpec((1, Q, SSM_CONV_DIM), lambda bi, c: (bi, c, 0)),
                  pl.BlockSpec((1, Q, SSM_DINNER), lambda bi, c: (bi, c, 0)),
                  pl.BlockSpec((1, SSM_GROUPS, Q, per_grp), lambda bi, c: (bi, 0, c, 0)),
                  pl.BlockSpec((1, SSM_GROUPS, per_grp, Q), lambda bi, c: (bi, 0, 0, c)),
                  full3((SSM_GROUPS, 1, per_grp)), full3((SSM_GROUPS, per_grp, 1)),
                  full3((SSM_GROUPS, 1, per_grp)), full3((SSM_GROUPS, per_grp, 1)),
                  lanes(SSM_DINNER), lanes(SSM_DINNER), st_spec],
        out_specs=[pl.BlockSpec((1, Q, SSM_DINNER), lambda bi, c: (bi, c, 0)), st_spec],
        out_shape=[jax.ShapeDtypeStruct((b, t, SSM_DINNER), BF16),
                   jax.ShapeDtypeStruct((b, SSM_DINNER, SSM_STATE), F32)],
        scratch_shapes=[pltpu.VMEM((SSM_DINNER, SSM_STATE), F32)],
        compiler_params=_params("parallel", "arbitrary"),
        name="ssd_scan",
    )(xa, z, dt_g, dt_gt, grp(dt_bias), grp_t(dt_bias), grp(a_log), grp_t(a_log),
      jnp.repeat(d_skip, SSM_HEADDIM).reshape(1, SSM_DINNER), norm_w.reshape(1, SSM_DINNER), h0)


def _pad_time(x3, mult):
    t = x3.shape[1]
    tp = -(-t // mult) * mult
    return x3 if tp == t else jnp.pad(x3, ((0, 0), (0, tp - t), (0, 0)))


def _rwkv_mixer(xf, b, t, ln, shift0, wkv0, W):
    d = D_MODEL
    x_p = _pad_time(xf.reshape(b, t, d), RWKV_CHUNK)
    tp = x_p.shape[1]
    mixes, xn_tail = rwkv_mix(x_p, ln, shift0, W['rwkv_mu'])
    xr, xw, xk, xv, xa, xg = [m.reshape(b * tp, d) for m in mixes]
    shift_out = xn_tail[:, (t - 1) % xn_tail.shape[1]]
    r = linear(xr, W['rwkv_wr'])
    k = linear(xk, W['rwkv_wk'])
    v = linear(xv, W['rwkv_wv'])
    wl = linear(linear(xw, W['rwkv_w1'], act=jnp.tanh, out_dtype=BF16), W['rwkv_w2'])
    al = linear(linear(xa, W['rwkv_a1'], out_dtype=BF16), W['rwkv_a2'])
    g = linear(linear(xg, W['rwkv_g1'], act=jax.nn.sigmoid, out_dtype=BF16), W['rwkv_g2'])
    prm = jnp.stack([W['rwkv_w0'], W['rwkv_a0'], W['rwkv_kk'], W['rwkv_ka'], W['rwkv_rk'].reshape(d),
                     W['rwkv_gn_w'], W['rwkv_gn_b'], jnp.zeros((d,), F32)])
    n_pair = d // LANES
    s4 = wkv0.astype(F32).reshape(b, n_pair, 2, RWKV_HD, RWKV_HD)
    zero = jnp.zeros_like(s4[:, :, 0])
    s0 = jnp.concatenate([jnp.concatenate([s4[:, :, 0], zero], axis=-1),
                          jnp.concatenate([zero, s4[:, :, 1]], axis=-1)], axis=-2)
    to3 = lambda a: a.reshape(b, tp, d)
    y, sf = rwkv_scan(to3(r), to3(k), to3(v), to3(wl), to3(al), to3(g), prm, s0, t_valid=t)
    wkv = jnp.stack([sf[:, :, :RWKV_HD, :RWKV_HD], sf[:, :, RWKV_HD:, RWKV_HD:]], axis=2)
    y = y[:, :t].reshape(b * t, d)
    return linear(y, W['rwkv_wo'], res=xf), shift_out, wkv.reshape(b, d // RWKV_HD, RWKV_HD, RWKV_HD)


def _sb_mixer(xf, b, t, ln, sb_past, W):
    d = D_MODEL
    xn = rmsnorm(xf, ln)
    qdt = BF16 if sb_past is None else F32
    q = linear(xn, W['sb_wqkv'], col0=0, n=d, gain=W['sb_gq'], hw=SB_HD, out_dtype=qdt)
    k = linear(xn, W['sb_wqkv'], col0=d, n=d, gain=W['sb_gk'], hw=SB_HD)
    v = linear(xn, W['sb_wqkv'], col0=2 * d, n=d)
    to3 = lambda a: a.reshape(b, t, d)
    if sb_past is None:
        o = sb_attention_prompt(to3(q), to3(k), to3(v), W['sb_bias'])
    else:
        pool_k, pool_v, table = sb_past
        assert t == 1
        heads = lambda a: a.reshape(b, SB_HEADS, SB_HD)
        o = sb_attention_decode(heads(q), heads(k), heads(v), pool_k, pool_v, table, W['sb_bias'])
    xf = linear(o.reshape(b * t, d), W['sb_wo'], res=xf)
    return xf, k.reshape(b, t, SB_HEADS, SB_HD), v.reshape(b, t, SB_HEADS, SB_HD)


def _dil_mixer(xf, b, t, pos0, ln, dil_bufs, W):
    n_grp = len(DIL_PATTERNS)
    gd = n_grp * DIL_HEADS * DIL_HD
    xn = rmsnorm(xf, ln)
    rows = t if t >= 8 else b * t
    rope = rope_tables(pos0 + (jnp.arange(rows, dtype=jnp.int32) % t))
    tm = min(rows, LINEAR_TM)
    q = linear(xn, W['dil_wqkv'], col0=0, n=gd, gain=W['dil_gq'], hw=DIL_HD, rope=rope, tm=tm)
    k = linear(xn, W['dil_wqkv'], col0=gd, n=gd, gain=W['dil_gk'], hw=DIL_HD, rope=rope, tm=tm)
    v = linear(xn, W['dil_wqkv'], col0=2 * gd, n=gd, tm=tm)
    to3 = lambda a: a.reshape(b, t, gd)
    if dil_bufs is None:
        o = dil_attention_prompt(to3(q), to3(k), to3(v))
    else:
        assert t == 1
        heads = lambda a: a.reshape(b, n_grp, DIL_HEADS, DIL_HD)
        o = dil_attention_decode(heads(q), heads(k), heads(v), dil_bufs)
    xf = linear(o.reshape(b * t, DIL_HEADS * DIL_HD), W['dil_wo'], res=xf)
    k5 = k.reshape(b, t, n_grp, DIL_HEADS, DIL_HD)
    v5 = v.reshape(b, t, n_grp, DIL_HEADS, DIL_HD)
    states = []
    for g, (win, _) in enumerate(DIL_PATTERNS):
        keep = min(win, t)
        states += [k5[:, t - keep:, g], v5[:, t - keep:, g]]
    return xf, states


def _ssd_mixer(xf, b, t, ln, conv0, h0, W):
    d = D_MODEL
    xn = _pad_time(rmsnorm(xf, ln).reshape(b, t, d), SSM_CHUNK)
    tp = xn.shape[1]
    xn = xn.reshape(b * tp, d)
    z = linear(xn, W['ssm_win'], col0=0, n=SSM_DINNER)
    xbc = linear(xn, W['ssm_win'], col0=SSM_DINNER, n=SSM_CONV_DIM).reshape(b, tp, SSM_CONV_DIM)
    dt_raw = linear(xn, W['ssm_win'][:, SSM_DINNER + SSM_CONV_DIM:])
    prev8 = jnp.pad(conv0.astype(F32), ((0, 0), (8 - (SSM_CONV - 1), 0), (0, 0)))
    xa = causal_conv_silu(xbc, prev8, W['ssm_conv_w'], W['ssm_conv_b'])
    y, hf = ssd_scan(xa, z.reshape(b, tp, SSM_DINNER), dt_raw.reshape(b, tp, SSM_HEADS),
                     W['ssm_dt_bias'], W['ssm_a_log'], W['ssm_d'], W['ssm_norm_w'],
                     h0.astype(F32).reshape(b, SSM_DINNER, SSM_STATE), t_valid=t)
    xf = linear(y[:, :t].reshape(b * t, SSM_DINNER), W['ssm_wout'], res=xf)
    conv_state = jnp.concatenate([conv0.astype(F32), xbc[:, :t]], axis=1)[:, t:]
    return xf, conv_state, hf.reshape(b, SSM_HEADS, SSM_HEADDIM, SSM_STATE)


def _mixer_and_memory(xf, i, G, W):
    b, t, d = G['shape']
    st = G['out']
    kind = i % 4
    ln = W['ln_mix'][i]
    if kind == 0:
        xf, st['rwkv_shift'], st['rwkv_wkv'] = _rwkv_mixer(xf, b, t, ln, G['shift0'], G['wkv0'], W)
    elif kind == 1:
        xf, st['sb_k'], st['sb_v'] = _sb_mixer(xf, b, t, ln, G['sb_past'], W)
    elif kind == 2:
        xf, st['dil'] = _dil_mixer(xf, b, t, G['pos0'], ln, G['dil_bufs'], W)
    else:
        xf, st['ssm_conv'], st['ssm_h'] = _ssd_mixer(xf, b, t, ln, G['conv0'], G['h0'], W)
    xn = rmsnorm(xf, W['ln_mem'][i])
    q = linear(xn, W['mem_wq'], layer=i, gain=W['mem_gq'][i], hw=MEM_HD, out_dtype=BF16)
    q = _pad_time(q.reshape(b, t, d), 8)
    o = mem_attention(q, G['mem_k'], G['mem_v'], i)[:, :t].reshape(b * t, d)
    return linear(o, W['mem_wo'], layer=i, res=xf)


def _run_groups(P, S, W):
    xp, xs = [G['x'].reshape(-1, D_MODEL) for G in (P, S)]
    for i in range(DEPTH):
        xp, xs = ffn(xp, xs, W['ln_ffn1'][i], W['ffn1_gate'], W['ffn1_up'], W['ffn1_down'], i)
        xp = _mixer_and_memory(xp, i, P, W)
        xs = _mixer_and_memory(xs, i, S, W)
        xp, xs = ffn(xp, xs, W['ln_ffn2'][i], W['ffn2_gate'], W['ffn2_up'], W['ffn2_down'], i)
    return xp.reshape(P['shape']), xs.reshape(S['shape'])


def kernel(x_prompt, x_sample, state_rwkv_shift, state_rwkv_wkv, cache_sb_k, cache_sb_v, cache_dil0_k, cache_dil0_v, cache_dil1_k, cache_dil1_v, cache_dil2_k, cache_dil2_v, state_ssm_conv, state_ssm_h, cache_mem_k, cache_mem_v, page_table, mem_prompt, ln_ffn1, ffn1_gate, ffn1_up, ffn1_down, ln_mix, ln_mem, mem_wq, mem_gq, mem_wk, mem_gk, mem_wv, mem_wo, ln_ffn2, ffn2_gate, ffn2_up, ffn2_down, rwkv_mu, rwkv_wr, rwkv_wk, rwkv_wv, rwkv_wo, rwkv_w0, rwkv_w1, rwkv_w2, rwkv_a0, rwkv_a1, rwkv_a2, rwkv_g1, rwkv_g2, rwkv_kk, rwkv_ka, rwkv_rk, rwkv_gn_w, rwkv_gn_b, sb_wqkv, sb_gq, sb_gk, sb_bias, sb_wo, dil_wqkv, dil_gq, dil_gk, dil_wo, ssm_win, ssm_conv_w, ssm_conv_b, ssm_dt_bias, ssm_a_log, ssm_d, ssm_norm_w, ssm_wout):
    W = dict(locals())
    bp, _, d = x_prompt.shape
    n_mem = mem_prompt.shape[1]
    past_len = page_table.shape[1] * cache_sb_k.shape[1]

    mem_rows = mem_prompt.reshape(bp * n_mem, d)
    p_mem_k = jnp.stack([linear(mem_rows, mem_wk, layer=i, gain=mem_gk[i], hw=MEM_HD)
                         for i in range(DEPTH)]).reshape(DEPTH, bp, n_mem, d)
    p_mem_v = jnp.stack([linear(mem_rows, mem_wv, layer=i)
                         for i in range(DEPTH)]).reshape(DEPTH, bp, n_mem, d)
    P = dict(x=x_prompt, shape=x_prompt.shape, pos0=0, mem_k=p_mem_k, mem_v=p_mem_v,
             shift0=jnp.zeros((bp, d), F32), wkv0=jnp.zeros((bp, d // RWKV_HD, RWKV_HD, RWKV_HD), F32),
             sb_past=None, dil_bufs=None,
             conv0=jnp.zeros((bp, SSM_CONV - 1, SSM_CONV_DIM), F32),
             h0=jnp.zeros((bp, SSM_HEADS, SSM_HEADDIM, SSM_STATE), F32), out={})
    merged = lambda a: a.reshape(a.shape[:3] + (MEM_HEADS * MEM_HD,))
    S = dict(x=x_sample, shape=x_sample.shape, pos0=past_len,
             mem_k=merged(cache_mem_k), mem_v=merged(cache_mem_v),
             shift0=state_rwkv_shift, wkv0=state_rwkv_wkv, sb_past=(cache_sb_k, cache_sb_v, page_table),
             dil_bufs=(cache_dil0_k, cache_dil0_v, cache_dil1_k, cache_dil1_v, cache_dil2_k, cache_dil2_v),
             conv0=state_ssm_conv, h0=state_ssm_h, out={})
    y_p, y_s = _run_groups(P, S, W)
    sp, ss = P['out'], S['out']

    mem_shape = (DEPTH, bp, n_mem, MEM_HEADS, MEM_HD)
    dil = []
    for g in range(len(DIL_PATTERNS)):
        dil += [sp['dil'][2 * g], sp['dil'][2 * g + 1], ss['dil'][2 * g], ss['dil'][2 * g + 1]]
    return (y_p, y_s,
            sp['rwkv_shift'], ss['rwkv_shift'], sp['rwkv_wkv'], ss['rwkv_wkv'],
            sp['sb_k'], sp['sb_v'], ss['sb_k'], ss['sb_v'],
            *dil,
            sp['ssm_conv'], ss['ssm_conv'], sp['ssm_h'], ss['ssm_h'],
            p_mem_k.reshape(mem_shape), p_mem_v.reshape(mem_shape))
```

```python
import functools
import itertools
import math

import jax
import jax.numpy as jnp
from jax import lax
from jax.experimental import pallas as pl
from jax.experimental.pallas import tpu as pltpu

F32 = jnp.float32
BF16 = jnp.bfloat16

D_MODEL = 2048
DEPTH = 4
NORM_EPS = 1e-6
PAGE_SIZE = 128
MEM_HEADS = 4
MEM_HD = D_MODEL // MEM_HEADS
RWKV_HD = 64
GN_EPS = 64e-5
SB_HD = 128
SB_HEADS = D_MODEL // SB_HD
DIL_PATTERNS = ((128, 1), (512, 4), (2048, 16))
DIL_HD = 128
DIL_HEADS = 8
ROPE_THETA = 500000.0
ROPE_DIM = DIL_HD // 4
SSM_DINNER = 2 * D_MODEL
SSM_HEADDIM = 64
SSM_HEADS = SSM_DINNER // SSM_HEADDIM
SSM_STATE = 128
SSM_GROUPS = 8
SSM_CONV = 4
SSM_CONV_DIM = SSM_DINNER + 2 * SSM_GROUPS * SSM_STATE
SSM_CHUNK = 128
SSM_NORM_EPS = 1e-5

LANES = 128
BF16_ROWS = 16
V7X_VMEM_BYTES = 64 << 20
VMEM_LIMIT = V7X_VMEM_BYTES - (8 << 20)
FFN_VMEM_LIMIT = V7X_VMEM_BYTES - (4 << 20)
DIL_BLOCKS_PER_ITER = 16
LINEAR_TM = 1024
LINEAR_TN = 512
LINEAR_TN_WIDE = 1024
RWKV_CHUNK = 64
SSD_GROUPS_PER_ITER = 4
RWKV_PAIRS_PER_ITER = 16
RWKV_PAIRS_PER_DOT = 2
NEG_BIG = -1e30


def _params(*sem, vmem=VMEM_LIMIT):
    return pltpu.CompilerParams(dimension_semantics=sem, vmem_limit_bytes=vmem)


def _dot(a, b):
    return jnp.dot(a.astype(BF16), b.astype(BF16), preferred_element_type=F32)


def _dot_nt(a, b):
    return lax.dot_general(a.astype(BF16), b.astype(BF16), (((1,), (1,)), ((), ())),
                           preferred_element_type=F32)


def _split3(x):
    hi = x.astype(BF16)
    r1 = x - hi.astype(F32)
    mid = r1.astype(BF16)
    lo = (r1 - mid.astype(F32)).astype(BF16)
    return hi, mid, lo


def _dot_exact_rhs(x, m_bf16):
    hi, mid, lo = _split3(x)
    d = lambda p: jnp.dot(p, m_bf16, preferred_element_type=F32)
    return d(hi) + d(mid) + d(lo)


def _dot_exact_lhs(m_bf16, x):
    hi, mid, lo = _split3(x)
    d = lambda p: jnp.dot(m_bf16, p, preferred_element_type=F32)
    return d(hi) + d(mid) + d(lo)


def _iota(shape, dim):
    return lax.broadcasted_iota(jnp.int32, shape, dim)


def _softplus(x):
    return jnp.maximum(x, 0.0) + jnp.log1p(jnp.exp(-jnp.abs(x)))


def _log_sigmoid(x):
    return jnp.minimum(x, 0.0) - jnp.log(1.0 + jnp.exp(-jnp.abs(x)))


def _silu(x):
    return x * jax.nn.sigmoid(x)


def _linear_body(*refs, has_gain, has_rope, has_res, act, hw, res_scale):
    it = iter(refs)
    x_ref, w_ref = next(it), next(it)
    gain_ref = next(it) if has_gain else None
    cos_ref, sin_ref = (next(it), next(it)) if has_rope else (None, None)
    res_ref = next(it) if has_res else None
    o_ref, wbf_ref = next(it), next(it)

    @pl.when(pl.program_id(1) == 0)
    def _():
        wbf_ref[...] = w_ref[...].astype(BF16)

    acc = jnp.dot(x_ref[...].astype(BF16), wbf_ref[...], preferred_element_type=F32)
    if act is not None:
        acc = act(acc)
    if has_res:
        acc = res_ref[...] + res_scale * acc
    if has_gain:
        for s in range(acc.shape[1] // hw):
            y = acc[:, s * hw:(s + 1) * hw]
            ms = jnp.mean(y * y, axis=-1, keepdims=True)
            y = y * lax.rsqrt(ms + NORM_EPS) * gain_ref[...]
            if has_rope:
                lane = _iota(y.shape, 1)
                half = ROPE_DIM // 2
                rot = jnp.where(lane < half, pltpu.roll(y, hw - half, 1), pltpu.roll(y, half, 1))
                y = y * cos_ref[...] + rot * sin_ref[...]
            o_ref[:, s * hw:(s + 1) * hw] = y.astype(o_ref.dtype)
    else:
        o_ref[...] = acc.astype(o_ref.dtype)


def linear(x, w, *, col0=0, n=None, out_dtype=F32, gain=None, hw=None, rope=None, res=None,
           res_scale=1.0, act=None, layer=None, tm=LINEAR_TM, tn=None, name="linear"):
    m, k = x.shape
    n = w.shape[-1] - col0 if n is None else n
    if tn is None:
        tn = LINEAR_TN_WIDE if (n % LINEAR_TN_WIDE == 0 and col0 % LINEAR_TN_WIDE == 0
                                and k * LINEAR_TN_WIDE * 10 <= VMEM_LIMIT // 3 * 2) else LINEAR_TN
    tm, tn = min(tm, m), min(tn, n)
    assert m % tm == 0 and n % tn == 0 and col0 % tn == 0 and w.shape[-2] == k
    cb = col0 // tn
    if layer is None:
        w_spec = pl.BlockSpec((k, tn), lambda j, i: (0, j + cb))
    else:
        w_spec = pl.BlockSpec((None, k, tn), lambda j, i: (layer, 0, j + cb))
    in_specs = [pl.BlockSpec((tm, k), lambda j, i: (i, 0)), w_spec]
    args = [x, w]
    if gain is not None:
        assert tn % hw == 0
        in_specs.append(pl.BlockSpec((1, hw), lambda j, i: (0, 0)))
        args.append(gain.reshape(1, hw).astype(F32))
    if rope is not None:
        cos, sin = rope
        nt = cos.shape[0] // tm
        assert hw == LANES and cos.shape[0] % tm == 0
        in_specs += [pl.BlockSpec((tm, LANES), lambda j, i: (i % nt, 0))] * 2
        args += [cos, sin]
    if res is not None:
        in_specs.append(pl.BlockSpec((tm, tn), lambda j, i: (i, j)))
        args.append(res)
    body = functools.partial(_linear_body, has_gain=gain is not None, has_rope=rope is not None,
                             has_res=res is not None, act=act, hw=hw, res_scale=res_scale)
    return pl.pallas_call(
        body,
        grid=(n // tn, m // tm),
        in_specs=in_specs,
        out_specs=pl.BlockSpec((tm, tn), lambda j, i: (i, j)),
        out_shape=jax.ShapeDtypeStruct((m, n), out_dtype),
        scratch_shapes=[pltpu.VMEM((k, tn), BF16)],
        compiler_params=_params("parallel", "arbitrary"),
        name=name,
    )(*args)


def _rmsnorm_body(x_ref, g_ref, o_ref):
    x = x_ref[...]
    ms = jnp.mean(x * x, axis=-1, keepdims=True)
    o_ref[...] = (x * lax.rsqrt(ms + NORM_EPS) * g_ref[...]).astype(o_ref.dtype)


def rmsnorm(x, g, out_dtype=BF16, tm=512):
    m, d = x.shape
    tm = min(tm, m)
    assert m % tm == 0
    return pl.pallas_call(
        _rmsnorm_body,
        grid=(m // tm,),
        in_specs=[pl.BlockSpec((tm, d), lambda i: (i, 0)), pl.BlockSpec((1, d), lambda i: (0, 0))],
        out_specs=pl.BlockSpec((tm, d), lambda i: (i, 0)),
        out_shape=jax.ShapeDtypeStruct((m, d), out_dtype),
        compiler_params=_params("parallel"),
        name="rmsnorm",
    )(x, g.reshape(1, d))


def _ffn_body(x_ref, xr_ref, g_ref, wg_ref, wu_ref, wd_ref, o_ref, or_ref, xn_ref, *, tm):
    i, f = pl.program_id(0), pl.program_id(1)

    def start(src_ref, rows, dst_ref):
        x = src_ref[...]
        ms = jnp.mean(x * x, axis=-1, keepdims=True)
        xn_ref[rows, :] = (x * lax.rsqrt(ms + NORM_EPS) * g_ref[...]).astype(BF16)
        dst_ref[...] = x

    @pl.when(f == 0)
    def _():
        start(x_ref, slice(0, tm), o_ref)

    @pl.when((f == 0) & (i == 0))
    def _():
        start(xr_ref, slice(tm, tm + xr_ref.shape[0]), or_ref)

    xn = xn_ref[...]
    gate = jnp.dot(xn, wg_ref[...].astype(BF16), preferred_element_type=F32)
    up = jnp.dot(xn, wu_ref[...].astype(BF16), preferred_element_type=F32)
    h = (0.5 * _silu(gate) * up).astype(BF16)
    y = jnp.dot(h, wd_ref[...].astype(BF16), preferred_element_type=F32)
    o_ref[...] += y[:tm]
    or_ref[...] += jnp.where(i == 0, y[tm:], 0.0)


def ffn(x, x_rider, g, w_gate, w_up, w_down, layer, tm=1024, tf=512):
    m, d = x.shape
    mr = x_rider.shape[0]
    mr_pad = -(-mr // BF16_ROWS) * BF16_ROWS
    x_rider = jnp.pad(x_rider, ((0, mr_pad - mr), (0, 0)))
    f = w_gate.shape[-1]
    tm = min(tm, m)
    assert m % tm == 0 and f % tf == 0
    out, out_rider = pl.pallas_call(
        functools.partial(_ffn_body, tm=tm),
        grid=(m // tm, f // tf),
        in_specs=[pl.BlockSpec((tm, d), lambda i, j: (i, 0), pipeline_mode=pl.Buffered(1)),
                  pl.BlockSpec((mr_pad, d), lambda i, j: (0, 0)),
                  pl.BlockSpec((1, d), lambda i, j: (0, 0)),
                  pl.BlockSpec((None, d, tf), lambda i, j: (layer, 0, j)),
                  pl.BlockSpec((None, d, tf), lambda i, j: (layer, 0, j)),
                  pl.BlockSpec((None, tf, d), lambda i, j: (layer, j, 0))],
        out_specs=[pl.BlockSpec((tm, d), lambda i, j: (i, 0)),
                   pl.BlockSpec((mr_pad, d), lambda i, j: (0, 0))],
        out_shape=[jax.ShapeDtypeStruct((m, d), F32), jax.ShapeDtypeStruct((mr_pad, d), F32)],
        scratch_shapes=[pltpu.VMEM((tm + mr_pad, d), BF16)],
        compiler_params=_params("arbitrary", "arbitrary", vmem=FFN_VMEM_LIMIT),
        name="ffn",
    )(x, x_rider, g.reshape(1, d), w_gate, w_up, w_down)
    return out, out_rider[:mr]


def _memattn_body(q_ref, *refs, scale, head_axis):
    each = lambda f, *cols: [f(*args) for args in zip(*cols)]
    lanes = [slice(h * MEM_HD, (h + 1) * MEM_HD) for h in range(MEM_HEADS)]
    k_ref, v_ref, o_ref = refs
    if head_axis:
        ks = [k_ref[0, 0, :, h, :] for h in range(MEM_HEADS)]
        vs = [v_ref[0, 0, :, h, :] for h in range(MEM_HEADS)]
    else:
        ks = [k_ref[0, 0, :, sl] for sl in lanes]
        vs = [v_ref[0, 0, :, sl] for sl in lanes]
    s = each(lambda sl, k: _dot_nt(q_ref[0, :, sl], k) * scale, lanes, ks)
    m = each(lambda t: jnp.max(t, axis=-1, keepdims=True), s)
    p = each(lambda t, mm: jnp.exp(t - mm), s, m)
    l = each(lambda t: jnp.sum(t, axis=-1, keepdims=True), p)
    o = each(lambda t, v, ll: _dot(t, v) / ll, p, vs, l)
    for sl, t in zip(lanes, o):
        o_ref[0, :, sl] = t.astype(o_ref.dtype)


def mem_attention(q, mem_k, mem_v, layer, tq=1024):
    b, t, d = q.shape
    n_mem = mem_k.shape[2]
    tq = min(tq, t)
    assert t % tq == 0
    head_axis = mem_k.ndim == 5
    if head_axis:
        kv_specs = [pl.BlockSpec((1, 1, n_mem, MEM_HEADS, MEM_HD), lambda bi, ti: (layer, bi, 0, 0, 0))] * 2
        kv_args = [mem_k, mem_v]
    else:
        kv_specs = [pl.BlockSpec((1, 1, n_mem, d), lambda bi, ti: (layer, bi, 0, 0))] * 2
        kv_args = [mem_k, mem_v]
    return pl.pallas_call(
        functools.partial(_memattn_body, scale=MEM_HD ** -0.5, head_axis=head_axis),
        grid=(b, t // tq),
        in_specs=[pl.BlockSpec((1, tq, d), lambda bi, ti: (bi, ti, 0))] + kv_specs,
        out_specs=pl.BlockSpec((1, tq, d), lambda bi, ti: (bi, ti, 0)),
        out_shape=jax.ShapeDtypeStruct((b, t, d), BF16),
        compiler_params=_params("parallel", "parallel"),
        name="mem_attention",
    )(q, *kv_args)


def rope_tables(pos):
    half = ROPE_DIM // 2
    inv_freq = ROPE_THETA ** (-jnp.arange(half, dtype=F32) / half)
    ang = pos.astype(F32)[:, None] * inv_freq[None, :]
    cos, sin = jnp.cos(ang), jnp.sin(ang)
    rest = DIL_HD - ROPE_DIM
    n = pos.shape[0]
    cos_t = jnp.concatenate([cos, cos, jnp.ones((n, rest), F32)], axis=1)
    sin_t = jnp.concatenate([-sin, sin, jnp.zeros((n, rest), F32)], axis=1)
    return cos_t, sin_t


def _sb_tiles(qs, ks, vs, biases, causal, carries, accs, upper):
    each = lambda f, *cols: [f(*args) for args in zip(*cols)]
    mask = (lambda t: t) if causal is None else (lambda t: jnp.where(causal, t, 0.0))
    z = each(lambda q, k, b: _dot_nt(q, k) * (SB_HD ** -0.5) + b, qs, ks, biases)
    ls = each(_log_sigmoid, z)
    log_keep = each(lambda l, t: mask(l - t), ls, z)
    hi = each(lambda t: t.astype(BF16), log_keep)
    lo = each(lambda t, h: (t - h.astype(F32)).astype(BF16), log_keep, hi)
    local = each(lambda h, l: jnp.dot(jnp.concatenate([h, l], axis=0), upper, preferred_element_type=F32),
                 hi, lo)
    rows = qs[0].shape[0]
    att = each(lambda l, s, c: mask(jnp.exp(l + s[:rows] + s[rows:] + c)), ls, local, carries)
    accs = each(lambda a, p, v: a + _dot(p, v), accs, att, vs)
    carries = each(lambda c, t: c + jnp.sum(t, axis=-1, keepdims=True), carries, log_keep)
    return carries, accs


def _upper_ones(n):
    return jnp.where(_iota((n, n), 0) > _iota((n, n), 1), 1.0, 0.0).astype(BF16)


def _sb_prompt_body(bias_ref, q_ref, k_ref, v_ref, o_ref, *, tq, heads):
    hb, qi = pl.program_id(1), pl.program_id(2)
    upper = _upper_ones(tq)
    lanes = [slice(u * SB_HD, (u + 1) * SB_HD) for u in range(heads)]
    qs = [q_ref[0, :, sl] for sl in lanes]
    biases = [bias_ref[hb * heads + u] for u in range(heads)]

    def tiles(j, state, causal):
        start = pl.multiple_of(j * tq, tq)
        ks = [k_ref[0, pl.ds(start, tq), sl] for sl in lanes]
        vs = [v_ref[0, pl.ds(start, tq), sl] for sl in lanes]
        return _sb_tiles(qs, ks, vs, biases, causal, state[0], state[1], upper)

    init = ([jnp.zeros((tq, 1), F32) for _ in lanes], [jnp.zeros((tq, SB_HD), F32) for _ in lanes])
    state = tiles(qi, init, _iota((tq, tq), 1) < _iota((tq, tq), 0))
    _, accs = lax.fori_loop(0, qi, lambda jj, st: tiles(qi - 1 - jj, st, None), state)
    for sl, acc in zip(lanes, accs):
        o_ref[0, :, sl] = acc.astype(o_ref.dtype)


def sb_attention_prompt(q, k, v, bias, tq=256, heads=4):
    b, t, d = q.shape
    tq = min(tq, t)
    w = heads * SB_HD
    assert t % tq == 0 and d % w == 0
    kv_spec = pl.BlockSpec((1, t, w), lambda bi, h, qi: (bi, 0, h))
    io_spec = pl.BlockSpec((1, tq, w), lambda bi, h, qi: (bi, qi, h))
    return pl.pallas_call(
        functools.partial(_sb_prompt_body, tq=tq, heads=heads),
        grid=(b, d // w, t // tq),
        in_specs=[pl.BlockSpec(memory_space=pltpu.SMEM), io_spec, kv_spec, kv_spec],
        out_specs=io_spec,
        out_shape=jax.ShapeDtypeStruct((b, t, d), BF16),
        compiler_params=_params("parallel", "parallel", "arbitrary"),
        name="sb_attention_prompt",
    )(bias, q, k, v)


def _sb_decode_body(table_ref, bias_ref, q_ref, kn_ref, vn_ref, *rest, n_pages, per_step):
    del table_ref
    page_refs = rest[:2 * per_step]
    o_ref, acc_ref, carry_ref = rest[2 * per_step:]
    j = pl.program_id(1)
    nh, cols = SB_HEADS, PAGE_SIZE * SB_HEADS
    n_sub = cols // LANES
    bias = bias_ref[...]
    past_len = n_pages * PAGE_SIZE
    scale = SB_HD ** -0.5
    q = q_ref[0]

    @pl.when(j == 0)
    def _():
        z = jnp.sum(q * kn_ref[0], axis=-1, keepdims=True) * scale + bias
        causal = jnp.full((nh, 1), past_len, jnp.int32) < past_len
        ls = _log_sigmoid(z)
        carry_ref[...] = jnp.where(causal, ls - z, 0.0)
        acc_ref[...] = jnp.where(causal, jnp.exp(ls), 0.0) * vn_ref[0]

    each = lambda f, *c: [f(*args) for args in zip(*c)]
    col = _iota((nh, cols), 1)
    own = (col & (nh - 1)) == _iota((nh, cols), 0)
    upper = _upper_ones(LANES)
    pages = [n_pages - 1 - (j * per_step + u) for u in range(per_step)]
    masks = [own & ((p * PAGE_SIZE + col // nh) < past_len) for p in pages]
    ks = [page_refs[2 * u][...] for u in range(per_step)]
    vs = [page_refs[2 * u + 1][...] for u in range(per_step)]
    z = each(lambda k: _dot_nt(q, k) * scale + bias, ks)
    ls = each(_log_sigmoid, z)
    log_keep = each(lambda l, t, m: jnp.where(m, l - t, 0.0), ls, z, masks)

    def local_suffix(t):
        x = jnp.concatenate([t[:, c * LANES:(c + 1) * LANES] for c in range(n_sub)], axis=0)
        hi = x.astype(BF16)
        lo = (x - hi.astype(F32)).astype(BF16)
        y = jnp.dot(jnp.concatenate([hi, lo], axis=0), upper, preferred_element_type=F32)
        y = y[:n_sub * nh] + y[n_sub * nh:]
        return [y[c * nh:(c + 1) * nh] for c in range(n_sub)]

    local = each(local_suffix, log_keep)
    carry = carry_ref[...]
    between = []
    for t, loc in zip(log_keep, local):
        pieces = [None] * n_sub
        for c in reversed(range(n_sub)):
            pieces[c] = loc[c] + carry
            carry = carry + jnp.sum(t[:, c * LANES:(c + 1) * LANES], axis=-1, keepdims=True)
        between.append(jnp.concatenate(pieces, axis=1))
    carry_ref[...] = carry
    att = each(lambda l, s, m: jnp.where(m, jnp.exp(l + s), 0.0), ls, between, masks)
    acc_ref[...] += sum(each(_dot, att, vs))

    @pl.when(j == pl.num_programs(1) - 1)
    def _():
        o_ref[0] = acc_ref[...]


def sb_attention_decode(q, k_new, v_new, pool_k, pool_v, table, bias, per_step=4):
    b = q.shape[0]
    n_pages = table.shape[1]
    assert n_pages % per_step == 0 and pool_k.shape[1:] == (PAGE_SIZE, SB_HEADS, SB_HD)
    assert SB_HEADS & (SB_HEADS - 1) == 0 and (PAGE_SIZE * SB_HEADS) % LANES == 0
    page_rows = PAGE_SIZE * SB_HEADS
    pool_k = pool_k.reshape(-1, SB_HD)
    pool_v = pool_v.reshape(-1, SB_HD)
    row_spec = pl.BlockSpec((1, SB_HEADS, SB_HD), lambda bi, j, tab: (bi, 0, 0))

    def page_spec(u):
        return pl.BlockSpec((page_rows, SB_HD),
                            lambda bi, j, tab: (tab[bi, n_pages - 1 - (j * per_step + u)], 0))

    pages = [page_spec(u) for u in range(per_step) for _ in range(2)]
    grid_spec = pltpu.PrefetchScalarGridSpec(
        num_scalar_prefetch=1,
        grid=(b, n_pages // per_step),
        in_specs=[pl.BlockSpec((SB_HEADS, 1), lambda bi, j, tab: (0, 0)),
                  row_spec, row_spec, row_spec] + pages,
        out_specs=row_spec,
        scratch_shapes=[pltpu.VMEM((SB_HEADS, SB_HD), F32), pltpu.VMEM((SB_HEADS, 1), F32)],
    )
    return pl.pallas_call(
        functools.partial(_sb_decode_body, n_pages=n_pages, per_step=per_step),
        grid_spec=grid_spec,
        out_shape=jax.ShapeDtypeStruct((b, SB_HEADS, SB_HD), F32),
        compiler_params=_params("parallel", "arbitrary"),
        name="sb_attention_decode",
    )(table, bias.reshape(SB_HEADS, 1), q, k_new, v_new, *([pool_k, pool_v] * per_step))


def _dil_prompt_body(*refs, t, blk):
    n_grp = len(DIL_PATTERNS)
    q_refs, k_refs, v_refs = refs[:n_grp], refs[n_grp:2 * n_grp], refs[2 * n_grp:3 * n_grp]
    o_ref, m_s, l_s, acc_s = refs[3 * n_grp:3 * n_grp + 4]
    scale = DIL_HD ** -0.5
    order = sorted(range(n_grp), key=lambda g: -DIL_PATTERNS[g][1])
    for n, g in enumerate(order):
        win, dil = DIL_PATTERNS[g]
        band, cls_len = win // dil, t // dil
        n_blk = cls_len // blk
        n_keys = min(2 * blk, cls_len)
        first, last = n == 0, n == n_grp - 1
        assert band == blk and (not last or dil == 1) and n_blk * blk == cls_len
        q_ref, k_ref, v_ref = q_refs[g], k_refs[g], v_refs[g]
        rel = _iota((blk, n_keys), 0) - _iota((blk, n_keys), 1)

        def rows(start, size, dil=dil):
            return pl.ds(start, size, stride=dil) if dil > 1 else pl.ds(start, size)

        def blocks(i, _, q_ref=q_ref, k_ref=k_ref, v_ref=v_ref, dil=dil, n_blk=n_blk, n_keys=n_keys,
                   rel=rel, rows=rows, first=first, last=last, band=band):
            each = lambda f, *cols: [f(*args) for args in zip(*cols)]
            idxs = [i * DIL_BLOCKS_PER_ITER + u for u in range(DIL_BLOCKS_PER_ITER)]
            lbs = [idx % n_blk for idx in idxs]
            k_cls = [jnp.maximum(lb - 1, 0) * blk for lb in lbs]
            if dil == 1:
                q_tok = [pl.multiple_of(lb * blk, blk) for lb in lbs]
                k_tok = [pl.multiple_of(kc, blk) for kc in k_cls]
            else:
                q_tok = [idx // n_blk + lb * blk * dil for idx, lb in zip(idxs, lbs)]
                k_tok = [idx // n_blk + kc * dil for idx, kc in zip(idxs, k_cls)]
            sel = [rows(t, blk) for t in q_tok]
            q = [q_ref[0, s, :] for s in sel]
            k = [k_ref[0, rows(t, n_keys), :] for t in k_tok]
            v = [v_ref[0, rows(t, n_keys), :] for t in k_tok]
            old = None if first else [(m_s[s, :], l_s[s, :], acc_s[s, :]) for s in sel]

            def scores(qq, kk, lb, kc):
                dist = rel + (lb * blk - kc)
                valid = jnp.where(dist >= 0, dist, band + 1) <= band
                return jnp.where(valid, _dot_nt(qq, kk) * scale, NEG_BIG)

            s = each(scores, q, k, lbs, k_cls)
            m = each(lambda t: jnp.max(t, axis=-1, keepdims=True), s)
            p = each(lambda t, mm: jnp.exp(t - mm), s, m)
            l = each(lambda t: jnp.sum(t, axis=-1, keepdims=True), p)
            acc = each(_dot, p, v)
            if not first:
                m_new = each(lambda o, mm: jnp.maximum(o[0], mm), old, m)
                e0 = each(lambda o, mn: jnp.exp(o[0] - mn), old, m_new)
                e1 = each(lambda mm, mn: jnp.exp(mm - mn), m, m_new)
                l = each(lambda o, a, b, t: a * o[1] + b * t, old, e0, e1, l)
                acc = each(lambda o, a, b, t: a * o[2] + b * t, old, e0, e1, acc)
                m = m_new
            for u, sl in enumerate(sel):
                if last:
                    o_ref[0, sl, :] = (acc[u] / l[u]).astype(o_ref.dtype)
                else:
                    m_s[sl, :], l_s[sl, :], acc_s[sl, :] = m[u], l[u], acc[u]
            return 0

        assert (dil * n_blk) % DIL_BLOCKS_PER_ITER == 0
        lax.fori_loop(0, dil * n_blk // DIL_BLOCKS_PER_ITER, blocks, 0)


def dil_attention_prompt(q, k, v, blk=128):
    b, t, _ = q.shape
    n_grp = len(DIL_PATTERNS)
    assert t % blk == 0

    def spec(g):
        return pl.BlockSpec((1, t, DIL_HD), lambda bi, h: (bi, 0, g * DIL_HEADS + h))

    return pl.pallas_call(
        functools.partial(_dil_prompt_body, t=t, blk=blk),
        grid=(b, DIL_HEADS),
        in_specs=[spec(g) for g in range(n_grp)] * 3,
        out_specs=pl.BlockSpec((1, t, DIL_HD), lambda bi, h: (bi, 0, h)),
        out_shape=jax.ShapeDtypeStruct((b, t, DIL_HEADS * DIL_HD), BF16),
        scratch_shapes=[pltpu.VMEM((t, 1), F32), pltpu.VMEM((t, 1), F32), pltpu.VMEM((t, DIL_HD), F32)],
        compiler_params=_params("parallel", "parallel"),
        name="dil_attention_prompt",
    )(*([q] * n_grp + [k] * n_grp + [v] * n_grp))


def _dil_decode_body(*refs):
    n_grp = len(DIL_PATTERNS)
    q_ref, kn_ref, vn_ref = refs[:3]
    kb_refs, vb_refs = refs[3:3 + n_grp], refs[3 + n_grp:3 + 2 * n_grp]
    o_ref = refs[3 + 2 * n_grp]
    scale = DIL_HD ** -0.5
    nh = DIL_HEADS
    scores, news, masks = [], [], []
    for g in range(n_grp):
        q = q_ref[0, g]
        band = kb_refs[g].shape[1]
        k2 = kb_refs[g][0].reshape(band * nh, DIL_HD)
        own = (_iota((nh, band * nh), 1) & (nh - 1)) == _iota((nh, band * nh), 0)
        masks.append(own)
        scores.append(jnp.where(own, _dot_nt(q, k2) * scale, NEG_BIG))
        news.append(jnp.sum(q * kn_ref[0, g], axis=-1, keepdims=True) * scale)
    m = functools.reduce(jnp.maximum, [jnp.max(s, axis=-1, keepdims=True) for s in scores] + news)
    l = jnp.zeros((nh, 1), F32)
    acc = jnp.zeros((nh, DIL_HD), F32)
    for g in range(n_grp):
        p = jnp.where(masks[g], jnp.exp(scores[g] - m), 0.0)
        pn = jnp.exp(news[g] - m)
        v2 = vb_refs[g][0].reshape(p.shape[1], DIL_HD)
        l = l + jnp.sum(p, axis=-1, keepdims=True) + pn
        acc = acc + _dot(p, v2) + pn * vn_ref[0, g]
    o_ref[0] = acc / l


def dil_attention_decode(q, k_new, v_new, bufs):
    b = q.shape[0]
    n_grp = len(DIL_PATTERNS)
    assert DIL_HEADS & (DIL_HEADS - 1) == 0
    row_spec = pl.BlockSpec((1, n_grp, DIL_HEADS, DIL_HD), lambda bi: (bi, 0, 0, 0))
    views, specs = [], []
    for which in range(2):
        for g, (win, dil) in enumerate(DIL_PATTERNS):
            buf = bufs[2 * g + which]
            assert buf.shape[1] == win and win % dil == 0
            views.append(buf.reshape(b, win // dil, dil * DIL_HEADS, DIL_HD))
            specs.append(pl.BlockSpec((1, win // dil, DIL_HEADS, DIL_HD), lambda bi: (bi, 0, 0, 0)))
    return pl.pallas_call(
        _dil_decode_body,
        grid=(b,),
        in_specs=[row_spec] * 3 + specs,
        out_specs=pl.BlockSpec((1, DIL_HEADS, DIL_HD), lambda bi: (bi, 0, 0)),
        out_shape=jax.ShapeDtypeStruct((b, DIL_HEADS, DIL_HD), F32),
        compiler_params=_params("parallel"),
        name="dil_attention_decode",
    )(q, k_new, v_new, *views)


def _rwkv_mix_body(x_ref, g_ref, shift_ref, mu_ref, *refs, tt):
    o_refs, xn_ref, buf_ref = refs[:-2], refs[-2], refs[-1]
    halo = 8

    @pl.when(pl.program_id(1) == 0)
    def _():
        buf_ref[0:halo, :] = jnp.broadcast_to(shift_ref[0], (halo, shift_ref.shape[-1]))

    x = x_ref[0]
    inv = lax.rsqrt(jnp.mean(x * x, axis=-1, keepdims=True) + NORM_EPS)
    width = 2 * LANES
    for c in range(x.shape[1] // width):
        sl = slice(c * width, (c + 1) * width)
        xn = x_ref[0, :, sl] * inv * g_ref[:, sl]
        buf_ref[halo:halo + tt, sl] = xn
        xx = buf_ref[halo - 1:halo - 1 + tt, sl] - xn
        for i, o_ref in enumerate(o_refs):
            o_ref[0, :, sl] = (xn + xx * mu_ref[i:i + 1, sl]).astype(o_ref.dtype)
        xn_ref[0, :, sl] = xn
    buf_ref[0:halo, :] = buf_ref[tt:tt + halo, :]


def rwkv_mix(x, g, shift0, mu, tt=128):
    b, t, d = x.shape
    tt = math.gcd(tt, t)
    assert tt % 8 == 0
    n_mix = mu.shape[0]
    seq = pl.BlockSpec((1, tt, d), lambda bi, ti: (bi, ti, 0))
    outs = pl.pallas_call(
        functools.partial(_rwkv_mix_body, tt=tt),
        grid=(b, t // tt),
        in_specs=[seq, pl.BlockSpec((1, d), lambda bi, ti: (0, 0)),
                  pl.BlockSpec((1, 1, d), lambda bi, ti: (bi, 0, 0)),
                  pl.BlockSpec((n_mix, d), lambda bi, ti: (0, 0))],
        out_specs=[seq] * n_mix + [pl.BlockSpec((1, tt, d), lambda bi, ti: (bi, 0, 0))],
        out_shape=[jax.ShapeDtypeStruct((b, t, d), BF16)] * n_mix + [jax.ShapeDtypeStruct((b, tt, d), F32)],
        scratch_shapes=[pltpu.VMEM((tt + 8, d), F32)],
        compiler_params=_params("parallel", "arbitrary"),
        name="rwkv_mix",
    )(x, g.reshape(1, d), shift0.astype(F32).reshape(b, 1, d), mu)
    return outs[:n_mix], outs[n_mix]


def _rwkv_scan_body(r_ref, k_ref, v_ref, wl_ref, al_ref, g_ref, prm_ref, s0_ref, y_ref, sf_ref,
                    st_ref, *, t_valid, t_total):
    c = pl.program_id(1)
    C, hd = RWKV_CHUNK, RWKV_HD
    n2 = 2 * C
    grp = RWKV_PAIRS_PER_DOT
    rows = grp * n2

    @pl.when(c == 0)
    def _():
        st_ref[...] = s0_ref[0]

    left = _iota((C, LANES), 1) < hd
    ri, ci = _iota((rows, rows), 0), _iota((rows, rows), 1)
    strict = (ci & (C - 1)) < (ri & (C - 1))
    incl = (ci & (C - 1)) <= (ri & (C - 1))
    tri = jnp.where(_iota((C, C), 1) <= _iota((C, C), 0), 1.0, 0.0).astype(BF16)
    same_head = (_iota((LANES, LANES), 0) // hd) == (_iota((LANES, LANES), 1) // hd)
    seg_ones = jnp.where(same_head, 1.0, 0.0).astype(BF16)
    masked = t_valid < t_total
    rows_valid = (c * C + _iota((C, LANES), 0)) < t_valid
    zero_slab = jnp.zeros((n2, LANES), F32)

    def seg_sum(x):
        hi = x.astype(BF16)
        lo = (x - hi.astype(F32)).astype(BF16)
        y = jnp.dot(jnp.concatenate([hi, lo], axis=0), seg_ones, preferred_element_type=F32)
        return y[:x.shape[0]] + y[x.shape[0]:]

    def cumsum_steps(x):
        w = x.shape[1]
        y = jnp.dot(tri, jnp.concatenate(_split3(x), axis=1), preferred_element_type=F32)
        return y[:, :w] + y[:, w:2 * w] + y[:, 2 * w:]

    def stack(x):
        return jnp.concatenate([jnp.where(left, x, 0.0), jnp.where(left, 0.0, x)], axis=0)

    def spread(slabs):
        return jnp.concatenate(
            [jnp.concatenate([s if q == j else zero_slab for j in range(grp)], axis=1)
             for q, s in enumerate(slabs)], axis=0)

    def core(groups):
        each = lambda f, *cols: [f(*args) for args in zip(*cols)]
        dss, stss = [g[0] for g in groups], [g[1] for g in groups]
        flat_d = [d for ds in dss for d in ds]
        flat_st = [st for sts in stss for st in sts]
        ar = each(lambda d: jnp.concatenate([stack(d['a_t']), stack(d['r_t'])], axis=0).astype(BF16), flat_d)
        bk = each(lambda d: jnp.concatenate([stack(d['b_t']), stack(d['k_t'])], axis=0).astype(BF16), flat_d)
        gram_p = each(_dot_nt, ar, bk)
        a_s_p = each(_dot_nt, ar, flat_st)
        by_group = lambda xs: [xs[q:q + grp] for q in range(0, len(xs), grp)]
        quadrant = lambda gs, r0, c0: spread([g[r0:r0 + n2, c0:c0 + n2] for g in gs])
        vv = each(lambda ds: jnp.concatenate([d['v'] for d in ds for _ in range(2)], axis=0), dss)
        y = each(lambda gs, v, ss: _dot(jnp.where(strict, quadrant(gs, 0, n2), 0.0), v)
                 + jnp.concatenate([s[:n2] for s in ss], axis=0),
                 by_group(gram_p), vv, by_group(a_s_p))
        pw = each(lambda gs: jnp.where(strict, quadrant(gs, 0, 0), 0.0), by_group(gram_p))
        n = 1
        while n < C:
            y = each(lambda p, t: t + _dot(p, t), pw, y)
            n *= 2
            if n < C:
                pw = each(lambda p: _dot(p, p), pw)
        o2 = each(lambda gs, t, v, ss: _dot(
            jnp.concatenate([jnp.where(incl, quadrant(gs, n2, 0), 0.0),
                             jnp.where(incl, quadrant(gs, n2, n2), 0.0)], axis=1),
            jnp.concatenate([t, v], axis=0)) + jnp.concatenate([s[n2:] for s in ss], axis=0),
            by_group(gram_p), y, vv, by_group(a_s_p))
        pick = lambda t, q: jnp.where(left, t[q * n2:q * n2 + C], t[q * n2 + C:(q + 1) * n2])
        upd = each(lambda ds, t: _dot(
            jnp.concatenate([jnp.concatenate([pick(t, q), d['v']], axis=0)
                             for q, d in enumerate(ds)], axis=1).T,
            jnp.concatenate([jnp.concatenate([d['b_end'], d['k_end']], axis=0) for d in ds], axis=1)),
            dss, y)
        os, s_new = [], []
        for ds, sts, o, up in zip(dss, stss, o2, upd):
            os += [pick(o, q) for q in range(grp)]
            s_new += [jnp.where(same_head, st * d['decay_end']
                                + up[q * LANES:(q + 1) * LANES, q * LANES:(q + 1) * LANES], 0.0)
                      for q, (d, st) in enumerate(zip(ds, sts))]
        return os, s_new

    def pairs(i, _):
        ps = [i * RWKV_PAIRS_PER_ITER + u for u in range(RWKV_PAIRS_PER_ITER)]
        offs = [pl.multiple_of(p * LANES, LANES) for p in ps]
        sts = [st_ref[p] for p in ps]
        ds = []
        for off in offs:
            r, k, v, wl, al, g = [ref[0, :, pl.ds(off, LANES)]
                                  for ref in (r_ref, k_ref, v_ref, wl_ref, al_ref, g_ref)]
            prm = prm_ref[:, pl.ds(off, LANES)]
            w0, a0, k_k, k_a, r_k, gn_w, gn_b = [prm[j:j + 1] for j in range(7)]
            a = jax.nn.sigmoid(a0 + al)
            ds.append(dict(r=r, v=v, g=g, a=a, gn_w=gn_w, gn_b=gn_b, r_k=r_k, kkr=k * k_k,
                           lw=-jnp.exp(-_softplus(-(w0 + wl)) - 0.5),
                           kmod=k * (1.0 + (a - 1.0) * k_a)))
        sums = seg_sum(jnp.concatenate([d['kkr'] * d['kkr'] for d in ds]
                                       + [d['r'] * d['kmod'] * d['r_k'] for d in ds], axis=0))
        n_p = len(ds)
        for u, d in enumerate(ds):
            d['kk'] = d['kkr'] * lax.rsqrt(jnp.maximum(sums[u * C:(u + 1) * C], 1e-24))
            d['bonus'] = sums[(n_p + u) * C:(n_p + u + 1) * C] * d['v']
            if masked:
                for name in ('lw', 'kk', 'kmod', 'v'):
                    d[name] = jnp.where(rows_valid, d[name], 0.0)
        lc_all = cumsum_steps(jnp.concatenate([d['lw'] for d in ds], axis=1))
        for u, d in enumerate(ds):
            lc = lc_all[:, u * LANES:(u + 1) * LANES]
            lend = lc[C - 1:C, :]
            kka = d['kk'] * d['a']
            e_inv, e_end = jnp.exp(-lc), jnp.exp(lend - lc)
            d.update(a_t=-d['kk'] * jnp.exp(lc - d['lw']), r_t=d['r'] * jnp.exp(lc),
                     b_t=kka * e_inv, k_t=d['kmod'] * e_inv, b_end=kka * e_end,
                     k_end=d['kmod'] * e_end, decay_end=jnp.exp(lend))
        os, s_news = core([(ds[q:q + grp], sts[q:q + grp]) for q in range(0, n_p, grp)])
        stats = jnp.dot(jnp.concatenate(os + [o * o for o in os], axis=0).astype(BF16), seg_ones,
                        preferred_element_type=F32)
        for u, (p, off, d, o) in enumerate(zip(ps, offs, ds, os)):
            mean = stats[u * C:(u + 1) * C] * (1.0 / hd)
            var = stats[(n_p + u) * C:(n_p + u + 1) * C] * (1.0 / hd) - mean * mean
            on = (o - mean) * lax.rsqrt(var + GN_EPS) * d['gn_w'] + d['gn_b']
            st_ref[p] = s_news[u]
            y_ref[0, :, pl.ds(off, LANES)] = ((on + d['bonus']) * d['g'])[:y_ref.shape[1]].astype(y_ref.dtype)
        return 0

    lax.fori_loop(0, D_MODEL // LANES // RWKV_PAIRS_PER_ITER, pairs, 0)

    @pl.when(c == pl.num_programs(1) - 1)
    def _():
        sf_ref[0] = st_ref[...]


def rwkv_scan(r, k, v, wl, al, g, prm, s0, t_valid):
    b, t, d = r.shape
    C = RWKV_CHUNK
    assert t % C == 0 and d == D_MODEL
    seq = pl.BlockSpec((1, C, d), lambda bi, c: (bi, c, 0))
    st_spec = pl.BlockSpec((1, d // LANES, LANES, LANES), lambda bi, c: (bi, 0, 0, 0))
    y_rows = BF16_ROWS if (t == C and t_valid <= BF16_ROWS) else C
    return pl.pallas_call(
        functools.partial(_rwkv_scan_body, t_valid=t_valid, t_total=t),
        grid=(b, t // C),
        in_specs=[seq] * 6 + [pl.BlockSpec((8, d), lambda bi, c: (0, 0)), st_spec],
        out_specs=[pl.BlockSpec((1, y_rows, d), lambda bi, c: (bi, c, 0)), st_spec],
        out_shape=[jax.ShapeDtypeStruct((b, t // C * y_rows, d), BF16),
                   jax.ShapeDtypeStruct((b, d // LANES, LANES, LANES), F32)],
        scratch_shapes=[pltpu.VMEM((d // LANES, LANES, LANES), F32)],
        compiler_params=_params("parallel", "arbitrary"),
        name="rwkv_scan",
    )(r, k, v, wl, al, g, prm, s0)


def _conv_body(x_ref, prev_ref, w_ref, b_ref, o_ref, buf_ref, *, tt):
    halo = 8

    @pl.when(pl.program_id(2) == 0)
    def _():
        buf_ref[0:halo, :] = prev_ref[0]

    buf_ref[halo:halo + tt, :] = x_ref[0]
    for c in range(x_ref.shape[2] // LANES):
        sl = slice(c * LANES, (c + 1) * LANES)
        acc = b_ref[:, sl] + x_ref[0, :, sl] * w_ref[SSM_CONV - 1:SSM_CONV, sl]
        for back in range(1, SSM_CONV):
            tap = SSM_CONV - 1 - back
            acc = acc + buf_ref[halo - back:halo - back + tt, sl] * w_ref[tap:tap + 1, sl]
        o_ref[0, :, sl] = _silu(acc)
    buf_ref[0:halo, :] = buf_ref[tt:tt + halo, :]


def causal_conv_silu(x, prev8, w, bias, tt=256, tc=1024):
    b, t, ch = x.shape
    tt = min(tt, t)
    assert t % tt == 0 and ch % tc == 0
    return pl.pallas_call(
        functools.partial(_conv_body, tt=tt),
        grid=(b, ch // tc, t // tt),
        in_specs=[pl.BlockSpec((1, tt, tc), lambda bi, ci, ti: (bi, ti, ci)),
                  pl.BlockSpec((1, 8, tc), lambda bi, ci, ti: (bi, 0, ci)),
                  pl.BlockSpec((SSM_CONV, tc), lambda bi, ci, ti: (0, ci)),
                  pl.BlockSpec((1, tc), lambda bi, ci, ti: (0, ci))],
        out_specs=pl.BlockSpec((1, tt, tc), lambda bi, ci, ti: (bi, ti, ci)),
        out_shape=jax.ShapeDtypeStruct((b, t, ch), F32),
        scratch_shapes=[pltpu.VMEM((tt + 8, tc), F32)],
        compiler_params=_params("parallel", "parallel", "arbitrary"),
        name="causal_conv_silu",
    )(x, prev8, w, bias.reshape(1, ch))


def _ssd_body(xa_ref, z_ref, dt_ref, dtt_ref, dtb_ref, dtbt_ref, al_ref, alt_ref, dsk_ref, nw_ref,
              h0_ref, y_ref, hf_ref, h_ref, *, t_valid, t_total):
    c = pl.program_id(1)
    Q, P = SSM_CHUNK, SSM_HEADDIM
    per_grp = SSM_HEADS // SSM_GROUPS
    gw = per_grp * P

    @pl.when(c == 0)
    def _():
        h_ref[...] = h0_ref[0]

    tri = _iota((Q, Q), 1) <= _iota((Q, Q), 0)
    tri_b = jnp.where(tri, 1.0, 0.0).astype(BF16)
    upp_b = jnp.where(_iota((Q, Q), 0) <= _iota((Q, Q), 1), 1.0, 0.0).astype(BF16)
    left = _iota((Q, LANES), 1) < P
    masked = t_valid < t_total

    def one_hot_rows(width, per):
        head = _iota((3 * per_grp, width), 0) % per_grp
        return jnp.where(_iota((3 * per_grp, width), 1) // per == head, 1.0, 0.0).astype(BF16)

    to_heads = one_hot_rows(gw, P)
    to_slabs = one_hot_rows(per_grp * Q, Q)

    def group(g):
        dt = _softplus(dt_ref[0, g] + dtb_ref[g])
        dtt = _softplus(dtt_ref[0, g] + dtbt_ref[g])
        if masked:
            dt = jnp.where(c * Q + _iota(dt.shape, 0) < t_valid, dt, 0.0)
            dtt = jnp.where(c * Q + _iota(dtt.shape, 1) < t_valid, dtt, 0.0)
        cum = _dot_exact_lhs(tri_b, dt * -jnp.exp(al_ref[g]))
        cumt = _dot_exact_rhs(dtt * -jnp.exp(alt_ref[g]), upp_b)
        yield
        spread_cols = lambda t, e3: jnp.dot(jnp.concatenate(_split3(t), axis=1), e3,
                                            preferred_element_type=F32)
        ecum_w = spread_cols(jnp.exp(cum), to_heads)
        tail_w = spread_cols(jnp.exp(cum[Q - 1:Q, :] - cum) * dt, to_heads)
        cum_w = spread_cols(cum, to_slabs)
        yield
        boff = pl.multiple_of(SSM_DINNER + g * SSM_STATE, SSM_STATE)
        coff = pl.multiple_of(SSM_DINNER + SSM_GROUPS * SSM_STATE + g * SSM_STATE, SSM_STATE)
        roff = pl.multiple_of(g * gw, gw)
        bm = xa_ref[0, :, pl.ds(boff, SSM_STATE)]
        cm = xa_ref[0, :, pl.ds(coff, SSM_STATE)]
        cb = _dot_nt(cm, bm)
        hg = h_ref[pl.ds(roff, gw), :]
        y_state = _dot_nt(cm, hg)
        xg = xa_ref[0, :, pl.ds(roff, gw)]
        yield
        xps = [xg[:, q * LANES:(q + 1) * LANES] for q in range(gw // LANES)]

        def head_matrix(hh):
            seg = cum_w[:, hh * Q:(hh + 1) * Q] - cumt[hh:hh + 1, :]
            dec = jnp.where(tri, jnp.exp(jnp.where(tri, seg, 0.0)), 0.0)
            return (cb * dec * dtt[hh:hh + 1, :]).astype(BF16)

        mats = [head_matrix(hh) for hh in range(per_grp)]
        yield
        prods = [_dot(mats[hh], xps[hh // 2]) for hh in range(per_grp)]
        yield
        ys, xts = [], []
        for q, xp in enumerate(xps):
            sl = slice(q * LANES, (q + 1) * LANES)
            ys.append(jnp.where(left, prods[2 * q], prods[2 * q + 1]) + y_state[:, sl] * ecum_w[:, sl])
            xts.append(xp * tail_w[:, sl])
        upd = _dot(jnp.concatenate(xts, axis=1).T, bm)
        yield
        for hh in range(per_grp):
            rows = slice(hh * P, (hh + 1) * P)
            h_ref[pl.ds(pl.multiple_of(roff + hh * P, P), P), :] = (
                hg[rows] * jnp.exp(cumt[hh:hh + 1, Q - 1:Q]) + upd[rows])
        yg = jnp.concatenate(ys, axis=1)
        yg = (yg + xg * dsk_ref[:, pl.ds(roff, gw)]) * _silu(z_ref[0, :, pl.ds(roff, gw)])
        ms = jnp.mean(yg * yg, axis=-1, keepdims=True)
        y_ref[0, :, pl.ds(roff, gw)] = (
            yg * lax.rsqrt(ms + SSM_NORM_EPS) * nw_ref[:, pl.ds(roff, gw)]).astype(y_ref.dtype)

    def groups(i, _):
        stages = [group(i * SSD_GROUPS_PER_ITER + u) for u in range(SSD_GROUPS_PER_ITER)]
        for _ in itertools.zip_longest(*stages):
            pass
        return 0

    lax.fori_loop(0, SSM_GROUPS // SSD_GROUPS_PER_ITER, groups, 0)

    @pl.when(c == pl.num_programs(1) - 1)
    def _():
        hf_ref[0] = h_ref[...]


def ssd_scan(xa, z, dt_raw, dt_bias, a_log, d_skip, norm_w, h0, t_valid):
    b, t, _ = xa.shape
    Q = SSM_CHUNK
    per_grp = SSM_HEADS // SSM_GROUPS
    assert t % Q == 0
    dt_g = dt_raw.reshape(b, t, SSM_GROUPS, per_grp).transpose(0, 2, 1, 3)
    dt_gt = dt_g.transpose(0, 1, 3, 2)
    grp = lambda p: p.reshape(SSM_GROUPS, 1, per_grp)
    grp_t = lambda p: p.reshape(SSM_GROUPS, per_grp, 1)
    full3 = lambda s: pl.BlockSpec(s, lambda bi, c: (0, 0, 0))
    lanes = lambda w: pl.BlockSpec((1, w), lambda bi, c: (0, 0))
    st_spec = pl.BlockSpec((1, SSM_DINNER, SSM_STATE), lambda bi, c: (bi, 0, 0))
    return pl.pallas_call(
        functools.partial(_ssd_body, t_valid=t_valid, t_total=t),
        grid=(b, t // Q),
        in_specs=[pl.BlockSpec((1, Q, SSM_CONV_DIM), lambda bi, c: (bi, c, 0)),
                  pl.BlockSpec((1, Q, SSM_DINNER), lambda bi, c: (bi, c, 0)),
                  pl.BlockSpec((1, SSM_GROUPS, Q, per_grp), lambda bi, c: (bi, 0, c, 0)),
                  pl.BlockSpec((1, SSM_GROUPS, per_grp, Q), lambda bi, c: (bi, 0, 0, c)),
                  full3((SSM_GROUPS, 1, per_grp)), full3((SSM_GROUPS, per_grp, 1)),
                  full3((SSM_GROUPS, 1, per_grp)), full3((SSM_GROUPS, per_grp, 1)),
                  lanes(SSM_DINNER), lanes(SSM_DINNER), st_spec],
        out_specs=[pl.BlockSpec((1, Q, SSM_DINNER), lambda bi, c: (bi, c, 0)), st_spec],
        out_shape=[jax.ShapeDtypeStruct((b, t, SSM_DINNER), BF16),
                   jax.ShapeDtypeStruct((b, SSM_DINNER, SSM_STATE), F32)],
        scratch_shapes=[pltpu.VMEM((SSM_DINNER, SSM_STATE), F32)],
        compiler_params=_params("parallel", "arbitrary"),
        name="ssd_scan",
    )(xa, z, dt_g, dt_gt, grp(dt_bias), grp_t(dt_bias), grp(a_log), grp_t(a_log),
      jnp.repeat(d_skip, SSM_HEADDIM).reshape(1, SSM_DINNER), norm_w.reshape(1, SSM_DINNER), h0)


def _pad_time(x3, mult):
    t = x3.shape[1]
    tp = -(-t // mult) * mult
    return x3 if tp == t else jnp.pad(x3, ((0, 0), (0, tp - t), (0, 0)))


def _rwkv_mixer(xf, b, t, ln, shift0, wkv0, W):
    d = D_MODEL
    x_p = _pad_time(xf.reshape(b, t, d), RWKV_CHUNK)
    tp = x_p.shape[1]
    mixes, xn_tail = rwkv_mix(x_p, ln, shift0, W['rwkv_mu'])
    xr, xw, xk, xv, xa, xg = [m.reshape(b * tp, d) for m in mixes]
    shift_out = xn_tail[:, (t - 1) % xn_tail.shape[1]]
    r = linear(xr, W['rwkv_wr'])
    k = linear(xk, W['rwkv_wk'])
    v = linear(xv, W['rwkv_wv'])
    wl = linear(linear(xw, W['rwkv_w1'], act=jnp.tanh, out_dtype=BF16), W['rwkv_w2'])
    al = linear(linear(xa, W['rwkv_a1'], out_dtype=BF16), W['rwkv_a2'])
    g = linear(linear(xg, W['rwkv_g1'], act=jax.nn.sigmoid, out_dtype=BF16), W['rwkv_g2'])
    prm = jnp.stack([W['rwkv_w0'], W['rwkv_a0'], W['rwkv_kk'], W['rwkv_ka'], W['rwkv_rk'].reshape(d),
                     W['rwkv_gn_w'], W['rwkv_gn_b'], jnp.zeros((d,), F32)])
    n_pair = d // LANES
    s4 = wkv0.astype(F32).reshape(b, n_pair, 2, RWKV_HD, RWKV_HD)
    zero = jnp.zeros_like(s4[:, :, 0])
    s0 = jnp.concatenate([jnp.concatenate([s4[:, :, 0], zero], axis=-1),
                          jnp.concatenate([zero, s4[:, :, 1]], axis=-1)], axis=-2)
    to3 = lambda a: a.reshape(b, tp, d)
    y, sf = rwkv_scan(to3(r), to3(k), to3(v), to3(wl), to3(al), to3(g), prm, s0, t_valid=t)
    wkv = jnp.stack([sf[:, :, :RWKV_HD, :RWKV_HD], sf[:, :, RWKV_HD:, RWKV_HD:]], axis=2)
    y = y[:, :t].reshape(b * t, d)
    return linear(y, W['rwkv_wo'], res=xf), shift_out, wkv.reshape(b, d // RWKV_HD, RWKV_HD, RWKV_HD)


def _sb_mixer(xf, b, t, ln, sb_past, W):
    d = D_MODEL
    xn = rmsnorm(xf, ln)
    qdt = BF16 if sb_past is None else F32
    q = linear(xn, W['sb_wqkv'], col0=0, n=d, gain=W['sb_gq'], hw=SB_HD, out_dtype=qdt)
    k = linear(xn, W['sb_wqkv'], col0=d, n=d, gain=W['sb_gk'], hw=SB_HD)
    v = linear(xn, W['sb_wqkv'], col0=2 * d, n=d)
    to3 = lambda a: a.reshape(b, t, d)
    if sb_past is None:
        o = sb_attention_prompt(to3(q), to3(k), to3(v), W['sb_bias'])
    else:
        pool_k, pool_v, table = sb_past
        assert t == 1
        heads = lambda a: a.reshape(b, SB_HEADS, SB_HD)
        o = sb_attention_decode(heads(q), heads(k), heads(v), pool_k, pool_v, table, W['sb_bias'])
    xf = linear(o.reshape(b * t, d), W['sb_wo'], res=xf)
    return xf, k.reshape(b, t, SB_HEADS, SB_HD), v.reshape(b, t, SB_HEADS, SB_HD)


def _dil_mixer(xf, b, t, pos0, ln, dil_bufs, W):
    n_grp = len(DIL_PATTERNS)
    gd = n_grp * DIL_HEADS * DIL_HD
    xn = rmsnorm(xf, ln)
    rows = t if t >= 8 else b * t
    rope = rope_tables(pos0 + (jnp.arange(rows, dtype=jnp.int32) % t))
    tm = min(rows, LINEAR_TM)
    q = linear(xn, W['dil_wqkv'], col0=0, n=gd, gain=W['dil_gq'], hw=DIL_HD, rope=rope, tm=tm)
    k = linear(xn, W['dil_wqkv'], col0=gd, n=gd, gain=W['dil_gk'], hw=DIL_HD, rope=rope, tm=tm)
    v = linear(xn, W['dil_wqkv'], col0=2 * gd, n=gd, tm=tm)
    to3 = lambda a: a.reshape(b, t, gd)
    if dil_bufs is None:
        o = dil_attention_prompt(to3(q), to3(k), to3(v))
    else:
        assert t == 1
        heads = lambda a: a.reshape(b, n_grp, DIL_HEADS, DIL_HD)
        o = dil_attention_decode(heads(q), heads(k), heads(v), dil_bufs)
    xf = linear(o.reshape(b * t, DIL_HEADS * DIL_HD), W['dil_wo'], res=xf)
    k5 = k.reshape(b, t, n_grp, DIL_HEADS, DIL_HD)
    v5 = v.reshape(b, t, n_grp, DIL_HEADS, DIL_HD)
    states = []
    for g, (win, _) in enumerate(DIL_PATTERNS):
        keep = min(win, t)
        states += [k5[:, t - keep:, g], v5[:, t - keep:, g]]
    return xf, states


def _ssd_mixer(xf, b, t, ln, conv0, h0, W):
    d = D_MODEL
    xn = _pad_time(rmsnorm(xf, ln).reshape(b, t, d), SSM_CHUNK)
    tp = xn.shape[1]
    xn = xn.reshape(b * tp, d)
    z = linear(xn, W['ssm_win'], col0=0, n=SSM_DINNER)
    xbc = linear(xn, W['ssm_win'], col0=SSM_DINNER, n=SSM_CONV_DIM).reshape(b, tp, SSM_CONV_DIM)
    dt_raw = linear(xn, W['ssm_win'][:, SSM_DINNER + SSM_CONV_DIM:])
    prev8 = jnp.pad(conv0.astype(F32), ((0, 0), (8 - (SSM_CONV - 1), 0), (0, 0)))
    xa = causal_conv_silu(xbc, prev8, W['ssm_conv_w'], W['ssm_conv_b'])
    y, hf = ssd_scan(xa, z.reshape(b, tp, SSM_DINNER), dt_raw.reshape(b, tp, SSM_HEADS),
                     W['ssm_dt_bias'], W['ssm_a_log'], W['ssm_d'], W['ssm_norm_w'],
                     h0.astype(F32).reshape(b, SSM_DINNER, SSM_STATE), t_valid=t)
    xf = linear(y[:, :t].reshape(b * t, SSM_DINNER), W['ssm_wout'], res=xf)
    conv_state = jnp.concatenate([conv0.astype(F32), xbc[:, :t]], axis=1)[:, t:]
    return xf, conv_state, hf.reshape(b, SSM_HEADS, SSM_HEADDIM, SSM_STATE)


def _mixer_and_memory(xf, i, G, W):
    b, t, d = G['shape']
    st = G['out']
    kind = i % 4
    ln = W['ln_mix'][i]
    if kind == 0:
        xf, st['rwkv_shift'], st['rwkv_wkv'] = _rwkv_mixer(xf, b, t, ln, G['shift0'], G['wkv0'], W)
    elif kind == 1:
        xf, st['sb_k'], st['sb_v'] = _sb_mixer(xf, b, t, ln, G['sb_past'], W)
    elif kind == 2:
        xf, st['dil'] = _dil_mixer(xf, b, t, G['pos0'], ln, G['dil_bufs'], W)
    else:
        xf, st['ssm_conv'], st['ssm_h'] = _ssd_mixer(xf, b, t, ln, G['conv0'], G['h0'], W)
    xn = rmsnorm(xf, W['ln_mem'][i])
    q = linear(xn, W['mem_wq'], layer=i, gain=W['mem_gq'][i], hw=MEM_HD, out_dtype=BF16)
    q = _pad_time(q.reshape(b, t, d), 8)
    o = mem_attention(q, G['mem_k'], G['mem_v'], i)[:, :t].reshape(b * t, d)
    return linear(o, W['mem_wo'], layer=i, res=xf)


def _run_groups(P, S, W):
    xp, xs = [G['x'].reshape(-1, D_MODEL) for G in (P, S)]
    for i in range(DEPTH):
        xp, xs = ffn(xp, xs, W['ln_ffn1'][i], W['ffn1_gate'], W['ffn1_up'], W['ffn1_down'], i)
        xp = _mixer_and_memory(xp, i, P, W)
        xs = _mixer_and_memory(xs, i, S, W)
        xp, xs = ffn(xp, xs, W['ln_ffn2'][i], W['ffn2_gate'], W['ffn2_up'], W['ffn2_down'], i)
    return xp.reshape(P['shape']), xs.reshape(S['shape'])


def kernel(x_prompt, x_sample, state_rwkv_shift, state_rwkv_wkv, cache_sb_k, cache_sb_v, cache_dil0_k, cache_dil0_v, cache_dil1_k, cache_dil1_v, cache_dil2_k, cache_dil2_v, state_ssm_conv, state_ssm_h, cache_mem_k, cache_mem_v, page_table, mem_prompt, ln_ffn1, ffn1_gate, ffn1_up, ffn1_down, ln_mix, ln_mem, mem_wq, mem_gq, mem_wk, mem_gk, mem_wv, mem_wo, ln_ffn2, ffn2_gate, ffn2_up, ffn2_down, rwkv_mu, rwkv_wr, rwkv_wk, rwkv_wv, rwkv_wo, rwkv_w0, rwkv_w1, rwkv_w2, rwkv_a0, rwkv_a1, rwkv_a2, rwkv_g1, rwkv_g2, rwkv_kk, rwkv_ka, rwkv_rk, rwkv_gn_w, rwkv_gn_b, sb_wqkv, sb_gq, sb_gk, sb_bias, sb_wo, dil_wqkv, dil_gq, dil_gk, dil_wo, ssm_win, ssm_conv_w, ssm_conv_b, ssm_dt_bias, ssm_a_log, ssm_d, ssm_norm_w, ssm_wout):
    W = dict(locals())
    bp, _, d = x_prompt.shape
    n_mem = mem_prompt.shape[1]
    past_len = page_table.shape[1] * cache_sb_k.shape[1]

    mem_rows = mem_prompt.reshape(bp * n_mem, d)
    p_mem_k = jnp.stack([linear(mem_rows, mem_wk, layer=i, gain=mem_gk[i], hw=MEM_HD)
                         for i in range(DEPTH)]).reshape(DEPTH, bp, n_mem, d)
    p_mem_v = jnp.stack([linear(mem_rows, mem_wv, layer=i)
                         for i in range(DEPTH)]).reshape(DEPTH, bp, n_mem, d)
    P = dict(x=x_prompt, shape=x_prompt.shape, pos0=0, mem_k=p_mem_k, mem_v=p_mem_v,
             shift0=jnp.zeros((bp, d), F32), wkv0=jnp.zeros((bp, d // RWKV_HD, RWKV_HD, RWKV_HD), F32),
             sb_past=None, dil_bufs=None,
             conv0=jnp.zeros((bp, SSM_CONV - 1, SSM_CONV_DIM), F32),
             h0=jnp.zeros((bp, SSM_HEADS, SSM_HEADDIM, SSM_STATE), F32), out={})
    merged = lambda a: a.reshape(a.shape[:3] + (MEM_HEADS * MEM_HD,))
    S = dict(x=x_sample, shape=x_sample.shape, pos0=past_len,
             mem_k=merged(cache_mem_k), mem_v=merged(cache_mem_v),
             shift0=state_rwkv_shift, wkv0=state_rwkv_wkv, sb_past=(cache_sb_k, cache_sb_v, page_table),
             dil_bufs=(cache_dil0_k, cache_dil0_v, cache_dil1_k, cache_dil1_v, cache_dil2_k, cache_dil2_v),
             conv0=state_ssm_conv, h0=state_ssm_h, out={})
    y_p, y_s = _run_groups(P, S, W)
    sp, ss = P['out'], S['out']

    mem_shape = (DEPTH, bp, n_mem, MEM_HEADS, MEM_HD)
    dil = []
    for g in range(len(DIL_PATTERNS)):
        dil += [sp['dil'][2 * g], sp['dil'][2 * g + 1], ss['dil'][2 * g], ss['dil'][2 * g + 1]]
    return (y_p, y_s,
            sp['rwkv_shift'], ss['rwkv_shift'], sp['rwkv_wkv'], ss['rwkv_wkv'],
            sp['sb_k'], sp['sb_v'], ss['sb_k'], ss['sb_v'],
            *dil,
            sp['ssm_conv'], ss['ssm_conv'], sp['ssm_h'], ss['ssm_h'],
            p_mem_k.reshape(mem_shape), p_mem_v.reshape(mem_shape))
```
